```python
import math
import jax, jax.numpy as jnp
from jax import lax
import numpy as np

D_MODEL = 1024
BATCH = 2
SEQ = 8192
DEPTH = 1

HEAD_DIM = 64
N_DIFF_HEADS = D_MODEL // 256
DIFF_WIDTH = N_DIFF_HEADS * 2 * HEAD_DIM
N_NAT_HEADS = D_MODEL // 128
NAT_WIDTH = N_NAT_HEADS * HEAD_DIM
MIX_WIDTH = DIFF_WIDTH + NAT_WIDTH
GRID_W = 64
NAT_KH_MAX = 8
NAT_KW = 16
Q_BLOCK = 128
N_EXPERTS = 32
TOP_K = 4
D_FF_EXPERT = D_MODEL
SWIGLU_LIMIT = 7.0
SWIGLU_ALPHA = 1.702
MOE_BLOCK = 128
RMS_EPS = 1e-6

kernel_name = "hybrid_diffattn_natten_moe_encoder"


def rms_norm(x, g):
    xf = x.astype(jnp.float32)
    y = xf * lax.rsqrt(jnp.mean(xf * xf, axis=-1, keepdims=True) + RMS_EPS)
    return (y * g.astype(jnp.float32)).astype(x.dtype)


def alibi_slopes(n_heads):
    i = jnp.arange(1, n_heads + 1, dtype=jnp.float32)
    return jnp.exp2(-8.0 * i / n_heads)


def diff_attention(q, k, v, lam, lam_init, g_sub):
    B, L, H, _, Dh = q.shape
    nb = L // Q_BLOCK
    slopes = alibi_slopes(H)
    kpos = jnp.arange(L, dtype=jnp.float32)
    scale = Dh ** -0.5
    qb = q.reshape(B, nb, Q_BLOCK, H, 2, Dh).transpose(1, 0, 2, 3, 4, 5)

    def block(args):
        q_blk, i = args
        s = jnp.einsum('bqhmd,bkhmd->bhmqk', q_blk, k).astype(jnp.float32) * scale
        qpos = (i * Q_BLOCK + jnp.arange(Q_BLOCK)).astype(jnp.float32)
        dist = jnp.abs(qpos[:, None] - kpos[None, :])
        s = s - slopes[None, :, None, None, None] * dist[None, None, None]
        p = jax.nn.softmax(s, axis=-1)
        a = p[:, :, 0] - lam * p[:, :, 1]
        return jnp.einsum('bhqk,bkhe->bqhe', a.astype(v.dtype), v)

    o = lax.map(block, (qb, jnp.arange(nb)))
    o = o.transpose(1, 0, 2, 3, 4).reshape(B, L, H, 2 * Dh)
    o = rms_norm(o, g_sub) * (1.0 - lam_init)
    return o.reshape(B, L, H * 2 * Dh)


def neighbourhood_attention(q, k, v, rpb):
    B, L, H, Dh = q.shape
    rows = L // GRID_W
    kh = min(NAT_KH_MAX, rows)
    scale = Dh ** -0.5
    qg = q.reshape(B, rows, GRID_W, H, Dh)
    kg = k.reshape(B, rows, GRID_W, H, Dh)
    vg = v.reshape(B, rows, GRID_W, H, Dh)
    r = jnp.arange(rows)
    row_start = jnp.clip(r - kh // 2, 0, rows - kh)
    row_idx = row_start[:, None] + jnp.arange(kh)[None, :]
    k_rows = kg[:, row_idx]
    v_rows = vg[:, row_idx]
    c = jnp.arange(GRID_W)
    col_start = jnp.clip(c - NAT_KW // 2, 0, GRID_W - NAT_KW)
    col_in = (c[None, :] >= col_start[:, None]) & (c[None, :] < col_start[:, None] + NAT_KW)
    dr = row_idx - r[:, None] + (NAT_KH_MAX - 1)
    dc = jnp.clip(c[None, :] - c[:, None], -(NAT_KW - 1), NAT_KW - 1) + (NAT_KW - 1)
    bias = rpb.astype(jnp.float32)[:, dr[:, None, :, None], dc[None, :, None, :]]
    s = jnp.einsum('brqhd,brkwhd->bhrqkw', qg, k_rows).astype(jnp.float32) * scale + bias[None]
    s = jnp.where(col_in[:, None, :], s, -jnp.inf)
    p = jax.nn.softmax(s.reshape(B, H, rows, GRID_W, kh * GRID_W), axis=-1)
    p = p.reshape(B, H, rows, GRID_W, kh, GRID_W).astype(v.dtype)
    o = jnp.einsum('bhrqkw,brkwhd->brqhd', p, v_rows)
    return o.reshape(B, L, H * Dh)


def swiglu_clamped(hh):
    x_glu = jnp.minimum(hh[..., ::2], SWIGLU_LIMIT)
    x_lin = jnp.clip(hh[..., 1::2], -SWIGLU_LIMIT, SWIGLU_LIMIT)
    return x_glu * jax.nn.sigmoid(SWIGLU_ALPHA * x_glu) * (x_lin + 1.0)


def moe_ffn(h, w_router, b_router, w1, b1, w2, b2):
    B, L, D = h.shape
    T = B * L
    hf = h.reshape(T, D)
    logits = (hf @ w_router + b_router).astype(jnp.float32)
    top_val, top_idx = lax.top_k(logits, TOP_K)
    gates = jax.nn.softmax(top_val, axis=-1)
    n_assign = T * TOP_K
    flat_e = top_idx.reshape(-1).astype(jnp.int32)
    flat_tok = jnp.arange(n_assign, dtype=jnp.int32) // TOP_K
    order = jnp.argsort(flat_e, stable=True)
    sorted_e = flat_e[order]
    counts = jnp.bincount(flat_e, length=N_EXPERTS).astype(jnp.int32)
    padded = (counts + MOE_BLOCK - 1) // MOE_BLOCK * MOE_BLOCK
    pad_end = jnp.cumsum(padded)
    pad_start = pad_end - padded
    grp_start = jnp.cumsum(counts) - counts
    rank = jnp.arange(n_assign, dtype=jnp.int32) - grp_start[sorted_e]
    dest_sorted = (pad_start[sorted_e] + rank).astype(jnp.int32)
    cap = (n_assign + N_EXPERTS * (MOE_BLOCK - 1) + MOE_BLOCK - 1) // MOE_BLOCK * MOE_BLOCK
    n_blocks = cap // MOE_BLOCK
    tok_buf = jnp.zeros((cap,), jnp.int32).at[dest_sorted].set(flat_tok[order])
    blk_start = jnp.arange(n_blocks, dtype=jnp.int32) * MOE_BLOCK
    blk_expert = jnp.minimum(jnp.searchsorted(pad_end, blk_start, side='right'), N_EXPERTS - 1)
    x_buf = hf[tok_buf].reshape(n_blocks, MOE_BLOCK, D)

    def expert_block(args):
        xb, e = args
        hh = xb @ w1[e] + b1[e]
        return swiglu_clamped(hh) @ w2[e] + b2[e]

    y_buf = lax.map(expert_block, (x_buf, blk_expert)).reshape(cap, D)
    dest = jnp.zeros((n_assign,), jnp.int32).at[order].set(dest_sorted)
    y = y_buf[dest].reshape(T, TOP_K, D)
    out = jnp.einsum('tk,tkd->td', gates.astype(y.dtype), y)
    return out.reshape(B, L, D)


def setup_inputs(seed: int = 0) -> dict:
    key = jax.random.key(seed)
    ks = jax.random.split(key, 24)
    f32 = jnp.float32
    D, F = D_MODEL, D_FF_EXPERT
    n = lambda k, shape, s: jax.random.normal(k, shape, f32) * s
    return {
        "x": n(ks[0], (BATCH, SEQ, D), 1.0),
        "c": n(ks[1], (BATCH, D), 1.0),
        "w_ada": n(ks[2], (DEPTH, D, 6 * D), 0.5 * D ** -0.5),
        "b_ada": n(ks[3], (DEPTH, 6 * D), 0.02),
        "g_pre_mix": 1.0 + n(ks[4], (DEPTH, D), 0.02),
        "g_post_mix": 1.0 + n(ks[5], (DEPTH, D), 0.02),
        "w_in": n(ks[6], (DEPTH, D, 3 * MIX_WIDTH), D ** -0.5),
        "w_out": n(ks[7], (DEPTH, MIX_WIDTH, D), MIX_WIDTH ** -0.5),
        "lam_q1": n(ks[8], (DEPTH, HEAD_DIM), 0.1),
        "lam_k1": n(ks[9], (DEPTH, HEAD_DIM), 0.1),
        "lam_q2": n(ks[10], (DEPTH, HEAD_DIM), 0.1),
        "lam_k2": n(ks[11], (DEPTH, HEAD_DIM), 0.1),
        "g_subln": 1.0 + n(ks[12], (DEPTH, 2 * HEAD_DIM), 0.02),
        "nat_rpb": n(ks[13], (DEPTH, N_NAT_HEADS, 2 * NAT_KH_MAX - 1, 2 * NAT_KW - 1), 0.02),
        "g_pre_ffn": 1.0 + n(ks[14], (DEPTH, D), 0.02),
        "g_post_ffn": 1.0 + n(ks[15], (DEPTH, D), 0.02),
        "w_router": n(ks[16], (DEPTH, D, N_EXPERTS), D ** -0.5),
        "b_router": n(ks[17], (DEPTH, N_EXPERTS), 0.01),
        "w1": n(ks[18], (DEPTH, N_EXPERTS, D, 2 * F), D ** -0.5),
        "b1": n(ks[19], (DEPTH, N_EXPERTS, 2 * F), 0.01),
        "w2": n(ks[20], (DEPTH, N_EXPERTS, F, D), F ** -0.5),
        "b2": n(ks[21], (DEPTH, N_EXPERTS, D), 0.01),
    }


def reference(x, c, w_ada, b_ada, g_pre_mix, g_post_mix, w_in, w_out,
              lam_q1, lam_k1, lam_q2, lam_k2, g_subln, nat_rpb,
              g_pre_ffn, g_post_ffn, w_router, b_router, w1, b1, w2, b2):
    B, L, D = x.shape
    splits = [DIFF_WIDTH, 2 * DIFF_WIDTH, 3 * DIFF_WIDTH,
              3 * DIFF_WIDTH + NAT_WIDTH, 3 * DIFF_WIDTH + 2 * NAT_WIDTH]
    for l in range(DEPTH):
        lam_init = 0.8 - 0.6 * math.exp(-0.3 * l)
        mod = jax.nn.silu(c) @ w_ada[l] + b_ada[l]
        sh1, sc1, gt1, sh2, sc2, gt2 = jnp.split(mod[:, None, :], 6, axis=-1)

        h = rms_norm(x, g_pre_mix[l]) * (1.0 + sc1) + sh1
        proj = h @ w_in[l]
        dq, dk, dv, nq, nk, nv = jnp.split(proj, splits, axis=-1)
        dq = dq.reshape(B, L, N_DIFF_HEADS, 2, HEAD_DIM)
        dk = dk.reshape(B, L, N_DIFF_HEADS, 2, HEAD_DIM)
        dv = dv.reshape(B, L, N_DIFF_HEADS, 2 * HEAD_DIM)
        lam = (jnp.exp(jnp.sum(lam_q1[l].astype(jnp.float32) * lam_k1[l].astype(jnp.float32)))
               - jnp.exp(jnp.sum(lam_q2[l].astype(jnp.float32) * lam_k2[l].astype(jnp.float32)))
               + lam_init)
        o_diff = diff_attention(dq, dk, dv, lam, lam_init, g_subln[l])
        o_nat = neighbourhood_attention(nq.reshape(B, L, N_NAT_HEADS, HEAD_DIM),
                                        nk.reshape(B, L, N_NAT_HEADS, HEAD_DIM),
                                        nv.reshape(B, L, N_NAT_HEADS, HEAD_DIM),
                                        nat_rpb[l])
        mix = jnp.concatenate([o_diff, o_nat], axis=-1) @ w_out[l]
        x = x + gt1 * rms_norm(mix, g_post_mix[l])

        h2 = rms_norm(x, g_pre_ffn[l]) * (1.0 + sc2) + sh2
        f = moe_ffn(h2, w_router[l], b_router[l], w1[l], b1[l], w2[l], b2[l])
        x = x + gt2 * rms_norm(f, g_post_ffn[l])
    return x
```

```python
import functools
import math

import jax
import jax.numpy as jnp
from jax import lax
from jax.experimental import pallas as pl
from jax.experimental.pallas import tpu as pltpu

F32 = jnp.float32
BF16 = jnp.bfloat16
I32 = jnp.int32

HEAD_DIM = 64
N_DIFF_HEADS = 4
DIFF_HEAD_W = 2 * HEAD_DIM
DIFF_WIDTH = N_DIFF_HEADS * DIFF_HEAD_W
N_NAT_HEADS = 8
NAT_WIDTH = N_NAT_HEADS * HEAD_DIM
NAT_PAIRS = N_NAT_HEADS // 2
GRID_W = 64
NAT_KH = 8
NAT_KW = 16
N_EXPERTS = 32
TOP_K = 4
SWIGLU_LIMIT = 7.0
SWIGLU_ALPHA = 1.702
RMS_EPS = 1e-6
NEG_BIG = -1e30

NT_DIMS = (((1,), (1,)), ((), ()))

ADA_TN = 1536
INPROJ_TM = 512
DIFF_TQ = 256
DIFF_TK = 512
NAT_ROWS_PER_STEP = 8
OUT_TM = 512
MOE_BM = 256
COMBINE_TM = 512
VMEM_LIMIT = 48 * 1024 * 1024


def _rms(x, axis=-1):
    return x * lax.rsqrt(jnp.mean(x * x, axis=axis, keepdims=True) + RMS_EPS)


def _ada_kernel(lam_init, c_ref, w_ref, b_ref, lq1_ref, lk1_ref, lq2_ref, lk2_ref, mod_ref, lam_ref):
    c = c_ref[...]
    s = c * (1.0 / (1.0 + jnp.exp(-c)))
    mod_ref[...] = jnp.dot(s, w_ref[...], preferred_element_type=F32,
                           precision=lax.Precision.HIGHEST) + b_ref[...]
    d1 = jnp.sum(lq1_ref[...] * lk1_ref[...], axis=-1, keepdims=True)
    d2 = jnp.sum(lq2_ref[...] * lk2_ref[...], axis=-1, keepdims=True)
    lam = jnp.exp(d1) - jnp.exp(d2) + lam_init
    lam_ref[...] = jnp.broadcast_to(lam, lam_ref.shape)


def _ada(c, w_ada, b_ada, lq1, lk1, lq2, lk2, lam_init):
    B, D = c.shape
    N = w_ada.shape[1]
    c8 = jnp.zeros((8, D), F32).at[:B].set(c)
    vec = pl.BlockSpec((1, HEAD_DIM), lambda j: (0, 0))
    mod, lam = pl.pallas_call(
        functools.partial(_ada_kernel, lam_init),
        grid=(N // ADA_TN,),
        in_specs=[pl.BlockSpec((8, D), lambda j: (0, 0)),
                  pl.BlockSpec((D, ADA_TN), lambda j: (0, j)),
                  pl.BlockSpec((1, ADA_TN), lambda j: (0, j)),
                  vec, vec, vec, vec],
        out_specs=[pl.BlockSpec((8, ADA_TN), lambda j: (0, j)),
                   pl.BlockSpec((8, 128), lambda j: (0, 0))],
        out_shape=[jax.ShapeDtypeStruct((8, N), F32), jax.ShapeDtypeStruct((8, 128), F32)],
        compiler_params=pltpu.CompilerParams(dimension_semantics=("arbitrary",),
                                             vmem_limit_bytes=VMEM_LIMIT),
        name="ada",
    )(c8, w_ada, b_ada.reshape(1, N), lq1.reshape(1, -1), lk1.reshape(1, -1),
      lq2.reshape(1, -1), lk2.reshape(1, -1))
    return mod[:B], lam


def _inproj_kernel(x_ref, g_ref, sc_ref, sh_ref, wqT_ref, wvT_ref, wn_ref,
                   qT_ref, k_ref, vT_ref, nq_ref, nk_ref, nv_ref):
    h = _rms(x_ref[0]) * g_ref[...]
    h = h * (1.0 + sc_ref[0]) + sh_ref[0]
    hb = h.astype(BF16)
    qT_ref[0] = lax.dot_general(wqT_ref[...], hb, NT_DIMS, preferred_element_type=F32).astype(BF16)
    vT_ref[0] = lax.dot_general(wvT_ref[...], hb, NT_DIMS, preferred_element_type=F32).astype(BF16)
    rest = jnp.dot(hb, wn_ref[...], preferred_element_type=F32).astype(BF16)
    k_ref[0] = rest[:, 0:512]
    nq_ref[0] = rest[:, 512:1024]
    nk_ref[0] = rest[:, 1024:1536]
    nv_ref[0] = rest[:, 1536:2048]


def _inproj(x, g_pre, sc1, sh1, w_in):
    B, L, D = x.shape
    tm = INPROJ_TM
    scale = HEAD_DIM ** -0.5
    wqT = (w_in[:, 0:512] * scale).T.astype(BF16)
    wvT = w_in[:, 1024:1536].T.astype(BF16)
    wn = jnp.concatenate([w_in[:, 512:1024], w_in[:, 1536:2048] * scale, w_in[:, 2048:3072]],
                         axis=1).astype(BF16)
    row_major = pl.BlockSpec((1, tm, 512), lambda b, i: (b, i, 0))
    col_major = pl.BlockSpec((1, 512, tm), lambda b, i: (b, 0, i))
    modv = pl.BlockSpec((1, 1, D), lambda b, i: (b, 0, 0))
    rm_shape = jax.ShapeDtypeStruct((B, L, 512), BF16)
    cm_shape = jax.ShapeDtypeStruct((B, 512, L), BF16)
    return pl.pallas_call(
        _inproj_kernel,
        grid=(B, L // tm),
        in_specs=[pl.BlockSpec((1, tm, D), lambda b, i: (b, i, 0)),
                  pl.BlockSpec((1, D), lambda b, i: (0, 0)),
                  modv, modv,
                  pl.BlockSpec((512, D), lambda b, i: (0, 0)),
                  pl.BlockSpec((512, D), lambda b, i: (0, 0)),
                  pl.BlockSpec((D, 2048), lambda b, i: (0, 0))],
        out_specs=[col_major, row_major, col_major, row_major, row_major, row_major],
        out_shape=[cm_shape, rm_shape, cm_shape, rm_shape, rm_shape, rm_shape],
        compiler_params=pltpu.CompilerParams(dimension_semantics=("arbitrary", "arbitrary"),
                                             vmem_limit_bytes=VMEM_LIMIT),
        name="inproj",
    )(x, g_pre.reshape(1, D), sc1.reshape(B, 1, D), sh1.reshape(B, 1, D), wqT, wvT, wn)


def _diff_kernel(lam_init, n_kchunks, slopes_ref, lam_ref, qT_ref, k_ref, vT_ref, g_ref, o_ref,
                 acc_ref, m_ref, l_ref, s0_ref):
    tq = qT_ref.shape[2]
    tk = s0_ref.shape[0]
    h = pl.program_id(1)
    q0 = pl.program_id(2) * tq
    slope = slopes_ref[h]

    qT = qT_ref[0]
    row = lax.broadcasted_iota(I32, qT.shape, 0)
    zero = jnp.zeros_like(qT)
    qT_maps = (jnp.where(row < HEAD_DIM, qT, zero), jnp.where(row >= HEAD_DIM, qT, zero))

    kk = lax.broadcasted_iota(I32, (tk, tq), 0)
    qq = lax.broadcasted_iota(I32, (tk, tq), 1)
    s0_ref[...] = slope * (kk - qq).astype(F32)
    acc_ref[...] = jnp.zeros_like(acc_ref)
    l_ref[...] = jnp.zeros_like(l_ref)
    m_ref[...] = jnp.full_like(m_ref, NEG_BIG)

    def body(kc, carry):
        k0 = pl.multiple_of(kc * tk, tk)
        kb = k_ref[0, pl.ds(k0, tk), :]
        vb = vT_ref[0, :, pl.ds(k0, tk)]
        bias = jnp.abs(s0_ref[...] + slope * (k0 - q0).astype(F32))
        for mi in range(2):
            s = jnp.dot(kb, qT_maps[mi], preferred_element_type=F32) - bias
            m_old = m_ref[mi]
            m_new = jnp.maximum(m_old, jnp.max(s, axis=0, keepdims=True))
            alpha = jnp.exp(m_old - m_new)
            p = jnp.exp(s - m_new)
            l_ref[mi] = alpha * l_ref[mi] + jnp.sum(p, axis=0, keepdims=True)
            acc_ref[mi] = alpha * acc_ref[mi] + jnp.dot(vb, p.astype(BF16), preferred_element_type=F32)
            m_ref[mi] = m_new
        return carry

    lax.fori_loop(0, n_kchunks, body, 0)

    lam = lam_ref[0:1, 0:1]
    o = acc_ref[0] / l_ref[0] - lam * (acc_ref[1] / l_ref[1])
    y = _rms(o, axis=0) * g_ref[...] * (1.0 - lam_init)
    o_ref[0] = y.T.astype(BF16)


def _diff_attention(qT, k, vT, lam, g_sub, lam_init):
    B, _, L = qT.shape
    tq, tk = min(DIFF_TQ, L), min(DIFF_TK, L)
    i = jnp.arange(1, N_DIFF_HEADS + 1, dtype=F32)
    slopes = jnp.exp2(-8.0 * i / N_DIFF_HEADS)
    return pl.pallas_call(
        functools.partial(_diff_kernel, lam_init, L // tk),
        grid=(B, N_DIFF_HEADS, L // tq),
        in_specs=[pl.BlockSpec(memory_space=pltpu.SMEM),
                  pl.BlockSpec((8, 128), lambda b, h, i: (0, 0)),
                  pl.BlockSpec((1, DIFF_HEAD_W, tq), lambda b, h, i: (b, h, i)),
                  pl.BlockSpec((1, L, DIFF_HEAD_W), lambda b, h, i: (b, 0, h)),
                  pl.BlockSpec((1, DIFF_HEAD_W, L), lambda b, h, i: (b, h, 0)),
                  pl.BlockSpec((DIFF_HEAD_W, 1), lambda b, h, i: (0, 0))],
        out_specs=pl.BlockSpec((1, tq, DIFF_HEAD_W), lambda b, h, i: (b, i, h)),
        out_shape=jax.ShapeDtypeStruct((B, L, DIFF_WIDTH), BF16),
        scratch_shapes=[pltpu.VMEM((2, DIFF_HEAD_W, tq), F32),
                        pltpu.VMEM((2, 1, tq), F32),
                        pltpu.VMEM((2, 1, tq), F32),
                        pltpu.VMEM((tk, tq), F32)],
        compiler_params=pltpu.CompilerParams(
            dimension_semantics=("arbitrary", "arbitrary", "arbitrary"),
            vmem_limit_bytes=VMEM_LIMIT),
        name="diff_attn",
    )(slopes, lam, qT, k, vT, g_sub.reshape(DIFF_HEAD_W, 1))


def _nat_bias_table(rpb):
    c = jnp.arange(GRID_W)
    col_start = jnp.clip(c - NAT_KW // 2, 0, GRID_W - NAT_KW)
    col_in = (c[None, :] >= col_start[:, None]) & (c[None, :] < col_start[:, None] + NAT_KW)
    dc = jnp.clip(c[None, :] - c[:, None], -(NAT_KW - 1), NAT_KW - 1) + (NAT_KW - 1)
    var = jnp.arange(NAT_KH)
    dr = jnp.arange(NAT_KH)[None, :] - var[:, None] + (NAT_KH - 1)
    tbl = rpb.astype(F32)[:, dr[:, None, :, None], dc[None, :, None, :]]
    tbl = jnp.where(col_in[None, None, :, None, :], tbl, NEG_BIG)
    return tbl.reshape(NAT_PAIRS, 2, NAT_KH, GRID_W, NAT_KH * GRID_W)


def _nat_kernel(n_rows, q_ref, k_ref, v_ref, bias_ref, o_ref):
    ri = pl.program_id(2)
    rows_per_step = q_ref.shape[1] // GRID_W
    win = NAT_KH * GRID_W
    lane = lax.broadcasted_iota(I32, (GRID_W, 2 * HEAD_DIM), 1)
    for j in range(rows_per_step):
        r = ri * rows_per_step + j
        rs = jnp.clip(r - NAT_KH // 2, 0, n_rows - NAT_KH)
        var = r - rs
        k0 = pl.multiple_of(rs * GRID_W, GRID_W)
        kw = k_ref[0, pl.ds(k0, win), :]
        vw = v_ref[0, pl.ds(k0, win), :]
        q = q_ref[0, j * GRID_W:(j + 1) * GRID_W, :]
        zero = jnp.zeros_like(q)
        outs = []
        for hh in range(2):
            keep = (lane < HEAD_DIM) if hh == 0 else (lane >= HEAD_DIM)
            s = lax.dot_general(jnp.where(keep, q, zero), kw, NT_DIMS, preferred_element_type=F32)
            s = s + bias_ref[0, hh, var]
            p = jnp.exp(s - jnp.max(s, axis=-1, keepdims=True))
            l = jnp.sum(p, axis=-1, keepdims=True)
            outs.append(jnp.dot(p.astype(BF16), vw, preferred_element_type=F32) / l)
        o_ref[0, j * GRID_W:(j + 1) * GRID_W, :] = jnp.where(lane < HEAD_DIM, outs[0], outs[1]).astype(BF16)


def _nat_attention(nq, nk, nv, rpb):
    B, L, _ = nq.shape
    n_rows = L // GRID_W
    assert n_rows >= NAT_KH
    rps = min(NAT_ROWS_PER_STEP, n_rows)
    tbl = _nat_bias_table(rpb)
    kv = pl.BlockSpec((1, L, 2 * HEAD_DIM), lambda b, p, i: (b, 0, p))
    qo = pl.BlockSpec((1, rps * GRID_W, 2 * HEAD_DIM), lambda b, p, i: (b, i, p))
    return pl.pallas_call(
        functools.partial(_nat_kernel, n_rows),
        grid=(B, NAT_PAIRS, n_rows // rps),
        in_specs=[qo, kv, kv,
                  pl.BlockSpec((1, 2, NAT_KH, GRID_W, NAT_KH * GRID_W), lambda b, p, i: (p, 0, 0, 0, 0))],
        out_specs=qo,
        out_shape=jax.ShapeDtypeStruct((B, L, NAT_WIDTH), BF16),
        compiler_params=pltpu.CompilerParams(
            dimension_semantics=("arbitrary", "arbitrary", "arbitrary"),
            vmem_limit_bytes=VMEM_LIMIT),
        name="nat_attn",
    )(nq, nk, nv, tbl)


def _out_kernel(od_ref, on_ref, x_ref, wt_ref, wb_ref, gpost_ref, gt1_ref, gpre_ref, sc2_ref, sh2_ref,
                wr_ref, br_ref, x1_ref, h2_ref, eidx_ref, gate_ref, rank_ref, cnt_ref, carry_ref):
    tm = x_ref.shape[0]

    @pl.when(pl.program_id(0) == 0)
    def _():
        carry_ref[...] = jnp.zeros_like(carry_ref)

    mix = (jnp.dot(od_ref[...], wt_ref[...], preferred_element_type=F32)
           + jnp.dot(on_ref[...], wb_ref[...], preferred_element_type=F32))
    x1 = x_ref[...] + gt1_ref[0] * (_rms(mix) * gpost_ref[...])
    x1_ref[...] = x1
    h2 = _rms(x1) * gpre_ref[...]
    h2 = h2 * (1.0 + sc2_ref[0]) + sh2_ref[0]
    h2_ref[...] = h2

    logits = jnp.dot(h2, wr_ref[...], preferred_element_type=F32,
                     precision=lax.Precision.HIGHEST) + br_ref[...]
    eio = lax.broadcasted_iota(I32, logits.shape, 1).astype(F32)
    onehot = jnp.zeros_like(logits)
    vals, idxs, sels = [], [], []
    cur = logits
    for _ in range(TOP_K):
        mx = jnp.max(cur, axis=-1, keepdims=True)
        idx = jnp.min(jnp.where(cur == mx, eio, float(N_EXPERTS)), axis=-1, keepdims=True)
        sel = eio == idx
        vals.append(mx)
        idxs.append(idx)
        sels.append(sel)
        cur = jnp.where(sel, -jnp.inf, cur)
        onehot = onehot + sel.astype(F32)

    ex = [jnp.exp(v - vals[0]) for v in vals]
    tot = ex[0] + ex[1] + ex[2] + ex[3]

    rr = lax.broadcasted_iota(I32, (tm, tm), 0)
    cc = lax.broadcasted_iota(I32, (tm, tm), 1)
    tri = (rr > cc).astype(BF16)
    carry = carry_ref[...]
    cum = jnp.dot(tri, onehot.astype(BF16), preferred_element_type=F32) + carry
    ranks = [jnp.sum(jnp.where(sel, cum, 0.0), axis=-1, keepdims=True) for sel in sels]
    carry = carry + jnp.sum(onehot, axis=0, keepdims=True)
    carry_ref[...] = carry
    cnt_ref[...] = carry

    kio = lax.broadcasted_iota(I32, (tm, TOP_K), 1)

    def pack(cols):
        out = jnp.broadcast_to(cols[TOP_K - 1], (tm, TOP_K))
        for k in range(TOP_K - 2, -1, -1):
            out = jnp.where(kio == k, cols[k], out)
        return out

    eidx_ref[...] = pack(idxs).astype(I32)
    gate_ref[...] = pack([e / tot for e in ex])
    rank_ref[...] = pack(ranks).astype(I32)


def _out_router(od, on, x, w_out, g_post, gt1, g_pre, sc2, sh2, w_router, b_router, tokens_per_batch):
    T, D = x.shape
    B = gt1.shape[0]
    tm = min(OUT_TM, tokens_per_batch)
    steps_per_batch = tokens_per_batch // tm
    wt = w_out[:DIFF_WIDTH].astype(BF16)
    wb = w_out[DIFF_WIDTH:].astype(BF16)
    rowblk = lambda w: pl.BlockSpec((tm, w), lambda i: (i, 0))
    const = lambda shape: pl.BlockSpec(shape, lambda i: (0,) * len(shape))
    modv = pl.BlockSpec((1, 1, D), lambda i: (i // steps_per_batch, 0, 0))
    return pl.pallas_call(
        _out_kernel,
        grid=(T // tm,),
        in_specs=[rowblk(DIFF_WIDTH), rowblk(NAT_WIDTH), rowblk(D),
                  const((DIFF_WIDTH, D)), const((NAT_WIDTH, D)), const((1, D)), modv, const((1, D)),
                  modv, modv, const((D, N_EXPERTS)), const((1, N_EXPERTS))],
        out_specs=[rowblk(D), rowblk(D), rowblk(TOP_K), rowblk(TOP_K), rowblk(TOP_K),
                   const((1, N_EXPERTS))],
        out_shape=[jax.ShapeDtypeStruct((T, D), F32), jax.ShapeDtypeStruct((T, D), F32),
                   jax.ShapeDtypeStruct((T, TOP_K), I32), jax.ShapeDtypeStruct((T, TOP_K), F32),
                   jax.ShapeDtypeStruct((T, TOP_K), I32), jax.ShapeDtypeStruct((1, N_EXPERTS), F32)],
        scratch_shapes=[pltpu.VMEM((1, N_EXPERTS), F32)],
        compiler_params=pltpu.CompilerParams(dimension_semantics=("arbitrary",),
                                             vmem_limit_bytes=VMEM_LIMIT),
        name="out_router",
    )(od, on, x, wt, wb, g_post.reshape(1, D), gt1.reshape(B, 1, D), g_pre.reshape(1, D),
      sc2.reshape(B, 1, D), sh2.reshape(B, 1, D), w_router, b_router.reshape(1, N_EXPERTS))


def _expert_kernel(blk_e_ref, tok_ref, tok_next_ref, dst_ref, h2_hbm, w1_ref, b1_ref, w2_ref, b2_ref,
                   y_hbm, xbuf, ybuf, gsem, ssem):
    i = pl.program_id(0)
    n = pl.num_programs(0)
    bm = xbuf.shape[1]
    slot = i % 2

    def gather_row(idx_ref, s, r):
        return pltpu.make_async_copy(h2_hbm.at[pl.ds(idx_ref[0, 0, r], 1), :],
                                     xbuf.at[s, pl.ds(r, 1), :], gsem.at[s])

    def scatter_row(r):
        return pltpu.make_async_copy(ybuf.at[pl.ds(r, 1), :],
                                     y_hbm.at[pl.ds(dst_ref[0, 0, r], 1), :], ssem.at[0])

    def start_gather(idx_ref, s):
        def go(r, c):
            gather_row(idx_ref, s, r).start()
            return c
        lax.fori_loop(0, bm, go, 0, unroll=8)

    @pl.when(i == 0)
    def _():
        start_gather(tok_ref, 0)

    @pl.when(i + 1 < n)
    def _():
        start_gather(tok_next_ref, 1 - slot)

    def wait_gather(r, c):
        gather_row(tok_ref, slot, r).wait()
        return c
    lax.fori_loop(0, bm, wait_gather, 0, unroll=8)

    x = xbuf[slot].astype(BF16)
    hh = jnp.dot(x, w1_ref[0], preferred_element_type=F32) + b1_ref[0]
    f = hh.shape[1] // 2
    glu = jnp.minimum(hh[:, :f], SWIGLU_LIMIT)
    lin = jnp.clip(hh[:, f:], -SWIGLU_LIMIT, SWIGLU_LIMIT)
    act = glu * (1.0 / (1.0 + jnp.exp(-SWIGLU_ALPHA * glu))) * (lin + 1.0)
    y = jnp.dot(act.astype(BF16), w2_ref[0], preferred_element_type=F32) + b2_ref[0]

    def wait_scatter(r, c):
        scatter_row(r).wait()
        return c

    @pl.when(i > 0)
    def _():
        lax.fori_loop(0, bm, wait_scatter, 0, unroll=8)

    ybuf[...] = y

    def go_scatter(r, c):
        scatter_row(r).start()
        return c
    lax.fori_loop(0, bm, go_scatter, 0, unroll=8)

    @pl.when(i == n - 1)
    def _():
        lax.fori_loop(0, bm, wait_scatter, 0, unroll=8)


def _experts(h2, blk_expert, tok_buf, dst_buf, w1, b1, w2, b2, n_rows_out):
    T, D = h2.shape
    n_blocks = blk_expert.shape[0]
    bm = MOE_BM
    F = w2.shape[1]
    w1p = jnp.concatenate([w1[..., 0::2], w1[..., 1::2]], axis=-1).astype(BF16)
    b1p = jnp.concatenate([b1[..., 0::2], b1[..., 1::2]], axis=-1).reshape(N_EXPERTS, 1, 2 * F)
    w2b = w2.astype(BF16)
    tok3 = tok_buf.reshape(n_blocks, 1, bm)
    dst3 = dst_buf.reshape(n_blocks, 1, bm)
    smem_blk = lambda fn: pl.BlockSpec((1, 1, bm), fn, memory_space=pltpu.SMEM)
    grid_spec = pltpu.PrefetchScalarGridSpec(
        num_scalar_prefetch=1,
        grid=(n_blocks,),
        in_specs=[smem_blk(lambda i, be: (i, 0, 0)),
                  smem_blk(lambda i, be: (jnp.minimum(i + 1, n_blocks - 1), 0, 0)),
                  smem_blk(lambda i, be: (i, 0, 0)),
                  pl.BlockSpec(memory_space=pl.ANY),
                  pl.BlockSpec((1, D, 2 * F), lambda i, be: (be[i], 0, 0)),
                  pl.BlockSpec((1, 1, 2 * F), lambda i, be: (be[i], 0, 0)),
                  pl.BlockSpec((1, F, D), lambda i, be: (be[i], 0, 0)),
                  pl.BlockSpec((1, 1, D), lambda i, be: (be[i], 0, 0))],
        out_specs=pl.BlockSpec(memory_space=pl.ANY),
        scratch_shapes=[pltpu.VMEM((2, bm, D), F32), pltpu.VMEM((bm, D), F32),
                        pltpu.SemaphoreType.DMA((2,)), pltpu.SemaphoreType.DMA((1,))],
    )
    return pl.pallas_call(
        _expert_kernel,
        grid_spec=grid_spec,
        out_shape=jax.ShapeDtypeStruct((n_rows_out, D), F32),
        compiler_params=pltpu.CompilerParams(dimension_semantics=("arbitrary",),
                                             vmem_limit_bytes=VMEM_LIMIT),
        name="experts",
    )(blk_expert, tok3, tok3, dst3, h2, w1p, b1p, w2b, b2.reshape(N_EXPERTS, 1, D))


def _combine_kernel(y_ref, gate_ref, x1_ref, gt2_ref, g_ref, o_ref):
    D = x1_ref.shape[1]
    gates = gate_ref[...]
    f = gates[:, 0:1] * y_ref[:, 0:D]
    for k in range(1, TOP_K):
        f = f + gates[:, k:k + 1] * y_ref[:, k * D:(k + 1) * D]
    o_ref[...] = x1_ref[...] + gt2_ref[0] * (_rms(f) * g_ref[...])


def _combine(y_tok, gates, x1, gt2, g_post, tokens_per_batch):
    T, D = x1.shape
    B = gt2.shape[0]
    tm = min(COMBINE_TM, tokens_per_batch)
    steps_per_batch = tokens_per_batch // tm
    y4 = y_tok.reshape(-1, TOP_K * D)
    return pl.pallas_call(
        _combine_kernel,
        grid=(T // tm,),
        in_specs=[pl.BlockSpec((tm, TOP_K * D), lambda i: (i, 0)),
                  pl.BlockSpec((tm, TOP_K), lambda i: (i, 0)),
                  pl.BlockSpec((tm, D), lambda i: (i, 0)),
                  pl.BlockSpec((1, 1, D), lambda i: (i // steps_per_batch, 0, 0)),
                  pl.BlockSpec((1, D), lambda i: (0, 0))],
        out_specs=pl.BlockSpec((tm, D), lambda i: (i, 0)),
        out_shape=jax.ShapeDtypeStruct((T, D), F32),
        compiler_params=pltpu.CompilerParams(dimension_semantics=("arbitrary",),
                                             vmem_limit_bytes=VMEM_LIMIT),
        name="combine",
    )(y4, gates, x1, gt2.reshape(B, 1, D), g_post.reshape(1, D))


def _dispatch_plan(eidx, rank, counts):
    n_assign = eidx.size
    bm = MOE_BM
    cap = (n_assign + N_EXPERTS * (bm - 1) + bm - 1) // bm * bm
    n_blocks = cap // bm
    counts = counts.reshape(N_EXPERTS).astype(I32)
    padded = (counts + bm - 1) // bm * bm
    pad_end = jnp.cumsum(padded)
    pad_start = pad_end - padded
    dest = (pad_start[eidx] + rank).reshape(-1)
    blk_start = jnp.arange(n_blocks, dtype=I32) * bm
    blk_expert = jnp.minimum(jnp.searchsorted(pad_end, blk_start, side='right'), N_EXPERTS - 1).astype(I32)
    flat = jnp.arange(n_assign, dtype=I32)
    tok_buf = jnp.zeros((cap,), I32).at[dest].set(flat // TOP_K)
    is_pad = jnp.ones((cap,), I32).at[dest].set(0)
    pad_row = n_assign + jnp.cumsum(is_pad) - 1
    dst_buf = jnp.where(is_pad == 1, pad_row, jnp.zeros((cap,), I32).at[dest].set(flat)).astype(I32)
    return blk_expert, tok_buf, dst_buf, cap


def _layer(x, c, l, w_ada, b_ada, g_pre_mix, g_post_mix, w_in, w_out, lam_q1, lam_k1, lam_q2, lam_k2,
           g_subln, nat_rpb, g_pre_ffn, g_post_ffn, w_router, b_router, w1, b1, w2, b2):
    B, L, D = x.shape
    lam_init = 0.8 - 0.6 * math.exp(-0.3 * l)
    mod, lam = _ada(c, w_ada, b_ada, lam_q1, lam_k1, lam_q2, lam_k2, lam_init)
    sh1, sc1, gt1, sh2, sc2, gt2 = jnp.split(mod, 6, axis=-1)

    qT, kd, vT, nq, nk, nv = _inproj(x, g_pre_mix, sc1, sh1, w_in)
    o_diff = _diff_attention(qT, kd, vT, lam, g_subln, lam_init)
    o_nat = _nat_attention(nq, nk, nv, nat_rpb)

    T = B * L
    x1, h2, eidx, gates, rank, counts = _out_router(
        o_diff.reshape(T, DIFF_WIDTH), o_nat.reshape(T, NAT_WIDTH), x.reshape(T, D), w_out,
        g_post_mix, gt1, g_pre_ffn, sc2, sh2, w_router, b_router, L)
    blk_expert, tok_buf, dst_buf, n_rows_out = _dispatch_plan(eidx, rank, counts)
    y_tok = _experts(h2, blk_expert, tok_buf, dst_buf, w1, b1, w2, b2, n_rows_out)
    out = _combine(y_tok, gates, x1, gt2, g_post_ffn, L)
    return out.reshape(B, L, D)


def kernel(x, c, w_ada, b_ada, g_pre_mix, g_post_mix, w_in, w_out, lam_q1, lam_k1, lam_q2, lam_k2,
           g_subln, nat_rpb, g_pre_ffn, g_post_ffn, w_router, b_router, w1, b1, w2, b2):
    depth = w_ada.shape[0]
    for l in range(depth):
        x = _layer(x, c, l, w_ada[l], b_ada[l], g_pre_mix[l], g_post_mix[l], w_in[l], w_out[l],
                   lam_q1[l], lam_k1[l], lam_q2[l], lam_k2[l], g_subln[l], nat_rpb[l],
                   g_pre_ffn[l], g_post_ffn[l], w_router[l], b_router[l], w1[l], b1[l], w2[l], b2[l])
    return x
```

```python
import functools
import math

import jax
import jax.numpy as jnp
from jax import lax
from jax.experimental import pallas as pl
from jax.experimental.pallas import tpu as pltpu

F32 = jnp.float32
BF16 = jnp.bfloat16
I32 = jnp.int32

HEAD_DIM = 64
N_DIFF_HEADS = 4
DIFF_HEAD_W = 2 * HEAD_DIM
DIFF_WIDTH = N_DIFF_HEADS * DIFF_HEAD_W
N_NAT_HEADS = 8
NAT_WIDTH = N_NAT_HEADS * HEAD_DIM
NAT_PAIRS = N_NAT_HEADS // 2
GRID_W = 64
NAT_KH = 8
NAT_KW = 16
N_EXPERTS = 32
TOP_K = 4
SWIGLU_LIMIT = 7.0
SWIGLU_ALPHA = 1.702
RMS_EPS = 1e-6
NEG_BIG = -1e30

NT_DIMS = (((1,), (1,)), ((), ()))

ADA_TN = 1536
INPROJ_TM = 512
DIFF_TQ = 256
DIFF_TK = 512
NAT_ROWS_PER_STEP = 8
OUT_TM = 512
MOE_BM = 256
COMBINE_TM = 512
VMEM_LIMIT = 48 * 1024 * 1024
EXPERT_VMEM_LIMIT = 56 * 1024 * 1024


def _rms(x, axis=-1):
    return x * lax.rsqrt(jnp.mean(x * x, axis=axis, keepdims=True) + RMS_EPS)


def _ada_kernel(lam_init, c_ref, w_ref, b_ref, lq1_ref, lk1_ref, lq2_ref, lk2_ref, mod_ref, lam_ref):
    c = c_ref[...]
    s = c * (1.0 / (1.0 + jnp.exp(-c)))
    mod_ref[...] = jnp.dot(s, w_ref[...], preferred_element_type=F32,
                           precision=lax.Precision.HIGHEST) + b_ref[...]
    d1 = jnp.sum(lq1_ref[...] * lk1_ref[...], axis=-1, keepdims=True)
    d2 = jnp.sum(lq2_ref[...] * lk2_ref[...], axis=-1, keepdims=True)
    lam = jnp.exp(d1) - jnp.exp(d2) + lam_init
    lam_ref[...] = jnp.broadcast_to(lam, lam_ref.shape)


def _ada(c, w_ada, b_ada, lq1, lk1, lq2, lk2, lam_init):
    B, D = c.shape
    N = w_ada.shape[1]
    c8 = jnp.zeros((8, D), F32).at[:B].set(c)
    vec = pl.BlockSpec((1, HEAD_DIM), lambda j: (0, 0))
    mod, lam = pl.pallas_call(
        functools.partial(_ada_kernel, lam_init),
        grid=(N // ADA_TN,),
        in_specs=[pl.BlockSpec((8, D), lambda j: (0, 0)),
                  pl.BlockSpec((D, ADA_TN), lambda j: (0, j)),
                  pl.BlockSpec((1, ADA_TN), lambda j: (0, j)),
                  vec, vec, vec, vec],
        out_specs=[pl.BlockSpec((8, ADA_TN), lambda j: (0, j)),
                   pl.BlockSpec((8, 128), lambda j: (0, 0))],
        out_shape=[jax.ShapeDtypeStruct((8, N), F32), jax.ShapeDtypeStruct((8, 128), F32)],
        compiler_params=pltpu.CompilerParams(dimension_semantics=("arbitrary",),
                                             vmem_limit_bytes=VMEM_LIMIT),
        name="ada",
    )(c8, w_ada, b_ada.reshape(1, N), lq1.reshape(1, -1), lk1.reshape(1, -1),
      lq2.reshape(1, -1), lk2.reshape(1, -1))
    return mod[:B], lam


def _inproj_kernel(x_ref, g_ref, sc_ref, sh_ref, wqT_ref, wvT_ref, wn_ref,
                   qT_ref, k_ref, vT_ref, nq_ref, nk_ref, nv_ref):
    h = _rms(x_ref[0]) * g_ref[...]
    h = h * (1.0 + sc_ref[0]) + sh_ref[0]
    hb = h.astype(BF16)
    qT_ref[0] = lax.dot_general(wqT_ref[...], hb, NT_DIMS, preferred_element_type=F32).astype(BF16)
    vT_ref[0] = lax.dot_general(wvT_ref[...], hb, NT_DIMS, preferred_element_type=F32).astype(BF16)
    rest = jnp.dot(hb, wn_ref[...], preferred_element_type=F32).astype(BF16)
    k_ref[0] = rest[:, 0:512]
    nq_ref[0] = rest[:, 512:1024]
    nk_ref[0] = rest[:, 1024:1536]
    nv_ref[0] = rest[:, 1536:2048]


def _inproj(x, g_pre, sc1, sh1, w_in):
    B, L, D = x.shape
    tm = INPROJ_TM
    scale = HEAD_DIM ** -0.5
    wqT = (w_in[:, 0:512] * scale).T.astype(BF16)
    wvT = w_in[:, 1024:1536].T.astype(BF16)
    wn = jnp.concatenate([w_in[:, 512:1024], w_in[:, 1536:2048] * scale, w_in[:, 2048:3072]],
                         axis=1).astype(BF16)
    row_major = pl.BlockSpec((1, tm, 512), lambda b, i: (b, i, 0))
    col_major = pl.BlockSpec((1, 512, tm), lambda b, i: (b, 0, i))
    modv = pl.BlockSpec((1, 1, D), lambda b, i: (b, 0, 0))
    rm_shape = jax.ShapeDtypeStruct((B, L, 512), BF16)
    cm_shape = jax.ShapeDtypeStruct((B, 512, L), BF16)
    return pl.pallas_call(
        _inproj_kernel,
        grid=(B, L // tm),
        in_specs=[pl.BlockSpec((1, tm, D), lambda b, i: (b, i, 0)),
                  pl.BlockSpec((1, D), lambda b, i: (0, 0)),
                  modv, modv,
                  pl.BlockSpec((512, D), lambda b, i: (0, 0)),
                  pl.BlockSpec((512, D), lambda b, i: (0, 0)),
                  pl.BlockSpec((D, 2048), lambda b, i: (0, 0))],
        out_specs=[col_major, row_major, col_major, row_major, row_major, row_major],
        out_shape=[cm_shape, rm_shape, cm_shape, rm_shape, rm_shape, rm_shape],
        compiler_params=pltpu.CompilerParams(dimension_semantics=("arbitrary", "arbitrary"),
                                             vmem_limit_bytes=VMEM_LIMIT),
        name="inproj",
    )(x, g_pre.reshape(1, D), sc1.reshape(B, 1, D), sh1.reshape(B, 1, D), wqT, wvT, wn)


def _diff_kernel(lam_init, n_kchunks, slopes_ref, lam_ref, qT_ref, k_ref, vT_ref, g_ref, o_ref,
                 acc_ref, m_ref, l_ref, s0_ref):
    tq = qT_ref.shape[2]
    tk = s0_ref.shape[0]
    h = pl.program_id(1)
    q0 = pl.program_id(2) * tq
    slope = slopes_ref[h]

    qT = qT_ref[0]
    row = lax.broadcasted_iota(I32, qT.shape, 0)
    zero = jnp.zeros_like(qT)
    qT_maps = (jnp.where(row < HEAD_DIM, qT, zero), jnp.where(row >= HEAD_DIM, qT, zero))

    kk = lax.broadcasted_iota(I32, (tk, tq), 0)
    qq = lax.broadcasted_iota(I32, (tk, tq), 1)
    s0_ref[...] = slope * (kk - qq).astype(F32)
    acc_ref[...] = jnp.zeros_like(acc_ref)
    l_ref[...] = jnp.zeros_like(l_ref)
    m_ref[...] = jnp.full_like(m_ref, NEG_BIG)

    def body(kc, carry):
        k0 = pl.multiple_of(kc * tk, tk)
        kb = k_ref[0, pl.ds(k0, tk), :]
        vb = vT_ref[0, :, pl.ds(k0, tk)]
        bias = jnp.abs(s0_ref[...] + slope * (k0 - q0).astype(F32))
        for mi in range(2):
            s = jnp.dot(kb, qT_maps[mi], preferred_element_type=F32) - bias
            m_old = m_ref[mi]
            m_new = jnp.maximum(m_old, jnp.max(s, axis=0, keepdims=True))
            alpha = jnp.exp(m_old - m_new)
            p = jnp.exp(s - m_new)
            l_ref[mi] = alpha * l_ref[mi] + jnp.sum(p, axis=0, keepdims=True)
            acc_ref[mi] = alpha * acc_ref[mi] + jnp.dot(vb, p.astype(BF16), preferred_element_type=F32)
            m_ref[mi] = m_new
        return carry

    lax.fori_loop(0, n_kchunks, body, 0)

    lam = lam_ref[0:1, 0:1]
    o = acc_ref[0] / l_ref[0] - lam * (acc_ref[1] / l_ref[1])
    y = _rms(o, axis=0) * g_ref[...] * (1.0 - lam_init)
    o_ref[0] = y.T.astype(BF16)


def _diff_attention(qT, k, vT, lam, g_sub, lam_init):
    B, _, L = qT.shape
    tq, tk = min(DIFF_TQ, L), min(DIFF_TK, L)
    i = jnp.arange(1, N_DIFF_HEADS + 1, dtype=F32)
    slopes = jnp.exp2(-8.0 * i / N_DIFF_HEADS)
    return pl.pallas_call(
        functools.partial(_diff_kernel, lam_init, L // tk),
        grid=(B, N_DIFF_HEADS, L // tq),
        in_specs=[pl.BlockSpec(memory_space=pltpu.SMEM),
                  pl.BlockSpec((8, 128), lambda b, h, i: (0, 0)),
                  pl.BlockSpec((1, DIFF_HEAD_W, tq), lambda b, h, i: (b, h, i)),
                  pl.BlockSpec((1, L, DIFF_HEAD_W), lambda b, h, i: (b, 0, h)),
                  pl.BlockSpec((1, DIFF_HEAD_W, L), lambda b, h, i: (b, h, 0)),
                  pl.BlockSpec((DIFF_HEAD_W, 1), lambda b, h, i: (0, 0))],
        out_specs=pl.BlockSpec((1, tq, DIFF_HEAD_W), lambda b, h, i: (b, i, h)),
        out_shape=jax.ShapeDtypeStruct((B, L, DIFF_WIDTH), BF16),
        scratch_shapes=[pltpu.VMEM((2, DIFF_HEAD_W, tq), F32),
                        pltpu.VMEM((2, 1, tq), F32),
                        pltpu.VMEM((2, 1, tq), F32),
                        pltpu.VMEM((tk, tq), F32)],
        compiler_params=pltpu.CompilerParams(
            dimension_semantics=("arbitrary", "arbitrary", "arbitrary"),
            vmem_limit_bytes=VMEM_LIMIT),
        name="diff_attn",
    )(slopes, lam, qT, k, vT, g_sub.reshape(DIFF_HEAD_W, 1))


def _nat_bias_table(rpb):
    c = jnp.arange(GRID_W)
    col_start = jnp.clip(c - NAT_KW // 2, 0, GRID_W - NAT_KW)
    col_in = (c[None, :] >= col_start[:, None]) & (c[None, :] < col_start[:, None] + NAT_KW)
    dc = jnp.clip(c[None, :] - c[:, None], -(NAT_KW - 1), NAT_KW - 1) + (NAT_KW - 1)
    rpb = rpb.astype(F32)
    cols = jnp.zeros(rpb.shape[:2] + (GRID_W, GRID_W), F32)
    for j in range(2 * NAT_KW - 1):
        cols = cols + jnp.where(dc == j, rpb[:, :, j][:, :, None, None], 0.0)
    cols = jnp.where(col_in[None, None], cols, NEG_BIG)
    tbl = jnp.stack([cols[:, NAT_KH - 1 - v:2 * NAT_KH - 1 - v] for v in range(NAT_KH)], axis=1)
    tbl = tbl.transpose(0, 1, 3, 2, 4)
    return tbl.reshape(NAT_PAIRS, 2, NAT_KH, GRID_W, NAT_KH * GRID_W)


def _nat_kernel(n_rows, q_ref, k_ref, v_ref, bias_ref, o_ref):
    ri = pl.program_id(2)
    rows_per_step = q_ref.shape[1] // GRID_W
    win = NAT_KH * GRID_W
    lane = lax.broadcasted_iota(I32, (GRID_W, 2 * HEAD_DIM), 1)
    for j in range(rows_per_step):
        r = ri * rows_per_step + j
        rs = jnp.clip(r - NAT_KH // 2, 0, n_rows - NAT_KH)
        var = r - rs
        k0 = pl.multiple_of(rs * GRID_W, GRID_W)
        kw = k_ref[0, pl.ds(k0, win), :]
        vw = v_ref[0, pl.ds(k0, win), :]
        q = q_ref[0, j * GRID_W:(j + 1) * GRID_W, :]
        zero = jnp.zeros_like(q)
        outs = []
        for hh in range(2):
            keep = (lane < HEAD_DIM) if hh == 0 else (lane >= HEAD_DIM)
            s = lax.dot_general(jnp.where(keep, q, zero), kw, NT_DIMS, preferred_element_type=F32)
            s = s + bias_ref[0, hh, var]
            p = jnp.exp(s - jnp.max(s, axis=-1, keepdims=True))
            l = jnp.sum(p, axis=-1, keepdims=True)
            outs.append(jnp.dot(p.astype(BF16), vw, preferred_element_type=F32) / l)
        o_ref[0, j * GRID_W:(j + 1) * GRID_W, :] = jnp.where(lane < HEAD_DIM, outs[0], outs[1]).astype(BF16)


def _nat_attention(nq, nk, nv, rpb):
    B, L, _ = nq.shape
    n_rows = L // GRID_W
    assert n_rows >= NAT_KH
    rps = min(NAT_ROWS_PER_STEP, n_rows)
    tbl = _nat_bias_table(rpb)
    kv = pl.BlockSpec((1, L, 2 * HEAD_DIM), lambda b, p, i: (b, 0, p))
    qo = pl.BlockSpec((1, rps * GRID_W, 2 * HEAD_DIM), lambda b, p, i: (b, i, p))
    return pl.pallas_call(
        functools.partial(_nat_kernel, n_rows),
        grid=(B, NAT_PAIRS, n_rows // rps),
        in_specs=[qo, kv, kv,
                  pl.BlockSpec((1, 2, NAT_KH, GRID_W, NAT_KH * GRID_W), lambda b, p, i: (p, 0, 0, 0, 0))],
        out_specs=qo,
        out_shape=jax.ShapeDtypeStruct((B, L, NAT_WIDTH), BF16),
        compiler_params=pltpu.CompilerParams(
            dimension_semantics=("arbitrary", "arbitrary", "arbitrary"),
            vmem_limit_bytes=VMEM_LIMIT),
        name="nat_attn",
    )(nq, nk, nv, tbl)


def _out_kernel(od_ref, on_ref, x_ref, wt_ref, wb_ref, gpost_ref, gt1_ref, gpre_ref, sc2_ref, sh2_ref,
                wr_ref, br_ref, x1_ref, h2_ref, eidx_ref, gate_ref, rank_ref, cnt_ref, carry_ref):
    tm = x_ref.shape[0]

    @pl.when(pl.program_id(0) == 0)
    def _():
        carry_ref[...] = jnp.zeros_like(carry_ref)

    mix = (jnp.dot(od_ref[...], wt_ref[...], preferred_element_type=F32)
           + jnp.dot(on_ref[...], wb_ref[...], preferred_element_type=F32))
    x1 = x_ref[...] + gt1_ref[0] * (_rms(mix) * gpost_ref[...])
    x1_ref[...] = x1
    h2 = _rms(x1) * gpre_ref[...]
    h2 = h2 * (1.0 + sc2_ref[0]) + sh2_ref[0]
    h2_ref[...] = h2

    logits = jnp.dot(h2, wr_ref[...], preferred_element_type=F32,
                     precision=lax.Precision.HIGHEST) + br_ref[...]
    eio = lax.broadcasted_iota(I32, logits.shape, 1).astype(F32)
    onehot = jnp.zeros_like(logits)
    vals, idxs, sels = [], [], []
    cur = logits
    for _ in range(TOP_K):
        mx = jnp.max(cur, axis=-1, keepdims=True)
        idx = jnp.min(jnp.where(cur == mx, eio, float(N_EXPERTS)), axis=-1, keepdims=True)
        sel = eio == idx
        vals.append(mx)
        idxs.append(idx)
        sels.append(sel)
        cur = jnp.where(sel, -jnp.inf, cur)
        onehot = onehot + sel.astype(F32)

    ex = [jnp.exp(v - vals[0]) for v in vals]
    tot = ex[0] + ex[1] + ex[2] + ex[3]

    rr = lax.broadcasted_iota(I32, (tm, tm), 0)
    cc = lax.broadcasted_iota(I32, (tm, tm), 1)
    tri = (rr > cc).astype(BF16)
    carry = carry_ref[...]
    cum = jnp.dot(tri, onehot.astype(BF16), preferred_element_type=F32) + carry
    ranks = [jnp.sum(jnp.where(sel, cum, 0.0), axis=-1, keepdims=True) for sel in sels]
    carry = carry + jnp.sum(onehot, axis=0, keepdims=True)
    carry_ref[...] = carry
    cnt_ref[...] = carry

    kio = lax.broadcasted_iota(I32, (tm, TOP_K), 1)

    def pack(cols):
        out = jnp.broadcast_to(cols[TOP_K - 1], (tm, TOP_K))
        for k in range(TOP_K - 2, -1, -1):
            out = jnp.where(kio == k, cols[k], out)
        return out

    eidx_ref[...] = pack(idxs).astype(I32)
    gate_ref[...] = pack([e / tot for e in ex])
    rank_ref[...] = pack(ranks).astype(I32)


def _out_router(od, on, x, w_out, g_post, gt1, g_pre, sc2, sh2, w_router, b_router, tokens_per_batch):
    T, D = x.shape
    B = gt1.shape[0]
    tm = min(OUT_TM, tokens_per_batch)
    steps_per_batch = tokens_per_batch // tm
    wt = w_out[:DIFF_WIDTH].astype(BF16)
    wb = w_out[DIFF_WIDTH:].astype(BF16)
    rowblk = lambda w: pl.BlockSpec((tm, w), lambda i: (i, 0))
    const = lambda shape: pl.BlockSpec(shape, lambda i: (0,) * len(shape))
    modv = pl.BlockSpec((1, 1, D), lambda i: (i // steps_per_batch, 0, 0))
    return pl.pallas_call(
        _out_kernel,
        grid=(T // tm,),
        in_specs=[rowblk(DIFF_WIDTH), rowblk(NAT_WIDTH), rowblk(D),
                  const((DIFF_WIDTH, D)), const((NAT_WIDTH, D)), const((1, D)), modv, const((1, D)),
                  modv, modv, const((D, N_EXPERTS)), const((1, N_EXPERTS))],
        out_specs=[rowblk(D), rowblk(D), rowblk(TOP_K), rowblk(TOP_K), rowblk(TOP_K),
                   const((1, N_EXPERTS))],
        out_shape=[jax.ShapeDtypeStruct((T, D), F32), jax.ShapeDtypeStruct((T, D), F32),
                   jax.ShapeDtypeStruct((T, TOP_K), I32), jax.ShapeDtypeStruct((T, TOP_K), F32),
                   jax.ShapeDtypeStruct((T, TOP_K), I32), jax.ShapeDtypeStruct((1, N_EXPERTS), F32)],
        scratch_shapes=[pltpu.VMEM((1, N_EXPERTS), F32)],
        compiler_params=pltpu.CompilerParams(dimension_semantics=("arbitrary",),
                                             vmem_limit_bytes=VMEM_LIMIT),
        name="out_router",
    )(od, on, x, wt, wb, g_post.reshape(1, D), gt1.reshape(B, 1, D), g_pre.reshape(1, D),
      sc2.reshape(B, 1, D), sh2.reshape(B, 1, D), w_router, b_router.reshape(1, N_EXPERTS))


def _expert_kernel(be_ref, tok_ref, tok_next_ref, dst_ref, h2_hbm, w1_ref, b1g_ref, b1l_ref, w2_ref, b2_ref,
                   y_hbm, xbuf, ybuf, w1t, w2t, hh, gsem, ssem):
    i = pl.program_id(0)
    n = pl.num_programs(0)
    bm = xbuf.shape[1]
    ff = w2_ref.shape[1]
    slot = i % 2

    def gather_row(idx_ref, s, r):
        return pltpu.make_async_copy(h2_hbm.at[pl.ds(idx_ref[0, 0, r], 1), :],
                                     xbuf.at[s, pl.ds(r, 1), :], gsem.at[s])

    def scatter_row(r):
        return pltpu.make_async_copy(ybuf.at[pl.ds(r, 1), :],
                                     y_hbm.at[pl.ds(dst_ref[0, 0, r], 1), :], ssem.at[0])

    def wait_gather(s):
        pltpu.make_async_copy(h2_hbm.at[pl.ds(0, bm), :], xbuf.at[s], gsem.at[s]).wait()

    def wait_scatter():
        pltpu.make_async_copy(ybuf, y_hbm.at[pl.ds(0, bm), :], ssem.at[0]).wait()

    @pl.when(i == 0)
    def _():
        for r in range(bm):
            gather_row(tok_ref, 0, r).start()

    for r in range(bm):
        gather_row(tok_next_ref, 1 - slot, r).start()

    @pl.when((i == 0) | (be_ref[i] != be_ref[jnp.maximum(i - 1, 0)]))
    def _():
        cw = 256
        for c in range(w1_ref.shape[2] // cw):
            w1t[c * cw:(c + 1) * cw, :] = w1_ref[0, :, c * cw:(c + 1) * cw].T.astype(BF16)
        for c in range(w2_ref.shape[2] // cw):
            w2t[c * cw:(c + 1) * cw, :] = w2_ref[0, :, c * cw:(c + 1) * cw].T.astype(BF16)

    wait_gather(slot)
    x = xbuf[slot].astype(BF16)
    hh_t = lax.dot_general(w1t[...], x, NT_DIMS, preferred_element_type=F32)
    n_lane_tiles = bm // 128
    for j in range(n_lane_tiles):
        hh[j] = hh_t[:, j * 128:(j + 1) * 128]
    even = jnp.concatenate([hh[j, pl.ds(0, ff, stride=2), :] for j in range(n_lane_tiles)], axis=1)
    odd = jnp.concatenate([hh[j, pl.ds(1, ff, stride=2), :] for j in range(n_lane_tiles)], axis=1)
    glu = jnp.minimum(even + b1g_ref[0], SWIGLU_LIMIT)
    lin = jnp.clip(odd + b1l_ref[0], -SWIGLU_LIMIT, SWIGLU_LIMIT)
    act = glu * (1.0 / (1.0 + jnp.exp(-SWIGLU_ALPHA * glu))) * (lin + 1.0)
    y_t = jnp.dot(w2t[...], act.astype(BF16), preferred_element_type=F32)

    @pl.when(i > 0)
    def _():
        wait_scatter()

    ybuf[...] = y_t.T + b2_ref[0]
    for r in range(bm):
        scatter_row(r).start()

    @pl.when(i == n - 1)
    def _():
        wait_scatter()
        wait_gather(1 - slot)


def _experts(h2, blk_expert, tok_buf, dst_buf, w1, b1, w2, b2, n_rows_out):
    T, D = h2.shape
    n_blocks = blk_expert.shape[0]
    bm = MOE_BM
    F = w2.shape[1]
    b1g = b1[:, 0::2].reshape(N_EXPERTS, F, 1)
    b1l = b1[:, 1::2].reshape(N_EXPERTS, F, 1)
    tok3 = tok_buf.reshape(n_blocks, 1, bm)
    dst3 = dst_buf.reshape(n_blocks, 1, bm)
    smem_blk = lambda fn: pl.BlockSpec((1, 1, bm), fn, memory_space=pltpu.SMEM)
    grid_spec = pltpu.PrefetchScalarGridSpec(
        num_scalar_prefetch=1,
        grid=(n_blocks,),
        in_specs=[smem_blk(lambda i, be: (i, 0, 0)),
                  smem_blk(lambda i, be: (jnp.minimum(i + 1, n_blocks - 1), 0, 0)),
                  smem_blk(lambda i, be: (i, 0, 0)),
                  pl.BlockSpec(memory_space=pl.ANY),
                  pl.BlockSpec((1, D, 2 * F), lambda i, be: (be[i], 0, 0)),
                  pl.BlockSpec((1, F, 1), lambda i, be: (be[i], 0, 0)),
                  pl.BlockSpec((1, F, 1), lambda i, be: (be[i], 0, 0)),
                  pl.BlockSpec((1, F, D), lambda i, be: (be[i], 0, 0)),
                  pl.BlockSpec((1, 1, D), lambda i, be: (be[i], 0, 0))],
        out_specs=pl.BlockSpec(memory_space=pl.ANY),
        scratch_shapes=[pltpu.VMEM((2, bm, D), F32), pltpu.VMEM((bm, D), F32),
                        pltpu.VMEM((2 * F, D), BF16), pltpu.VMEM((D, F), BF16),
                        pltpu.VMEM((bm // 128, 2 * F, 128), F32),
                        pltpu.SemaphoreType.DMA((2,)), pltpu.SemaphoreType.DMA((1,))],
    )
    return pl.pallas_call(
        _expert_kernel,
        grid_spec=grid_spec,
        out_shape=jax.ShapeDtypeStruct((n_rows_out, D), F32),
        compiler_params=pltpu.CompilerParams(dimension_semantics=("arbitrary",),
                                             vmem_limit_bytes=EXPERT_VMEM_LIMIT),
        name="experts",
    )(blk_expert, tok3, tok3, dst3, h2, w1, b1g, b1l, w2, b2.reshape(N_EXPERTS, 1, D))


def _combine_kernel(y0_ref, y1_ref, y2_ref, y3_ref, gate_ref, x1_ref, gt2_ref, g_ref, o_ref):
    gates = gate_ref[...]
    f = gates[:, 0:1] * y0_ref[...]
    for k, y_ref in enumerate((y1_ref, y2_ref, y3_ref), start=1):
        f = f + gates[:, k:k + 1] * y_ref[...]
    o_ref[...] = x1_ref[...] + gt2_ref[0] * (_rms(f) * g_ref[...])


def _combine(y_tok, gates, x1, gt2, g_post, tokens_per_batch):
    T, D = x1.shape
    B = gt2.shape[0]
    tm = min(COMBINE_TM, tokens_per_batch)
    steps_per_batch = tokens_per_batch // tm
    steps = T // tm
    y_spec = lambda k: pl.BlockSpec((tm, D), lambda i: (k * steps + i, 0))
    return pl.pallas_call(
        _combine_kernel,
        grid=(steps,),
        in_specs=[y_spec(0), y_spec(1), y_spec(2), y_spec(3),
                  pl.BlockSpec((tm, TOP_K), lambda i: (i, 0)),
                  pl.BlockSpec((tm, D), lambda i: (i, 0)),
                  pl.BlockSpec((1, 1, D), lambda i: (i // steps_per_batch, 0, 0)),
                  pl.BlockSpec((1, D), lambda i: (0, 0))],
        out_specs=pl.BlockSpec((tm, D), lambda i: (i, 0)),
        out_shape=jax.ShapeDtypeStruct((T, D), F32),
        compiler_params=pltpu.CompilerParams(dimension_semantics=("arbitrary",),
                                             vmem_limit_bytes=VMEM_LIMIT),
        name="combine",
    )(y_tok, y_tok, y_tok, y_tok, gates, x1, gt2.reshape(B, 1, D), g_post.reshape(1, D))


def _dispatch_plan(eidx, rank, counts):
    T = eidx.shape[0]
    n_assign = eidx.size
    bm = MOE_BM
    cap = (n_assign + N_EXPERTS * (bm - 1) + bm - 1) // bm * bm
    n_blocks = cap // bm
    counts = counts.reshape(N_EXPERTS).astype(I32)
    padded = (counts + bm - 1) // bm * bm
    pad_end = jnp.cumsum(padded)
    pad_start = pad_end - padded
    eio = jnp.arange(N_EXPERTS, dtype=I32)
    start_of = jnp.sum(jnp.where(eidx[..., None] == eio, pad_start, 0), axis=-1)
    dest = (start_of + rank).reshape(-1)
    blk_start = jnp.arange(n_blocks, dtype=I32) * bm
    blk_expert = jnp.minimum(jnp.sum((blk_start[:, None] >= pad_end[None, :]).astype(I32), axis=-1),
                             N_EXPERTS - 1)
    inv = jnp.zeros((cap,), I32).at[dest].set(jnp.arange(1, n_assign + 1, dtype=I32))
    is_pad = inv == 0
    a = inv - 1
    tok_buf = jnp.where(is_pad, 0, a // TOP_K).astype(I32)
    pad_row = n_assign + jnp.cumsum(is_pad.astype(I32)) - 1
    dst_buf = jnp.where(is_pad, pad_row, (a % TOP_K) * T + a // TOP_K).astype(I32)
    return blk_expert, tok_buf, dst_buf, cap


def _layer(x, c, l, w_ada, b_ada, g_pre_mix, g_post_mix, w_in, w_out, lam_q1, lam_k1, lam_q2, lam_k2,
           g_subln, nat_rpb, g_pre_ffn, g_post_ffn, w_router, b_router, w1, b1, w2, b2):
    B, L, D = x.shape
    lam_init = 0.8 - 0.6 * math.exp(-0.3 * l)
    mod, lam = _ada(c, w_ada, b_ada, lam_q1, lam_k1, lam_q2, lam_k2, lam_init)
    sh1, sc1, gt1, sh2, sc2, gt2 = jnp.split(mod, 6, axis=-1)

    qT, kd, vT, nq, nk, nv = _inproj(x, g_pre_mix, sc1, sh1, w_in)
    o_diff = _diff_attention(qT, kd, vT, lam, g_subln, lam_init)
    o_nat = _nat_attention(nq, nk, nv, nat_rpb)

    T = B * L
    x1, h2, eidx, gates, rank, counts = _out_router(
        o_diff.reshape(T, DIFF_WIDTH), o_nat.reshape(T, NAT_WIDTH), x.reshape(T, D), w_out,
        g_post_mix, gt1, g_pre_ffn, sc2, sh2, w_router, b_router, L)
    blk_expert, tok_buf, dst_buf, n_rows_out = _dispatch_plan(eidx, rank, counts)
    y_tok = _experts(h2, blk_expert, tok_buf, dst_buf, w1, b1, w2, b2, n_rows_out)
    out = _combine(y_tok, gates, x1, gt2, g_post_ffn, L)
    return out.reshape(B, L, D)


def kernel(x, c, w_ada, b_ada, g_pre_mix, g_post_mix, w_in, w_out, lam_q1, lam_k1, lam_q2, lam_k2,
           g_subln, nat_rpb, g_pre_ffn, g_post_ffn, w_router, b_router, w1, b1, w2, b2):
    depth = w_ada.shape[0]
    for l in range(depth):
        x = _layer(x, c, l, w_ada[l], b_ada[l], g_pre_mix[l], g_post_mix[l], w_in[l], w_out[l],
                   lam_q1[l], lam_k1[l], lam_q2[l], lam_k2[l], g_subln[l], nat_rpb[l],
                   g_pre_ffn[l], g_post_ffn[l], w_router[l], b_router[l], w1[l], b1[l], w2[l], b2[l])
    return x
```

```python
import functools
import math

import jax
import jax.numpy as jnp
from jax import lax
from jax.experimental import pallas as pl
from jax.experimental.pallas import tpu as pltpu

F32 = jnp.float32
BF16 = jnp.bfloat16
I32 = jnp.int32

HEAD_DIM = 64
N_DIFF_HEADS = 4
DIFF_HEAD_W = 2 * HEAD_DIM
DIFF_WIDTH = N_DIFF_HEADS * DIFF_HEAD_W
N_NAT_HEADS = 8
NAT_WIDTH = N_NAT_HEADS * HEAD_DIM
NAT_PAIRS = N_NAT_HEADS // 2
GRID_W = 64
NAT_KH = 8
NAT_KW = 16
N_EXPERTS = 32
TOP_K = 4
SWIGLU_LIMIT = 7.0
SWIGLU_ALPHA = 1.702
RMS_EPS = 1e-6
NEG_BIG = -1e30

NT_DIMS = (((1,), (1,)), ((), ()))

ADA_TN = 1536
INPROJ_TM = 512
DIFF_TQ = 256
DIFF_TK = 512
NAT_ROWS_PER_STEP = 8
OUT_TM = 512
MOE_BM = 256
COMBINE_TM = 512
VMEM_LIMIT = 48 * 1024 * 1024
EXPERT_VMEM_LIMIT = 56 * 1024 * 1024


def _rms(x, axis=-1):
    return x * lax.rsqrt(jnp.mean(x * x, axis=axis, keepdims=True) + RMS_EPS)


def _ada_kernel(lam_init, c_ref, w_ref, b_ref, lq1_ref, lk1_ref, lq2_ref, lk2_ref, mod_ref, lam_ref):
    c = c_ref[...]
    s = c * (1.0 / (1.0 + jnp.exp(-c)))
    mod_ref[...] = jnp.dot(s, w_ref[...], preferred_element_type=F32,
                           precision=lax.Precision.HIGHEST) + b_ref[...]
    d1 = jnp.sum(lq1_ref[...] * lk1_ref[...], axis=-1, keepdims=True)
    d2 = jnp.sum(lq2_ref[...] * lk2_ref[...], axis=-1, keepdims=True)
    lam = jnp.exp(d1) - jnp.exp(d2) + lam_init
    lam_ref[...] = jnp.broadcast_to(lam, lam_ref.shape)


def _ada(c, w_ada, b_ada, lq1, lk1, lq2, lk2, lam_init):
    B, D = c.shape
    N = w_ada.shape[1]
    c8 = jnp.zeros((8, D), F32).at[:B].set(c)
    vec = pl.BlockSpec((1, HEAD_DIM), lambda j: (0, 0))
    mod, lam = pl.pallas_call(
        functools.partial(_ada_kernel, lam_init),
        grid=(N // ADA_TN,),
        in_specs=[pl.BlockSpec((8, D), lambda j: (0, 0)),
                  pl.BlockSpec((D, ADA_TN), lambda j: (0, j)),
                  pl.BlockSpec((1, ADA_TN), lambda j: (0, j)),
                  vec, vec, vec, vec],
        out_specs=[pl.BlockSpec((8, ADA_TN), lambda j: (0, j)),
                   pl.BlockSpec((8, 128), lambda j: (0, 0))],
        out_shape=[jax.ShapeDtypeStruct((8, N), F32), jax.ShapeDtypeStruct((8, 128), F32)],
        compiler_params=pltpu.CompilerParams(dimension_semantics=("arbitrary",),
                                             vmem_limit_bytes=VMEM_LIMIT),
        name="ada",
    )(c8, w_ada, b_ada.reshape(1, N), lq1.reshape(1, -1), lk1.reshape(1, -1),
      lq2.reshape(1, -1), lk2.reshape(1, -1))
    return mod[:B], lam


def _inproj_kernel(x_ref, g_ref, sc_ref, sh_ref, wqT_ref, wvT_ref, wn_ref,
                   qT_ref, k_ref, vT_ref, nq_ref, nk_ref, nv_ref):
    h = _rms(x_ref[0]) * g_ref[...]
    h = h * (1.0 + sc_ref[0]) + sh_ref[0]
    hb = h.astype(BF16)
    qT_ref[0] = lax.dot_general(wqT_ref[...], hb, NT_DIMS, preferred_element_type=F32).astype(BF16)
    vT_ref[0] = lax.dot_general(wvT_ref[...], hb, NT_DIMS, preferred_element_type=F32).astype(BF16)
    rest = jnp.dot(hb, wn_ref[...], preferred_element_type=F32).astype(BF16)
    k_ref[0] = rest[:, 0:512]
    nq_ref[0] = rest[:, 512:1024]
    nk_ref[0] = rest[:, 1024:1536]
    nv_ref[0] = rest[:, 1536:2048]


def _inproj(x, g_pre, sc1, sh1, w_in):
    B, L, D = x.shape
    tm = INPROJ_TM
    scale = HEAD_DIM ** -0.5
    wqT = (w_in[:, 0:512] * scale).T.astype(BF16)
    wvT = w_in[:, 1024:1536].T.astype(BF16)
    wn = jnp.concatenate([w_in[:, 512:1024], w_in[:, 1536:2048] * scale, w_in[:, 2048:3072]],
                         axis=1).astype(BF16)
    row_major = pl.BlockSpec((1, tm, 512), lambda b, i: (b, i, 0))
    col_major = pl.BlockSpec((1, 512, tm), lambda b, i: (b, 0, i))
    modv = pl.BlockSpec((1, 1, D), lambda b, i: (b, 0, 0))
    rm_shape = jax.ShapeDtypeStruct((B, L, 512), BF16)
    cm_shape = jax.ShapeDtypeStruct((B, 512, L), BF16)
    return pl.pallas_call(
        _inproj_kernel,
        grid=(B, L // tm),
        in_specs=[pl.BlockSpec((1, tm, D), lambda b, i: (b, i, 0)),
                  pl.BlockSpec((1, D), lambda b, i: (0, 0)),
                  modv, modv,
                  pl.BlockSpec((512, D), lambda b, i: (0, 0)),
                  pl.BlockSpec((512, D), lambda b, i: (0, 0)),
                  pl.BlockSpec((D, 2048), lambda b, i: (0, 0))],
        out_specs=[col_major, row_major, col_major, row_major, row_major, row_major],
        out_shape=[cm_shape, rm_shape, cm_shape, rm_shape, rm_shape, rm_shape],
        compiler_params=pltpu.CompilerParams(dimension_semantics=("arbitrary", "arbitrary"),
                                             vmem_limit_bytes=VMEM_LIMIT),
        name="inproj",
    )(x, g_pre.reshape(1, D), sc1.reshape(B, 1, D), sh1.reshape(B, 1, D), wqT, wvT, wn)


def _diff_kernel(lam_init, n_kchunks, slopes_ref, lam_ref, qT_ref, k_ref, vT_ref, g_ref, o_ref,
                 acc_ref, m_ref, l_ref, s0_ref, qm_ref, s_ref, p_ref, a_ref):
    tq = qT_ref.shape[2]
    tk = s0_ref.shape[0]
    h = pl.program_id(1)
    q0 = pl.program_id(2) * tq
    slope = slopes_ref[h]

    qT = qT_ref[0]
    row = lax.broadcasted_iota(I32, qT.shape, 0)
    zero = jnp.zeros_like(qT)
    qm_ref[0] = jnp.where(row < HEAD_DIM, qT, zero)
    qm_ref[1] = jnp.where(row >= HEAD_DIM, qT, zero)

    kk = lax.broadcasted_iota(I32, (tk, tq), 0)
    qq = lax.broadcasted_iota(I32, (tk, tq), 1)
    s0_ref[...] = slope * (kk - qq).astype(F32)
    acc_ref[...] = jnp.zeros_like(acc_ref)
    l_ref[...] = jnp.zeros_like(l_ref)
    m_ref[...] = jnp.full_like(m_ref, NEG_BIG)

    def start(c):
        return c * tk if isinstance(c, int) else pl.multiple_of(c * tk, tk)

    def scores(c, slot):
        kb = k_ref[0, pl.ds(start(c), tk), :]
        for mi in range(2):
            s_ref[slot, mi] = jnp.dot(kb, qm_ref[mi], preferred_element_type=F32)

    def softmax(c, slot):
        bias = jnp.abs(s0_ref[...] + slope * (start(c) - q0).astype(F32))
        for mi in range(2):
            s = s_ref[slot, mi] - bias
            m_old = m_ref[mi]
            m_new = jnp.maximum(m_old, jnp.max(s, axis=0, keepdims=True))
            alpha = jnp.exp(m_old - m_new)
            p = jnp.exp(s - m_new)
            l_ref[mi] = alpha * l_ref[mi] + jnp.sum(p, axis=0, keepdims=True)
            m_ref[mi] = m_new
            a_ref[slot, mi] = alpha
            p_ref[slot, mi] = p.astype(BF16)

    def values(c, slot):
        vb = vT_ref[0, :, pl.ds(start(c), tk)]
        for mi in range(2):
            acc_ref[mi] = a_ref[slot, mi] * acc_ref[mi] + jnp.dot(vb, p_ref[slot, mi],
                                                                  preferred_element_type=F32)

    n = n_kchunks
    scores(0, 0)
    scores(1, 1)
    softmax(0, 0)

    def pair(j, carry):
        c = 2 * j + 1
        scores(c + 1, 0)
        softmax(c, 1)
        values(c - 1, 0)
        scores(c + 2, 1)
        softmax(c + 1, 0)
        values(c, 1)
        return carry

    lax.fori_loop(0, (n - 2) // 2, pair, 0)
    softmax(n - 1, 1)
    values(n - 2, 0)
    values(n - 1, 1)

    lam = lam_ref[0:1, 0:1]
    o = acc_ref[0] / l_ref[0] - lam * (acc_ref[1] / l_ref[1])
    y = _rms(o, axis=0) * g_ref[...] * (1.0 - lam_init)
    o_ref[0] = y.T.astype(BF16)


def _diff_attention(qT, k, vT, lam, g_sub, lam_init):
    B, _, L = qT.shape
    tq, tk = min(DIFF_TQ, L), min(DIFF_TK, L)
    i = jnp.arange(1, N_DIFF_HEADS + 1, dtype=F32)
    slopes = jnp.exp2(-8.0 * i / N_DIFF_HEADS)
    return pl.pallas_call(
        functools.partial(_diff_kernel, lam_init, L // tk),
        grid=(B, N_DIFF_HEADS, L // tq),
        in_specs=[pl.BlockSpec(memory_space=pltpu.SMEM),
                  pl.BlockSpec((8, 128), lambda b, h, i: (0, 0)),
                  pl.BlockSpec((1, DIFF_HEAD_W, tq), lambda b, h, i: (b, h, i)),
                  pl.BlockSpec((1, L, DIFF_HEAD_W), lambda b, h, i: (b, 0, h)),
                  pl.BlockSpec((1, DIFF_HEAD_W, L), lambda b, h, i: (b, h, 0)),
                  pl.BlockSpec((DIFF_HEAD_W, 1), lambda b, h, i: (0, 0))],
        out_specs=pl.BlockSpec((1, tq, DIFF_HEAD_W), lambda b, h, i: (b, i, h)),
        out_shape=jax.ShapeDtypeStruct((B, L, DIFF_WIDTH), BF16),
        scratch_shapes=[pltpu.VMEM((2, DIFF_HEAD_W, tq), F32),
                        pltpu.VMEM((2, 1, tq), F32),
                        pltpu.VMEM((2, 1, tq), F32),
                        pltpu.VMEM((tk, tq), F32),
                        pltpu.VMEM((2, DIFF_HEAD_W, tq), BF16),
                        pltpu.VMEM((2, 2, tk, tq), F32),
                        pltpu.VMEM((2, 2, tk, tq), BF16),
                        pltpu.VMEM((2, 2, 1, tq), F32)],
        compiler_params=pltpu.CompilerParams(
            dimension_semantics=("arbitrary", "arbitrary", "arbitrary"),
            vmem_limit_bytes=VMEM_LIMIT),
        name="diff_attn",
    )(slopes, lam, qT, k, vT, g_sub.reshape(DIFF_HEAD_W, 1))


def _nat_bias_table(rpb):
    c = jnp.arange(GRID_W)
    col_start = jnp.clip(c - NAT_KW // 2, 0, GRID_W - NAT_KW)
    col_in = (c[None, :] >= col_start[:, None]) & (c[None, :] < col_start[:, None] + NAT_KW)
    dc = jnp.clip(c[None, :] - c[:, None], -(NAT_KW - 1), NAT_KW - 1) + (NAT_KW - 1)
    rpb = rpb.astype(F32)
    cols = jnp.zeros(rpb.shape[:2] + (GRID_W, GRID_W), F32)
    for j in range(2 * NAT_KW - 1):
        cols = cols + jnp.where(dc == j, rpb[:, :, j][:, :, None, None], 0.0)
    cols = jnp.where(col_in[None, None], cols, NEG_BIG)
    tbl = jnp.stack([cols[:, NAT_KH - 1 - v:2 * NAT_KH - 1 - v] for v in range(NAT_KH)], axis=1)
    tbl = tbl.transpose(0, 1, 3, 2, 4)
    return tbl.reshape(NAT_PAIRS, 2, NAT_KH, GRID_W, NAT_KH * GRID_W)


def _nat_kernel(n_rows, q_ref, k_ref, v_ref, bias_ref, o_ref):
    ri = pl.program_id(2)
    rows_per_step = q_ref.shape[1] // GRID_W
    win = NAT_KH * GRID_W
    lane = lax.broadcasted_iota(I32, (GRID_W, 2 * HEAD_DIM), 1)
    for j in range(rows_per_step):
        r = ri * rows_per_step + j
        rs = jnp.clip(r - NAT_KH // 2, 0, n_rows - NAT_KH)
        var = r - rs
        k0 = pl.multiple_of(rs * GRID_W, GRID_W)
        kw = k_ref[0, pl.ds(k0, win), :]
        vw = v_ref[0, pl.ds(k0, win), :]
        q = q_ref[0, j * GRID_W:(j + 1) * GRID_W, :]
        zero = jnp.zeros_like(q)
        outs = []
        for hh in range(2):
            keep = (lane < HEAD_DIM) if hh == 0 else (lane >= HEAD_DIM)
            s = lax.dot_general(jnp.where(keep, q, zero), kw, NT_DIMS, preferred_element_type=F32)
            s = s + bias_ref[0, hh, var]
            p = jnp.exp(s - jnp.max(s, axis=-1, keepdims=True))
            l = jnp.sum(p, axis=-1, keepdims=True)
            outs.append(jnp.dot(p.astype(BF16), vw, preferred_element_type=F32) / l)
        o_ref[0, j * GRID_W:(j + 1) * GRID_W, :] = jnp.where(lane < HEAD_DIM, outs[0], outs[1]).astype(BF16)


def _nat_attention(nq, nk, nv, rpb):
    B, L, _ = nq.shape
    n_rows = L // GRID_W
    assert n_rows >= NAT_KH
    rps = min(NAT_ROWS_PER_STEP, n_rows)
    tbl = _nat_bias_table(rpb)
    kv = pl.BlockSpec((1, L, 2 * HEAD_DIM), lambda b, p, i: (b, 0, p))
    qo = pl.BlockSpec((1, rps * GRID_W, 2 * HEAD_DIM), lambda b, p, i: (b, i, p))
    return pl.pallas_call(
        functools.partial(_nat_kernel, n_rows),
        grid=(B, NAT_PAIRS, n_rows // rps),
        in_specs=[qo, kv, kv,
                  pl.BlockSpec((1, 2, NAT_KH, GRID_W, NAT_KH * GRID_W), lambda b, p, i: (p, 0, 0, 0, 0))],
        out_specs=qo,
        out_shape=jax.ShapeDtypeStruct((B, L, NAT_WIDTH), BF16),
        compiler_params=pltpu.CompilerParams(
            dimension_semantics=("arbitrary", "arbitrary", "arbitrary"),
            vmem_limit_bytes=VMEM_LIMIT),
        name="nat_attn",
    )(nq, nk, nv, tbl)


def _out_kernel(od_ref, on_ref, x_ref, wt_ref, wb_ref, gpost_ref, gt1_ref, gpre_ref, sc2_ref, sh2_ref,
                wr_ref, br_ref, x1_ref, h2_ref, eidx_ref, gate_ref, rank_ref, cnt_ref, carry_ref):
    tm = x_ref.shape[0]

    @pl.when(pl.program_id(0) == 0)
    def _():
        carry_ref[...] = jnp.zeros_like(carry_ref)

    mix = (jnp.dot(od_ref[...], wt_ref[...], preferred_element_type=F32)
           + jnp.dot(on_ref[...], wb_ref[...], preferred_element_type=F32))
    x1 = x_ref[...] + gt1_ref[0] * (_rms(mix) * gpost_ref[...])
    x1_ref[...] = x1
    h2 = _rms(x1) * gpre_ref[...]
    h2 = h2 * (1.0 + sc2_ref[0]) + sh2_ref[0]
    h2_ref[...] = h2

    logits = jnp.dot(h2, wr_ref[...], preferred_element_type=F32,
                     precision=lax.Precision.HIGHEST) + br_ref[...]
    eio = lax.broadcasted_iota(I32, logits.shape, 1).astype(F32)
    onehot = jnp.zeros_like(logits)
    vals, idxs, sels = [], [], []
    cur = logits
    for _ in range(TOP_K):
        mx = jnp.max(cur, axis=-1, keepdims=True)
        idx = jnp.min(jnp.where(cur == mx, eio, float(N_EXPERTS)), axis=-1, keepdims=True)
        sel = eio == idx
        vals.append(mx)
        idxs.append(idx)
        sels.append(sel)
        cur = jnp.where(sel, -jnp.inf, cur)
        onehot = onehot + sel.astype(F32)

    ex = [jnp.exp(v - vals[0]) for v in vals]
    tot = ex[0] + ex[1] + ex[2] + ex[3]

    rr = lax.broadcasted_iota(I32, (tm, tm), 0)
    cc = lax.broadcasted_iota(I32, (tm, tm), 1)
    tri = (rr > cc).astype(BF16)
    carry = carry_ref[...]
    cum = jnp.dot(tri, onehot.astype(BF16), preferred_element_type=F32) + carry
    ranks = [jnp.sum(jnp.where(sel, cum, 0.0), axis=-1, keepdims=True) for sel in sels]
    carry = carry + jnp.sum(onehot, axis=0, keepdims=True)
    carry_ref[...] = carry
    cnt_ref[...] = carry

    kio = lax.broadcasted_iota(I32, (tm, TOP_K), 1)

    def pack(cols):
        out = jnp.broadcast_to(cols[TOP_K - 1], (tm, TOP_K))
        for k in range(TOP_K - 2, -1, -1):
            out = jnp.where(kio == k, cols[k], out)
        return out

    eidx_ref[...] = pack(idxs).astype(I32)
    gate_ref[...] = pack([e / tot for e in ex])
    rank_ref[...] = pack(ranks).astype(I32)


def _out_router(od, on, x, w_out, g_post, gt1, g_pre, sc2, sh2, w_router, b_router, tokens_per_batch):
    T, D = x.shape
    B = gt1.shape[0]
    tm = min(OUT_TM, tokens_per_batch)
    steps_per_batch = tokens_per_batch // tm
    wt = w_out[:DIFF_WIDTH].astype(BF16)
    wb = w_out[DIFF_WIDTH:].astype(BF16)
    rowblk = lambda w: pl.BlockSpec((tm, w), lambda i: (i, 0))
    const = lambda shape: pl.BlockSpec(shape, lambda i: (0,) * len(shape))
    modv = pl.BlockSpec((1, 1, D), lambda i: (i // steps_per_batch, 0, 0))
    return pl.pallas_call(
        _out_kernel,
        grid=(T // tm,),
        in_specs=[rowblk(DIFF_WIDTH), rowblk(NAT_WIDTH), rowblk(D),
                  const((DIFF_WIDTH, D)), const((NAT_WIDTH, D)), const((1, D)), modv, const((1, D)),
                  modv, modv, const((D, N_EXPERTS)), const((1, N_EXPERTS))],
        out_specs=[rowblk(D), rowblk(D), rowblk(TOP_K), rowblk(TOP_K), rowblk(TOP_K),
                   const((1, N_EXPERTS))],
        out_shape=[jax.ShapeDtypeStruct((T, D), F32), jax.ShapeDtypeStruct((T, D), F32),
                   jax.ShapeDtypeStruct((T, TOP_K), I32), jax.ShapeDtypeStruct((T, TOP_K), F32),
                   jax.ShapeDtypeStruct((T, TOP_K), I32), jax.ShapeDtypeStruct((1, N_EXPERTS), F32)],
        scratch_shapes=[pltpu.VMEM((1, N_EXPERTS), F32)],
        compiler_params=pltpu.CompilerParams(dimension_semantics=("arbitrary",),
                                             vmem_limit_bytes=VMEM_LIMIT),
        name="out_router",
    )(od, on, x, wt, wb, g_post.reshape(1, D), gt1.reshape(B, 1, D), g_pre.reshape(1, D),
      sc2.reshape(B, 1, D), sh2.reshape(B, 1, D), w_router, b_router.reshape(1, N_EXPERTS))


def _expert_kernel(be_ref, tok_ref, tok_next_ref, dst_ref, h2_hbm, w1_ref, b1g_ref, b1l_ref, w2_ref, b2_ref,
                   y_hbm, xbuf, ybuf, w1t, w2t, hh, gsem, ssem):
    i = pl.program_id(0)
    n = pl.num_programs(0)
    bm = xbuf.shape[1]
    ff = w2_ref.shape[1]
    slot = i % 2

    def gather_row(idx_ref, s, r):
        return pltpu.make_async_copy(h2_hbm.at[pl.ds(idx_ref[0, 0, r], 1), :],
                                     xbuf.at[s, pl.ds(r, 1), :], gsem.at[s])

    def scatter_row(r):
        return pltpu.make_async_copy(ybuf.at[pl.ds(r, 1), :],
                                     y_hbm.at[pl.ds(dst_ref[0, 0, r], 1), :], ssem.at[0])

    def wait_gather(s):
        pltpu.make_async_copy(h2_hbm.at[pl.ds(0, bm), :], xbuf.at[s], gsem.at[s]).wait()

    def wait_scatter():
        pltpu.make_async_copy(ybuf, y_hbm.at[pl.ds(0, bm), :], ssem.at[0]).wait()

    @pl.when(i == 0)
    def _():
        for r in range(bm):
            gather_row(tok_ref, 0, r).start()

    for r in range(bm):
        gather_row(tok_next_ref, 1 - slot, r).start()

    @pl.when((i == 0) | (be_ref[i] != be_ref[jnp.maximum(i - 1, 0)]))
    def _():
        cw = 256
        for c in range(w1_ref.shape[2] // cw):
            w1t[c * cw:(c + 1) * cw, :] = w1_ref[0, :, c * cw:(c + 1) * cw].T.astype(BF16)
        for c in range(w2_ref.shape[2] // cw):
            w2t[c * cw:(c + 1) * cw, :] = w2_ref[0, :, c * cw:(c + 1) * cw].T.astype(BF16)

    wait_gather(slot)
    x = xbuf[slot].astype(BF16)
    hh_t = lax.dot_general(w1t[...], x, NT_DIMS, preferred_element_type=F32)
    n_lane_tiles = bm // 128
    for j in range(n_lane_tiles):
        hh[j] = hh_t[:, j * 128:(j + 1) * 128]
    even = jnp.concatenate([hh[j, pl.ds(0, ff, stride=2), :] for j in range(n_lane_tiles)], axis=1)
    odd = jnp.concatenate([hh[j, pl.ds(1, ff, stride=2), :] for j in range(n_lane_tiles)], axis=1)
    glu = jnp.minimum(even + b1g_ref[0], SWIGLU_LIMIT)
    lin = jnp.clip(odd + b1l_ref[0], -SWIGLU_LIMIT, SWIGLU_LIMIT)
    act = glu * (1.0 / (1.0 + jnp.exp(-SWIGLU_ALPHA * glu))) * (lin + 1.0)
    y_t = jnp.dot(w2t[...], act.astype(BF16), preferred_element_type=F32)

    @pl.when(i > 0)
    def _():
        wait_scatter()

    ybuf[...] = y_t.T + b2_ref[0]
    for r in range(bm):
        scatter_row(r).start()

    @pl.when(i == n - 1)
    def _():
        wait_scatter()
        wait_gather(1 - slot)


def _experts(h2, blk_expert, tok_buf, dst_buf, w1, b1, w2, b2, n_rows_out):
    T, D = h2.shape
    n_blocks = blk_expert.shape[0]
    bm = MOE_BM
    F = w2.shape[1]
    b1g = b1[:, 0::2].reshape(N_EXPERTS, F, 1)
    b1l = b1[:, 1::2].reshape(N_EXPERTS, F, 1)
    tok3 = tok_buf.reshape(n_blocks, 1, bm)
    dst3 = dst_buf.reshape(n_blocks, 1, bm)
    smem_blk = lambda fn: pl.BlockSpec((1, 1, bm), fn, memory_space=pltpu.SMEM)
    grid_spec = pltpu.PrefetchScalarGridSpec(
        num_scalar_prefetch=1,
        grid=(n_blocks,),
        in_specs=[smem_blk(lambda i, be: (i, 0, 0)),
                  smem_blk(lambda i, be: (jnp.minimum(i + 1, n_blocks - 1), 0, 0)),
                  smem_blk(lambda i, be: (i, 0, 0)),
                  pl.BlockSpec(memory_space=pl.ANY),
                  pl.BlockSpec((1, D, 2 * F), lambda i, be: (be[i], 0, 0)),
                  pl.BlockSpec((1, F, 1), lambda i, be: (be[i], 0, 0)),
                  pl.BlockSpec((1, F, 1), lambda i, be: (be[i], 0, 0)),
                  pl.BlockSpec((1, F, D), lambda i, be: (be[i], 0, 0)),
                  pl.BlockSpec((1, 1, D), lambda i, be: (be[i], 0, 0))],
        out_specs=pl.BlockSpec(memory_space=pl.ANY),
        scratch_shapes=[pltpu.VMEM((2, bm, D), F32), pltpu.VMEM((bm, D), F32),
                        pltpu.VMEM((2 * F, D), BF16), pltpu.VMEM((D, F), BF16),
                        pltpu.VMEM((bm // 128, 2 * F, 128), F32),
                        pltpu.SemaphoreType.DMA((2,)), pltpu.SemaphoreType.DMA((1,))],
    )
    return pl.pallas_call(
        _expert_kernel,
        grid_spec=grid_spec,
        out_shape=jax.ShapeDtypeStruct((n_rows_out, D), F32),
        compiler_params=pltpu.CompilerParams(dimension_semantics=("arbitrary",),
                                             vmem_limit_bytes=EXPERT_VMEM_LIMIT),
        name="experts",
    )(blk_expert, tok3, tok3, dst3, h2, w1, b1g, b1l, w2, b2.reshape(N_EXPERTS, 1, D))


def _combine_kernel(y0_ref, y1_ref, y2_ref, y3_ref, gate_ref, x1_ref, gt2_ref, g_ref, o_ref):
    gates = gate_ref[...]
    f = gates[:, 0:1] * y0_ref[...]
    for k, y_ref in enumerate((y1_ref, y2_ref, y3_ref), start=1):
        f = f + gates[:, k:k + 1] * y_ref[...]
    o_ref[...] = x1_ref[...] + gt2_ref[0] * (_rms(f) * g_ref[...])


def _combine(y_tok, gates, x1, gt2, g_post, tokens_per_batch):
    T, D = x1.shape
    B = gt2.shape[0]
    tm = min(COMBINE_TM, tokens_per_batch)
    steps_per_batch = tokens_per_batch // tm
    steps = T // tm
    y_spec = lambda k: pl.BlockSpec((tm, D), lambda i: (k * steps + i, 0))
    return pl.pallas_call(
        _combine_kernel,
        grid=(steps,),
        in_specs=[y_spec(0), y_spec(1), y_spec(2), y_spec(3),
                  pl.BlockSpec((tm, TOP_K), lambda i: (i, 0)),
                  pl.BlockSpec((tm, D), lambda i: (i, 0)),
                  pl.BlockSpec((1, 1, D), lambda i: (i // steps_per_batch, 0, 0)),
                  pl.BlockSpec((1, D), lambda i: (0, 0))],
        out_specs=pl.BlockSpec((tm, D), lambda i: (i, 0)),
        out_shape=jax.ShapeDtypeStruct((T, D), F32),
        compiler_params=pltpu.CompilerParams(dimension_semantics=("arbitrary",),
                                             vmem_limit_bytes=VMEM_LIMIT),
        name="combine",
    )(y_tok, y_tok, y_tok, y_tok, gates, x1, gt2.reshape(B, 1, D), g_post.reshape(1, D))


def _dispatch_plan(eidx, rank, counts):
    T = eidx.shape[0]
    n_assign = eidx.size
    bm = MOE_BM
    cap = (n_assign + N_EXPERTS * (bm - 1) + bm - 1) // bm * bm
    n_blocks = cap // bm
    counts = counts.reshape(N_EXPERTS).astype(I32)
    padded = (counts + bm - 1) // bm * bm
    pad_end = jnp.cumsum(padded)
    pad_start = pad_end - padded
    eio = jnp.arange(N_EXPERTS, dtype=I32)
    start_of = jnp.sum(jnp.where(eidx[..., None] == eio, pad_start, 0), axis=-1)
    dest = (start_of + rank).reshape(-1)
    blk_start = jnp.arange(n_blocks, dtype=I32) * bm
    blk_expert = jnp.minimum(jnp.sum((blk_start[:, None] >= pad_end[None, :]).astype(I32), axis=-1),
                             N_EXPERTS - 1)
    inv = jnp.zeros((cap,), I32).at[dest].set(jnp.arange(1, n_assign + 1, dtype=I32))
    is_pad = inv == 0
    a = inv - 1
    tok_buf = jnp.where(is_pad, 0, a // TOP_K).astype(I32)
    pad_row = n_assign + jnp.cumsum(is_pad.astype(I32)) - 1
    dst_buf = jnp.where(is_pad, pad_row, (a % TOP_K) * T + a // TOP_K).astype(I32)
    return blk_expert, tok_buf, dst_buf, cap


def _layer(x, c, l, w_ada, b_ada, g_pre_mix, g_post_mix, w_in, w_out, lam_q1, lam_k1, lam_q2, lam_k2,
           g_subln, nat_rpb, g_pre_ffn, g_post_ffn, w_router, b_router, w1, b1, w2, b2):
    B, L, D = x.shape
    lam_init = 0.8 - 0.6 * math.exp(-0.3 * l)
    mod, lam = _ada(c, w_ada, b_ada, lam_q1, lam_k1, lam_q2, lam_k2, lam_init)
    sh1, sc1, gt1, sh2, sc2, gt2 = jnp.split(mod, 6, axis=-1)

    qT, kd, vT, nq, nk, nv = _inproj(x, g_pre_mix, sc1, sh1, w_in)
    o_diff = _diff_attention(qT, kd, vT, lam, g_subln, lam_init)
    o_nat = _nat_attention(nq, nk, nv, nat_rpb)

    T = B * L
    x1, h2, eidx, gates, rank, counts = _out_router(
        o_diff.reshape(T, DIFF_WIDTH), o_nat.reshape(T, NAT_WIDTH), x.reshape(T, D), w_out,
        g_post_mix, gt1, g_pre_ffn, sc2, sh2, w_router, b_router, L)
    blk_expert, tok_buf, dst_buf, n_rows_out = _dispatch_plan(eidx, rank, counts)
    y_tok = _experts(h2, blk_expert, tok_buf, dst_buf, w1, b1, w2, b2, n_rows_out)
    out = _combine(y_tok, gates, x1, gt2, g_post_ffn, L)
    return out.reshape(B, L, D)


def kernel(x, c, w_ada, b_ada, g_pre_mix, g_post_mix, w_in, w_out, lam_q1, lam_k1, lam_q2, lam_k2,
           g_subln, nat_rpb, g_pre_ffn, g_post_ffn, w_router, b_router, w1, b1, w2, b2):
    depth = w_ada.shape[0]
    for l in range(depth):
        x = _layer(x, c, l, w_ada[l], b_ada[l], g_pre_mix[l], g_post_mix[l], w_in[l], w_out[l],
                   lam_q1[l], lam_k1[l], lam_q2[l], lam_k2[l], g_subln[l], nat_rpb[l],
                   g_pre_ffn[l], g_post_ffn[l], w_router[l], b_router[l], w1[l], b1[l], w2[l], b2[l])
    return x
```

```python
import functools
import math

import jax
import jax.numpy as jnp
from jax import lax
from jax.experimental import pallas as pl
from jax.experimental.pallas import tpu as pltpu

F32 = jnp.float32
BF16 = jnp.bfloat16
I32 = jnp.int32

HEAD_DIM = 64
N_DIFF_HEADS = 4
DIFF_HEAD_W = 2 * HEAD_DIM
DIFF_WIDTH = N_DIFF_HEADS * DIFF_HEAD_W
N_NAT_HEADS = 8
NAT_WIDTH = N_NAT_HEADS * HEAD_DIM
NAT_PAIRS = N_NAT_HEADS // 2
GRID_W = 64
NAT_KH = 8
NAT_KW = 16
N_EXPERTS = 32
TOP_K = 4
SWIGLU_LIMIT = 7.0
SWIGLU_ALPHA = 1.702
RMS_EPS = 1e-6
NEG_BIG = -1e30

NT_DIMS = (((1,), (1,)), ((), ()))

ADA_TN = 1536
INPROJ_TM = 512
DIFF_TQ = 256
DIFF_TK = 512
NAT_ROWS_PER_STEP = 8
OUT_TM = 512
MOE_BM = 256
COMBINE_TM = 512
VMEM_LIMIT = 48 * 1024 * 1024
EXPERT_VMEM_LIMIT = 56 * 1024 * 1024


def _rms(x, axis=-1):
    return x * lax.rsqrt(jnp.mean(x * x, axis=axis, keepdims=True) + RMS_EPS)


def _ada_kernel(lam_init, c_ref, w_ref, b_ref, lq1_ref, lk1_ref, lq2_ref, lk2_ref, mod_ref, lam_ref):
    c = c_ref[...]
    s = c * (1.0 / (1.0 + jnp.exp(-c)))
    mod_ref[...] = jnp.dot(s, w_ref[...], preferred_element_type=F32,
                           precision=lax.Precision.HIGHEST) + b_ref[...]
    d1 = jnp.sum(lq1_ref[...] * lk1_ref[...], axis=-1, keepdims=True)
    d2 = jnp.sum(lq2_ref[...] * lk2_ref[...], axis=-1, keepdims=True)
    lam = jnp.exp(d1) - jnp.exp(d2) + lam_init
    lam_ref[...] = jnp.broadcast_to(lam, lam_ref.shape)


def _ada(c, w_ada, b_ada, lq1, lk1, lq2, lk2, lam_init):
    B, D = c.shape
    N = w_ada.shape[1]
    c8 = jnp.zeros((8, D), F32).at[:B].set(c)
    vec = pl.BlockSpec((1, HEAD_DIM), lambda j: (0, 0))
    mod, lam = pl.pallas_call(
        functools.partial(_ada_kernel, lam_init),
        grid=(N // ADA_TN,),
        in_specs=[pl.BlockSpec((8, D), lambda j: (0, 0)),
                  pl.BlockSpec((D, ADA_TN), lambda j: (0, j)),
                  pl.BlockSpec((1, ADA_TN), lambda j: (0, j)),
                  vec, vec, vec, vec],
        out_specs=[pl.BlockSpec((8, ADA_TN), lambda j: (0, j)),
                   pl.BlockSpec((8, 128), lambda j: (0, 0))],
        out_shape=[jax.ShapeDtypeStruct((8, N), F32), jax.ShapeDtypeStruct((8, 128), F32)],
        compiler_params=pltpu.CompilerParams(dimension_semantics=("arbitrary",),
                                             vmem_limit_bytes=VMEM_LIMIT),
        name="ada",
    )(c8, w_ada, b_ada.reshape(1, N), lq1.reshape(1, -1), lk1.reshape(1, -1),
      lq2.reshape(1, -1), lk2.reshape(1, -1))
    return mod[:B], lam


def _inproj_kernel(x_ref, g_ref, sc_ref, sh_ref, wqT_ref, wvT_ref, wn_ref,
                   qT_ref, k_ref, vT_ref, nq_ref, nk_ref, nv_ref):
    h = _rms(x_ref[0]) * g_ref[...]
    h = h * (1.0 + sc_ref[0]) + sh_ref[0]
    hb = h.astype(BF16)
    qT_ref[0] = lax.dot_general(wqT_ref[...], hb, NT_DIMS, preferred_element_type=F32).astype(BF16)
    vT_ref[0] = lax.dot_general(wvT_ref[...], hb, NT_DIMS, preferred_element_type=F32).astype(BF16)
    rest = jnp.dot(hb, wn_ref[...], preferred_element_type=F32).astype(BF16)
    k_ref[0] = rest[:, 0:512]
    nq_ref[0] = rest[:, 512:1024]
    nk_ref[0] = rest[:, 1024:1536]
    nv_ref[0] = rest[:, 1536:2048]


def _inproj(x, g_pre, sc1, sh1, w_in):
    B, L, D = x.shape
    tm = INPROJ_TM
    scale = HEAD_DIM ** -0.5
    wqT = (w_in[:, 0:512] * scale).T.astype(BF16)
    wvT = w_in[:, 1024:1536].T.astype(BF16)
    wn = jnp.concatenate([w_in[:, 512:1024], w_in[:, 1536:2048] * scale, w_in[:, 2048:3072]],
                         axis=1).astype(BF16)
    row_major = pl.BlockSpec((1, tm, 512), lambda b, i: (b, i, 0))
    col_major = pl.BlockSpec((1, 512, tm), lambda b, i: (b, 0, i))
    modv = pl.BlockSpec((1, 1, D), lambda b, i: (b, 0, 0))
    rm_shape = jax.ShapeDtypeStruct((B, L, 512), BF16)
    cm_shape = jax.ShapeDtypeStruct((B, 512, L), BF16)
    return pl.pallas_call(
        _inproj_kernel,
        grid=(B, L // tm),
        in_specs=[pl.BlockSpec((1, tm, D), lambda b, i: (b, i, 0)),
                  pl.BlockSpec((1, D), lambda b, i: (0, 0)),
                  modv, modv,
                  pl.BlockSpec((512, D), lambda b, i: (0, 0)),
                  pl.BlockSpec((512, D), lambda b, i: (0, 0)),
                  pl.BlockSpec((D, 2048), lambda b, i: (0, 0))],
        out_specs=[col_major, row_major, col_major, row_major, row_major, row_major],
        out_shape=[cm_shape, rm_shape, cm_shape, rm_shape, rm_shape, rm_shape],
        compiler_params=pltpu.CompilerParams(dimension_semantics=("arbitrary", "arbitrary"),
                                             vmem_limit_bytes=VMEM_LIMIT),
        name="inproj",
    )(x, g_pre.reshape(1, D), sc1.reshape(B, 1, D), sh1.reshape(B, 1, D), wqT, wvT, wn)


def _diff_kernel(lam_init, n_kchunks, slopes_ref, lam_ref, qT_ref, k_ref, vT_ref, g_ref, o_ref,
                 acc_ref, m_ref, l_ref, s0_ref, qm_ref, s_ref, p_ref, a_ref):
    tq = qT_ref.shape[2]
    tk = s0_ref.shape[0]
    h = pl.program_id(1)
    q0 = pl.program_id(2) * tq
    slope = slopes_ref[h]

    qT = qT_ref[0]
    row = lax.broadcasted_iota(I32, qT.shape, 0)
    zero = jnp.zeros_like(qT)
    qm_ref[0] = jnp.where(row < HEAD_DIM, qT, zero)
    qm_ref[1] = jnp.where(row >= HEAD_DIM, qT, zero)

    kk = lax.broadcasted_iota(I32, (tk, tq), 0)
    qq = lax.broadcasted_iota(I32, (tk, tq), 1)
    s0_ref[...] = slope * (kk - qq).astype(F32)
    acc_ref[...] = jnp.zeros_like(acc_ref)
    l_ref[...] = jnp.zeros_like(l_ref)
    m_ref[...] = jnp.full_like(m_ref, NEG_BIG)

    def start(c):
        return c * tk if isinstance(c, int) else pl.multiple_of(c * tk, tk)

    def scores(c, slot):
        kb = k_ref[0, pl.ds(start(c), tk), :]
        for mi in range(2):
            s_ref[slot, mi] = jnp.dot(kb, qm_ref[mi], preferred_element_type=F32)

    def softmax(c, slot):
        bias = jnp.abs(s0_ref[...] + slope * (start(c) - q0).astype(F32))
        for mi in range(2):
            s = s_ref[slot, mi] - bias
            m_old = m_ref[mi]
            m_new = jnp.maximum(m_old, jnp.max(s, axis=0, keepdims=True))
            alpha = jnp.exp(m_old - m_new)
            p = jnp.exp(s - m_new)
            l_ref[mi] = alpha * l_ref[mi] + jnp.sum(p, axis=0, keepdims=True)
            m_ref[mi] = m_new
            a_ref[slot, mi] = alpha
            p_ref[slot, mi] = p.astype(BF16)

    def values(c, slot):
        vb = vT_ref[0, :, pl.ds(start(c), tk)]
        for mi in range(2):
            acc_ref[mi] = a_ref[slot, mi] * acc_ref[mi] + jnp.dot(vb, p_ref[slot, mi],
                                                                  preferred_element_type=F32)

    n = n_kchunks
    scores(0, 0)
    scores(1, 1)
    softmax(0, 0)

    def pair(j, carry):
        c = 2 * j + 1
        scores(c + 1, 0)
        softmax(c, 1)
        values(c - 1, 0)
        scores(c + 2, 1)
        softmax(c + 1, 0)
        values(c, 1)
        return carry

    lax.fori_loop(0, (n - 2) // 2, pair, 0)
    softmax(n - 1, 1)
    values(n - 2, 0)
    values(n - 1, 1)

    lam = lam_ref[0:1, 0:1]
    o = acc_ref[0] / l_ref[0] - lam * (acc_ref[1] / l_ref[1])
    y = _rms(o, axis=0) * g_ref[...] * (1.0 - lam_init)
    o_ref[0] = y.T.astype(BF16)


def _diff_attention(qT, k, vT, lam, g_sub, lam_init):
    B, _, L = qT.shape
    tq, tk = min(DIFF_TQ, L), min(DIFF_TK, L)
    i = jnp.arange(1, N_DIFF_HEADS + 1, dtype=F32)
    slopes = jnp.exp2(-8.0 * i / N_DIFF_HEADS)
    return pl.pallas_call(
        functools.partial(_diff_kernel, lam_init, L // tk),
        grid=(B, N_DIFF_HEADS, L // tq),
        in_specs=[pl.BlockSpec(memory_space=pltpu.SMEM),
                  pl.BlockSpec((8, 128), lambda b, h, i: (0, 0)),
                  pl.BlockSpec((1, DIFF_HEAD_W, tq), lambda b, h, i: (b, h, i)),
                  pl.BlockSpec((1, L, DIFF_HEAD_W), lambda b, h, i: (b, 0, h)),
                  pl.BlockSpec((1, DIFF_HEAD_W, L), lambda b, h, i: (b, h, 0)),
                  pl.BlockSpec((DIFF_HEAD_W, 1), lambda b, h, i: (0, 0))],
        out_specs=pl.BlockSpec((1, tq, DIFF_HEAD_W), lambda b, h, i: (b, i, h)),
        out_shape=jax.ShapeDtypeStruct((B, L, DIFF_WIDTH), BF16),
        scratch_shapes=[pltpu.VMEM((2, DIFF_HEAD_W, tq), F32),
                        pltpu.VMEM((2, 1, tq), F32),
                        pltpu.VMEM((2, 1, tq), F32),
                        pltpu.VMEM((tk, tq), F32),
                        pltpu.VMEM((2, DIFF_HEAD_W, tq), BF16),
                        pltpu.VMEM((2, 2, tk, tq), F32),
                        pltpu.VMEM((2, 2, tk, tq), BF16),
                        pltpu.VMEM((2, 2, 1, tq), F32)],
        compiler_params=pltpu.CompilerParams(
            dimension_semantics=("arbitrary", "arbitrary", "arbitrary"),
            vmem_limit_bytes=VMEM_LIMIT),
        name="diff_attn",
    )(slopes, lam, qT, k, vT, g_sub.reshape(DIFF_HEAD_W, 1))


def _nat_bias_table(rpb):
    c = jnp.arange(GRID_W)
    col_start = jnp.clip(c - NAT_KW // 2, 0, GRID_W - NAT_KW)
    col_in = (c[None, :] >= col_start[:, None]) & (c[None, :] < col_start[:, None] + NAT_KW)
    dc = jnp.clip(c[None, :] - c[:, None], -(NAT_KW - 1), NAT_KW - 1) + (NAT_KW - 1)
    rpb = rpb.astype(F32)
    cols = jnp.zeros(rpb.shape[:2] + (GRID_W, GRID_W), F32)
    for j in range(2 * NAT_KW - 1):
        cols = cols + jnp.where(dc == j, rpb[:, :, j][:, :, None, None], 0.0)
    cols = jnp.where(col_in[None, None], cols, NEG_BIG)
    tbl = jnp.stack([cols[:, NAT_KH - 1 - v:2 * NAT_KH - 1 - v] for v in range(NAT_KH)], axis=1)
    tbl = tbl.transpose(0, 1, 3, 2, 4)
    return tbl.reshape(NAT_PAIRS, 2, NAT_KH, GRID_W, NAT_KH * GRID_W)


def _nat_kernel(n_rows, q_ref, k_ref, v_ref, bias_ref, o_ref):
    ri = pl.program_id(2)
    rows_per_step = q_ref.shape[1] // GRID_W
    win = NAT_KH * GRID_W
    lane = lax.broadcasted_iota(I32, (GRID_W, 2 * HEAD_DIM), 1)
    for j in range(rows_per_step):
        r = ri * rows_per_step + j
        rs = jnp.clip(r - NAT_KH // 2, 0, n_rows - NAT_KH)
        var = r - rs
        k0 = pl.multiple_of(rs * GRID_W, GRID_W)
        kw = k_ref[0, pl.ds(k0, win), :]
        vw = v_ref[0, pl.ds(k0, win), :]
        q = q_ref[0, j * GRID_W:(j + 1) * GRID_W, :]
        zero = jnp.zeros_like(q)
        outs = []
        for hh in range(2):
            keep = (lane < HEAD_DIM) if hh == 0 else (lane >= HEAD_DIM)
            s = lax.dot_general(jnp.where(keep, q, zero), kw, NT_DIMS, preferred_element_type=F32)
            s = s + bias_ref[0, hh, var]
            p = jnp.exp(s - jnp.max(s, axis=-1, keepdims=True))
            l = jnp.sum(p, axis=-1, keepdims=True)
            outs.append(jnp.dot(p.astype(BF16), vw, preferred_element_type=F32) / l)
        o_ref[0, j * GRID_W:(j + 1) * GRID_W, :] = jnp.where(lane < HEAD_DIM, outs[0], outs[1]).astype(BF16)


def _nat_attention(nq, nk, nv, rpb):
    B, L, _ = nq.shape
    n_rows = L // GRID_W
    assert n_rows >= NAT_KH
    rps = min(NAT_ROWS_PER_STEP, n_rows)
    tbl = _nat_bias_table(rpb)
    kv = pl.BlockSpec((1, L, 2 * HEAD_DIM), lambda b, p, i: (b, 0, p))
    qo = pl.BlockSpec((1, rps * GRID_W, 2 * HEAD_DIM), lambda b, p, i: (b, i, p))
    return pl.pallas_call(
        functools.partial(_nat_kernel, n_rows),
        grid=(B, NAT_PAIRS, n_rows // rps),
        in_specs=[qo, kv, kv,
                  pl.BlockSpec((1, 2, NAT_KH, GRID_W, NAT_KH * GRID_W), lambda b, p, i: (p, 0, 0, 0, 0))],
        out_specs=qo,
        out_shape=jax.ShapeDtypeStruct((B, L, NAT_WIDTH), BF16),
        compiler_params=pltpu.CompilerParams(
            dimension_semantics=("arbitrary", "arbitrary", "arbitrary"),
            vmem_limit_bytes=VMEM_LIMIT),
        name="nat_attn",
    )(nq, nk, nv, tbl)


def _out_kernel(od_ref, on_ref, x_ref, wt_ref, wb_ref, gpost_ref, gt1_ref, gpre_ref, sc2_ref, sh2_ref,
                wr_ref, br_ref, x1_ref, h2_ref, eidx_ref, gate_ref, rank_ref, cnt_ref, carry_ref):
    tm = x_ref.shape[0]

    @pl.when(pl.program_id(0) == 0)
    def _():
        carry_ref[...] = jnp.zeros_like(carry_ref)

    mix = (jnp.dot(od_ref[...], wt_ref[...], preferred_element_type=F32)
           + jnp.dot(on_ref[...], wb_ref[...], preferred_element_type=F32))
    x1 = x_ref[...] + gt1_ref[0] * (_rms(mix) * gpost_ref[...])
    x1_ref[...] = x1
    h2 = _rms(x1) * gpre_ref[...]
    h2 = h2 * (1.0 + sc2_ref[0]) + sh2_ref[0]
    h2_ref[...] = h2

    logits = jnp.dot(h2, wr_ref[...], preferred_element_type=F32,
                     precision=lax.Precision.HIGHEST) + br_ref[...]
    eio = lax.broadcasted_iota(I32, logits.shape, 1).astype(F32)
    onehot = jnp.zeros_like(logits)
    vals, idxs, sels = [], [], []
    cur = logits
    for _ in range(TOP_K):
        mx = jnp.max(cur, axis=-1, keepdims=True)
        idx = jnp.min(jnp.where(cur == mx, eio, float(N_EXPERTS)), axis=-1, keepdims=True)
        sel = eio == idx
        vals.append(mx)
        idxs.append(idx)
        sels.append(sel)
        cur = jnp.where(sel, -jnp.inf, cur)
        onehot = onehot + sel.astype(F32)

    ex = [jnp.exp(v - vals[0]) for v in vals]
    tot = ex[0] + ex[1] + ex[2] + ex[3]

    rr = lax.broadcasted_iota(I32, (tm, tm), 0)
    cc = lax.broadcasted_iota(I32, (tm, tm), 1)
    tri = (rr > cc).astype(BF16)
    carry = carry_ref[...]
    cum = jnp.dot(tri, onehot.astype(BF16), preferred_element_type=F32) + carry
    ranks = [jnp.sum(jnp.where(sel, cum, 0.0), axis=-1, keepdims=True) for sel in sels]
    carry = carry + jnp.sum(onehot, axis=0, keepdims=True)
    carry_ref[...] = carry
    cnt_ref[...] = carry

    kio = lax.broadcasted_iota(I32, (tm, TOP_K), 1)

    def pack(cols):
        out = jnp.broadcast_to(cols[TOP_K - 1], (tm, TOP_K))
        for k in range(TOP_K - 2, -1, -1):
            out = jnp.where(kio == k, cols[k], out)
        return out

    eidx_ref[...] = pack(idxs).astype(I32)
    gate_ref[...] = pack([e / tot for e in ex])
    rank_ref[...] = pack(ranks).astype(I32)


def _out_router(od, on, x, w_out, g_post, gt1, g_pre, sc2, sh2, w_router, b_router, tokens_per_batch):
    T, D = x.shape
    B = gt1.shape[0]
    tm = min(OUT_TM, tokens_per_batch)
    steps_per_batch = tokens_per_batch // tm
    wt = w_out[:DIFF_WIDTH].astype(BF16)
    wb = w_out[DIFF_WIDTH:].astype(BF16)
    rowblk = lambda w: pl.BlockSpec((tm, w), lambda i: (i, 0))
    const = lambda shape: pl.BlockSpec(shape, lambda i: (0,) * len(shape))
    modv = pl.BlockSpec((1, 1, D), lambda i: (i // steps_per_batch, 0, 0))
    return pl.pallas_call(
        _out_kernel,
        grid=(T // tm,),
        in_specs=[rowblk(DIFF_WIDTH), rowblk(NAT_WIDTH), rowblk(D),
                  const((DIFF_WIDTH, D)), const((NAT_WIDTH, D)), const((1, D)), modv, const((1, D)),
                  modv, modv, const((D, N_EXPERTS)), const((1, N_EXPERTS))],
        out_specs=[rowblk(D), rowblk(D), rowblk(TOP_K), rowblk(TOP_K), rowblk(TOP_K),
                   const((1, N_EXPERTS))],
        out_shape=[jax.ShapeDtypeStruct((T, D), F32), jax.ShapeDtypeStruct((T, D), F32),
                   jax.ShapeDtypeStruct((T, TOP_K), I32), jax.ShapeDtypeStruct((T, TOP_K), F32),
                   jax.ShapeDtypeStruct((T, TOP_K), I32), jax.ShapeDtypeStruct((1, N_EXPERTS), F32)],
        scratch_shapes=[pltpu.VMEM((1, N_EXPERTS), F32)],
        compiler_params=pltpu.CompilerParams(dimension_semantics=("arbitrary",),
                                             vmem_limit_bytes=VMEM_LIMIT),
        name="out_router",
    )(od, on, x, wt, wb, g_post.reshape(1, D), gt1.reshape(B, 1, D), g_pre.reshape(1, D),
      sc2.reshape(B, 1, D), sh2.reshape(B, 1, D), w_router, b_router.reshape(1, N_EXPERTS))


def _expert_kernel(be_ref, tok_ref, tok_next_ref, dst_prev_ref, dst_ref, h2_hbm, w1_ref, b1g_ref, b1l_ref,
                   w2_ref, b2_ref, y_hbm, xbuf, ybuf, xb, w1t, w2t, hh, gsem, ssem):
    i = pl.program_id(0)
    n = pl.num_programs(0)
    bm = xbuf.shape[1]
    ff = w2_ref.shape[1]
    slot = i % 2

    def gather_row(idx_ref, s, r):
        return pltpu.make_async_copy(h2_hbm.at[pl.ds(idx_ref[0, 0, r], 1), :],
                                     xbuf.at[s, pl.ds(r, 1), :], gsem.at[s])

    def scatter_row(idx_ref, s, r):
        return pltpu.make_async_copy(ybuf.at[s, pl.ds(r, 1), :],
                                     y_hbm.at[pl.ds(idx_ref[0, 0, r], 1), :], ssem.at[0])

    def wait_gather(s):
        pltpu.make_async_copy(h2_hbm.at[pl.ds(0, bm), :], xbuf.at[s], gsem.at[s]).wait()

    def wait_scatter(s):
        pltpu.make_async_copy(ybuf.at[s], y_hbm.at[pl.ds(0, bm), :], ssem.at[0]).wait()

    @pl.when(i == 0)
    def _():
        ybuf[1] = jnp.zeros(ybuf.shape[1:], F32)
        for r in range(bm):
            gather_row(tok_ref, 0, r).start()

    @pl.when(i > 0)
    def _():
        wait_scatter(slot)

    @pl.when((i == 0) | (be_ref[i] != be_ref[jnp.maximum(i - 1, 0)]))
    def _():
        cw = 256
        for c in range(w1_ref.shape[2] // cw):
            w1t[c * cw:(c + 1) * cw, :] = w1_ref[0, :, c * cw:(c + 1) * cw].T.astype(BF16)
        for c in range(w2_ref.shape[2] // cw):
            w2t[c * cw:(c + 1) * cw, :] = w2_ref[0, :, c * cw:(c + 1) * cw].T.astype(BF16)

    wait_gather(slot)
    xb[...] = xbuf[slot].astype(BF16)

    n_lane_tiles = bm // 128

    @pl.when(i >= 0)
    def _():
        for r in range(bm):
            gather_row(tok_next_ref, 1 - slot, r).start()
        hh_t = lax.dot_general(w1t[...], xb[...], NT_DIMS, preferred_element_type=F32)
        for j in range(n_lane_tiles):
            hh[j] = hh_t[:, j * 128:(j + 1) * 128]

    for r in range(bm):
        scatter_row(dst_prev_ref, 1 - slot, r).start()
    even = jnp.concatenate([hh[j, pl.ds(0, ff, stride=2), :] for j in range(n_lane_tiles)], axis=1)
    odd = jnp.concatenate([hh[j, pl.ds(1, ff, stride=2), :] for j in range(n_lane_tiles)], axis=1)
    glu = jnp.minimum(even + b1g_ref[0], SWIGLU_LIMIT)
    lin = jnp.clip(odd + b1l_ref[0], -SWIGLU_LIMIT, SWIGLU_LIMIT)
    act = glu * (1.0 / (1.0 + jnp.exp(-SWIGLU_ALPHA * glu))) * (lin + 1.0)
    y_t = jnp.dot(w2t[...], act.astype(BF16), preferred_element_type=F32)

    ybuf[slot] = y_t.T + b2_ref[0]

    @pl.when(i == n - 1)
    def _():
        wait_scatter(1 - slot)
        for r in range(bm):
            scatter_row(dst_ref, slot, r).start()
        wait_scatter(slot)
        wait_gather(1 - slot)


def _experts(h2, blk_expert, tok_buf, dst_buf, w1, b1, w2, b2, n_rows_out):
    T, D = h2.shape
    n_blocks = blk_expert.shape[0]
    bm = MOE_BM
    F = w2.shape[1]
    b1g = b1[:, 0::2].reshape(N_EXPERTS, F, 1)
    b1l = b1[:, 1::2].reshape(N_EXPERTS, F, 1)
    tok3 = tok_buf.reshape(n_blocks, 1, bm)
    spare = n_rows_out - bm + jnp.arange(bm, dtype=I32)
    dst3 = jnp.concatenate([spare, dst_buf]).reshape(n_blocks + 1, 1, bm)
    smem_blk = lambda fn: pl.BlockSpec((1, 1, bm), fn, memory_space=pltpu.SMEM)
    grid_spec = pltpu.PrefetchScalarGridSpec(
        num_scalar_prefetch=1,
        grid=(n_blocks,),
        in_specs=[smem_blk(lambda i, be: (i, 0, 0)),
                  smem_blk(lambda i, be: (jnp.minimum(i + 1, n_blocks - 1), 0, 0)),
                  smem_blk(lambda i, be: (i, 0, 0)),
                  smem_blk(lambda i, be: (i + 1, 0, 0)),
                  pl.BlockSpec(memory_space=pl.ANY),
                  pl.BlockSpec((1, D, 2 * F), lambda i, be: (be[i], 0, 0)),
                  pl.BlockSpec((1, F, 1), lambda i, be: (be[i], 0, 0)),
                  pl.BlockSpec((1, F, 1), lambda i, be: (be[i], 0, 0)),
                  pl.BlockSpec((1, F, D), lambda i, be: (be[i], 0, 0)),
                  pl.BlockSpec((1, 1, D), lambda i, be: (be[i], 0, 0))],
        out_specs=pl.BlockSpec(memory_space=pl.ANY),
        scratch_shapes=[pltpu.VMEM((2, bm, D), F32), pltpu.VMEM((2, bm, D), F32),
                        pltpu.VMEM((bm, D), BF16),
                        pltpu.VMEM((2 * F, D), BF16), pltpu.VMEM((D, F), BF16),
                        pltpu.VMEM((bm // 128, 2 * F, 128), F32),
                        pltpu.SemaphoreType.DMA((2,)), pltpu.SemaphoreType.DMA((1,))],
    )
    return pl.pallas_call(
        _expert_kernel,
        grid_spec=grid_spec,
        out_shape=jax.ShapeDtypeStruct((n_rows_out, D), F32),
        compiler_params=pltpu.CompilerParams(dimension_semantics=("arbitrary",),
                                             vmem_limit_bytes=EXPERT_VMEM_LIMIT),
        name="experts",
    )(blk_expert, tok3, tok3, dst3, dst3, h2, w1, b1g, b1l, w2, b2.reshape(N_EXPERTS, 1, D))


def _combine_kernel(y0_ref, y1_ref, y2_ref, y3_ref, gate_ref, x1_ref, gt2_ref, g_ref, o_ref):
    gates = gate_ref[...]
    f = gates[:, 0:1] * y0_ref[...]
    for k, y_ref in enumerate((y1_ref, y2_ref, y3_ref), start=1):
        f = f + gates[:, k:k + 1] * y_ref[...]
    o_ref[...] = x1_ref[...] + gt2_ref[0] * (_rms(f) * g_ref[...])


def _combine(y_tok, gates, x1, gt2, g_post, tokens_per_batch):
    T, D = x1.shape
    B = gt2.shape[0]
    tm = min(COMBINE_TM, tokens_per_batch)
    steps_per_batch = tokens_per_batch // tm
    steps = T // tm
    y_spec = lambda k: pl.BlockSpec((tm, D), lambda i: (k * steps + i, 0))
    return pl.pallas_call(
        _combine_kernel,
        grid=(steps,),
        in_specs=[y_spec(0), y_spec(1), y_spec(2), y_spec(3),
                  pl.BlockSpec((tm, TOP_K), lambda i: (i, 0)),
                  pl.BlockSpec((tm, D), lambda i: (i, 0)),
                  pl.BlockSpec((1, 1, D), lambda i: (i // steps_per_batch, 0, 0)),
                  pl.BlockSpec((1, D), lambda i: (0, 0))],
        out_specs=pl.BlockSpec((tm, D), lambda i: (i, 0)),
        out_shape=jax.ShapeDtypeStruct((T, D), F32),
        compiler_params=pltpu.CompilerParams(dimension_semantics=("arbitrary",),
                                             vmem_limit_bytes=VMEM_LIMIT),
        name="combine",
    )(y_tok, y_tok, y_tok, y_tok, gates, x1, gt2.reshape(B, 1, D), g_post.reshape(1, D))


def _dispatch_plan(eidx, rank, counts):
    T = eidx.shape[0]
    n_assign = eidx.size
    bm = MOE_BM
    cap = (n_assign + N_EXPERTS * (bm - 1) + bm - 1) // bm * bm
    n_blocks = cap // bm
    counts = counts.reshape(N_EXPERTS).astype(I32)
    padded = (counts + bm - 1) // bm * bm
    pad_end = jnp.cumsum(padded)
    pad_start = pad_end - padded
    eio = jnp.arange(N_EXPERTS, dtype=I32)
    start_of = jnp.sum(jnp.where(eidx[..., None] == eio, pad_start, 0), axis=-1)
    dest = (start_of + rank).reshape(-1)
    blk_start = jnp.arange(n_blocks, dtype=I32) * bm
    blk_expert = jnp.minimum(jnp.sum((blk_start[:, None] >= pad_end[None, :]).astype(I32), axis=-1),
                             N_EXPERTS - 1)
    inv = jnp.zeros((cap,), I32).at[dest].set(jnp.arange(1, n_assign + 1, dtype=I32))
    is_pad = inv == 0
    a = inv - 1
    tok_buf = jnp.where(is_pad, 0, a // TOP_K).astype(I32)
    pad_row = n_assign + jnp.cumsum(is_pad.astype(I32)) - 1
    dst_buf = jnp.where(is_pad, pad_row, (a % TOP_K) * T + a // TOP_K).astype(I32)
    return blk_expert, tok_buf, dst_buf, cap + bm


def _layer(x, c, l, w_ada, b_ada, g_pre_mix, g_post_mix, w_in, w_out, lam_q1, lam_k1, lam_q2, lam_k2,
           g_subln, nat_rpb, g_pre_ffn, g_post_ffn, w_router, b_router, w1, b1, w2, b2):
    B, L, D = x.shape
    lam_init = 0.8 - 0.6 * math.exp(-0.3 * l)
    mod, lam = _ada(c, w_ada, b_ada, lam_q1, lam_k1, lam_q2, lam_k2, lam_init)
    sh1, sc1, gt1, sh2, sc2, gt2 = jnp.split(mod, 6, axis=-1)

    qT, kd, vT, nq, nk, nv = _inproj(x, g_pre_mix, sc1, sh1, w_in)
    o_diff = _diff_attention(qT, kd, vT, lam, g_subln, lam_init)
    o_nat = _nat_attention(nq, nk, nv, nat_rpb)

    T = B * L
    x1, h2, eidx, gates, rank, counts = _out_router(
        o_diff.reshape(T, DIFF_WIDTH), o_nat.reshape(T, NAT_WIDTH), x.reshape(T, D), w_out,
        g_post_mix, gt1, g_pre_ffn, sc2, sh2, w_router, b_router, L)
    blk_expert, tok_buf, dst_buf, n_rows_out = _dispatch_plan(eidx, rank, counts)
    y_tok = _experts(h2, blk_expert, tok_buf, dst_buf, w1, b1, w2, b2, n_rows_out)
    out = _combine(y_tok, gates, x1, gt2, g_post_ffn, L)
    return out.reshape(B, L, D)


def kernel(x, c, w_ada, b_ada, g_pre_mix, g_post_mix, w_in, w_out, lam_q1, lam_k1, lam_q2, lam_k2,
           g_subln, nat_rpb, g_pre_ffn, g_post_ffn, w_router, b_router, w1, b1, w2, b2):
    depth = w_ada.shape[0]
    for l in range(depth):
        x = _layer(x, c, l, w_ada[l], b_ada[l], g_pre_mix[l], g_post_mix[l], w_in[l], w_out[l],
                   lam_q1[l], lam_k1[l], lam_q2[l], lam_k2[l], g_subln[l], nat_rpb[l],
                   g_pre_ffn[l], g_post_ffn[l], w_router[l], b_router[l], w1[l], b1[l], w2[l], b2[l])
    return x
```

```python
import functools
import math

import jax
import jax.numpy as jnp
from jax import lax
from jax.experimental import pallas as pl
from jax.experimental.pallas import tpu as pltpu

F32 = jnp.float32
BF16 = jnp.bfloat16
I32 = jnp.int32

HEAD_DIM = 64
N_DIFF_HEADS = 4
DIFF_HEAD_W = 2 * HEAD_DIM
DIFF_WIDTH = N_DIFF_HEADS * DIFF_HEAD_W
N_NAT_HEADS = 8
NAT_WIDTH = N_NAT_HEADS * HEAD_DIM
NAT_PAIRS = N_NAT_HEADS // 2
GRID_W = 64
NAT_KH = 8
NAT_KW = 16
N_EXPERTS = 32
TOP_K = 4
SWIGLU_LIMIT = 7.0
SWIGLU_ALPHA = 1.702
RMS_EPS = 1e-6
NEG_BIG = -1e30
LOG2E = 1.4426950408889634

NT_DIMS = (((1,), (1,)), ((), ()))

ADA_TN = 1536
INPROJ_TM = 512
DIFF_TQ = 256
DIFF_TK = 512
NAT_ROWS_PER_STEP = 8
OUT_TM = 512
MOE_BM = 256
COMBINE_TM = 512
VMEM_LIMIT = 48 * 1024 * 1024
EXPERT_VMEM_LIMIT = 56 * 1024 * 1024


def _rms(x, axis=-1):
    return x * lax.rsqrt(jnp.mean(x * x, axis=axis, keepdims=True) + RMS_EPS)


def _ada_kernel(lam_init, c_ref, w_ref, b_ref, lq1_ref, lk1_ref, lq2_ref, lk2_ref, mod_ref, lam_ref):
    c = c_ref[...]
    s = c * (1.0 / (1.0 + jnp.exp(-c)))
    mod_ref[...] = jnp.dot(s, w_ref[...], preferred_element_type=F32,
                           precision=lax.Precision.HIGHEST) + b_ref[...]
    d1 = jnp.sum(lq1_ref[...] * lk1_ref[...], axis=-1, keepdims=True)
    d2 = jnp.sum(lq2_ref[...] * lk2_ref[...], axis=-1, keepdims=True)
    lam = jnp.exp(d1) - jnp.exp(d2) + lam_init
    lam_ref[...] = jnp.broadcast_to(lam, lam_ref.shape)


def _ada(c, w_ada, b_ada, lq1, lk1, lq2, lk2, lam_init):
    B, D = c.shape
    N = w_ada.shape[1]
    c8 = jnp.zeros((8, D), F32).at[:B].set(c)
    vec = pl.BlockSpec((1, HEAD_DIM), lambda j: (0, 0))
    mod, lam = pl.pallas_call(
        functools.partial(_ada_kernel, lam_init),
        grid=(N // ADA_TN,),
        in_specs=[pl.BlockSpec((8, D), lambda j: (0, 0)),
                  pl.BlockSpec((D, ADA_TN), lambda j: (0, j)),
                  pl.BlockSpec((1, ADA_TN), lambda j: (0, j)),
                  vec, vec, vec, vec],
        out_specs=[pl.BlockSpec((8, ADA_TN), lambda j: (0, j)),
                   pl.BlockSpec((8, 128), lambda j: (0, 0))],
        out_shape=[jax.ShapeDtypeStruct((8, N), F32), jax.ShapeDtypeStruct((8, 128), F32)],
        compiler_params=pltpu.CompilerParams(dimension_semantics=("arbitrary",),
                                             vmem_limit_bytes=VMEM_LIMIT),
        name="ada",
    )(c8, w_ada, b_ada.reshape(1, N), lq1.reshape(1, -1), lk1.reshape(1, -1),
      lq2.reshape(1, -1), lk2.reshape(1, -1))
    return mod[:B], lam


def _inproj_kernel(x_ref, g_ref, sc_ref, sh_ref, wqT_ref, wvT_ref, wn_ref,
                   qT_ref, k_ref, vT_ref, nq_ref, nk_ref, nv_ref):
    h = _rms(x_ref[0]) * g_ref[...]
    h = h * (1.0 + sc_ref[0]) + sh_ref[0]
    hb = h.astype(BF16)
    qT_ref[0] = lax.dot_general(wqT_ref[...], hb, NT_DIMS, preferred_element_type=F32).astype(BF16)
    vT_ref[0] = lax.dot_general(wvT_ref[...], hb, NT_DIMS, preferred_element_type=F32).astype(BF16)
    rest = jnp.dot(hb, wn_ref[...], preferred_element_type=F32).astype(BF16)
    k_ref[0] = rest[:, 0:512]
    nq_ref[0] = rest[:, 512:1024]
    nk_ref[0] = rest[:, 1024:1536]
    nv_ref[0] = rest[:, 1536:2048]


def _inproj(x, g_pre, sc1, sh1, w_in):
    B, L, D = x.shape
    tm = INPROJ_TM
    scale = HEAD_DIM ** -0.5
    wqT = (w_in[:, 0:512] * (scale * LOG2E)).T.astype(BF16)
    wvT = w_in[:, 1024:1536].T.astype(BF16)
    wn = jnp.concatenate([w_in[:, 512:1024], w_in[:, 1536:2048] * scale, w_in[:, 2048:3072]],
                         axis=1).astype(BF16)
    row_major = pl.BlockSpec((1, tm, 512), lambda b, i: (b, i, 0))
    col_major = pl.BlockSpec((1, 512, tm), lambda b, i: (b, 0, i))
    modv = pl.BlockSpec((1, 1, D), lambda b, i: (b, 0, 0))
    rm_shape = jax.ShapeDtypeStruct((B, L, 512), BF16)
    cm_shape = jax.ShapeDtypeStruct((B, 512, L), BF16)
    return pl.pallas_call(
        _inproj_kernel,
        grid=(B, L // tm),
        in_specs=[pl.BlockSpec((1, tm, D), lambda b, i: (b, i, 0)),
                  pl.BlockSpec((1, D), lambda b, i: (0, 0)),
                  modv, modv,
                  pl.BlockSpec((512, D), lambda b, i: (0, 0)),
                  pl.BlockSpec((512, D), lambda b, i: (0, 0)),
                  pl.BlockSpec((D, 2048), lambda b, i: (0, 0))],
        out_specs=[col_major, row_major, col_major, row_major, row_major, row_major],
        out_shape=[cm_shape, rm_shape, cm_shape, rm_shape, rm_shape, rm_shape],
        compiler_params=pltpu.CompilerParams(dimension_semantics=("arbitrary", "arbitrary"),
                                             vmem_limit_bytes=VMEM_LIMIT),
        name="inproj",
    )(x, g_pre.reshape(1, D), sc1.reshape(B, 1, D), sh1.reshape(B, 1, D), wqT, wvT, wn)


def _diff_kernel(lam_init, n_kchunks, coef_ref, lam_ref, qT_ref, k_ref, kaug_ref, vT_ref, g_ref, o_ref,
                 acc_ref, m_ref, l_ref, s0_ref, qd_ref, qm_ref, cq_ref, s_ref, p_ref, a_ref):
    tq = qT_ref.shape[2]
    tk = s0_ref.shape[0]
    h = pl.program_id(1)
    q0 = pl.program_id(2) * tq
    c_f = coef_ref[4 * h]
    c_parts = (coef_ref[4 * h + 1], coef_ref[4 * h + 2], coef_ref[4 * h + 3])
    c_diag = lax.div(q0, tk)

    qT = qT_ref[0]
    row = lax.broadcasted_iota(I32, qT.shape, 0)
    zero = jnp.zeros_like(qT)
    q_maps = (jnp.where(row < HEAD_DIM, qT, zero), jnp.where(row >= HEAD_DIM, qT, zero))
    aug = jnp.zeros(qT.shape, F32)
    for j in range(3):
        aug = jnp.where(row == j, 64.0 * c_parts[j], aug)
        aug = jnp.where(row == 3 + j, c_parts[j], aug)
    for mi in range(2):
        qd_ref[mi] = q_maps[mi]
        for sg, sign in enumerate((1.0, -1.0)):
            qm_ref[mi, sg, 0:DIFF_HEAD_W, :] = q_maps[mi]
            qm_ref[mi, sg, DIFF_HEAD_W:, :] = (sign * aug).astype(BF16)

    qpos = q0 + lax.broadcasted_iota(I32, (1, tq), 1)
    cq_ref[...] = c_f * qpos.astype(F32)
    kk = lax.broadcasted_iota(I32, (tk, tq), 0)
    qq = lax.broadcasted_iota(I32, (tk, tq), 1)
    s0_ref[...] = c_f * (kk - qq).astype(F32)
    acc_ref[...] = jnp.zeros_like(acc_ref)
    l_ref[...] = jnp.zeros_like(l_ref)
    m_ref[...] = jnp.full_like(m_ref, NEG_BIG)

    def chunk(i):
        j = i - 1
        c = j + (j >= c_diag).astype(I32)
        return c_diag if isinstance(i, int) and i == 0 else jnp.where(i == 0, c_diag, c)

    def start(c):
        return pl.multiple_of(c * tk, tk)

    def update(mi, slot, logits, shift):
        m_old = m_ref[mi]
        m_new = jnp.maximum(m_old, jnp.max(logits, axis=0, keepdims=True) - shift)
        alpha = jnp.exp2(m_old - m_new)
        p = jnp.exp2(logits - (m_new + shift))
        l_ref[mi] = alpha * l_ref[mi] + jnp.sum(p, axis=0, keepdims=True)
        m_ref[mi] = m_new
        a_ref[slot, mi] = alpha
        p_ref[slot, mi] = p.astype(BF16)

    def scores_diag(slot):
        kb = k_ref[0, pl.ds(start(c_diag), tk), :]
        for mi in range(2):
            s_ref[slot, mi] = jnp.dot(kb, qd_ref[mi], preferred_element_type=F32)

    def softmax_diag(slot):
        bias = jnp.abs(s0_ref[...] + c_f * (start(c_diag) - q0).astype(F32))
        for mi in range(2):
            update(mi, slot, s_ref[slot, mi] - bias, 0.0)

    def scores(c, slot):
        k0 = start(c)
        kb = jnp.concatenate([k_ref[0, pl.ds(k0, tk), :], kaug_ref[pl.ds(k0, tk), :]], axis=1)
        sg = (c > c_diag).astype(I32)
        for mi in range(2):
            s_ref[slot, mi] = jnp.dot(kb, qm_ref[mi, sg], preferred_element_type=F32)

    def softmax(c, slot):
        shift = jnp.where(c > c_diag, -1.0, 1.0) * cq_ref[...]
        for mi in range(2):
            update(mi, slot, s_ref[slot, mi], shift)

    def values(c, slot):
        vb = vT_ref[0, :, pl.ds(start(c), tk)]
        for mi in range(2):
            acc_ref[mi] = a_ref[slot, mi] * acc_ref[mi] + jnp.dot(vb, p_ref[slot, mi],
                                                                  preferred_element_type=F32)

    n = n_kchunks
    scores_diag(0)
    scores(chunk(1), 1)
    softmax_diag(0)

    def pair(j, carry):
        i = 2 * j + 1
        scores(chunk(i + 1), 0)
        softmax(chunk(i), 1)
        values(chunk(i - 1), 0)
        scores(chunk(i + 2), 1)
        softmax(chunk(i + 1), 0)
        values(chunk(i), 1)
        return carry

    lax.fori_loop(0, (n - 2) // 2, pair, 0)
    softmax(chunk(n - 1), 1)
    values(chunk(n - 2), 0)
    values(chunk(n - 1), 1)

    lam = lam_ref[0:1, 0:1]
    o = acc_ref[0] / l_ref[0] - lam * (acc_ref[1] / l_ref[1])
    y = _rms(o, axis=0) * g_ref[...] * (1.0 - lam_init)
    o_ref[0] = y.T.astype(BF16)


def _diff_attention(qT, k, vT, lam, g_sub, lam_init):
    B, _, L = qT.shape
    tq, tk = min(DIFF_TQ, L), min(DIFF_TK, L)
    assert (L // tk) % 2 == 0 and tk % tq == 0 and L <= 64 * 256
    i = jnp.arange(1, N_DIFF_HEADS + 1, dtype=F32)
    c = jnp.exp2(-8.0 * i / N_DIFF_HEADS) * LOG2E
    c1 = c.astype(BF16).astype(F32)
    c2 = (c - c1).astype(BF16).astype(F32)
    c3 = (c - c1 - c2).astype(BF16).astype(F32)
    coef = jnp.stack([c, c1, c2, c3], axis=1).reshape(-1)
    kpos = jnp.arange(L, dtype=I32)
    lane = jnp.arange(DIFF_HEAD_W, dtype=I32)
    kaug = jnp.where(lane[None, :] < 3, (kpos // 64)[:, None],
                     jnp.where(lane[None, :] < 6, (kpos % 64)[:, None], 0)).astype(BF16)
    return pl.pallas_call(
        functools.partial(_diff_kernel, lam_init, L // tk),
        grid=(B, N_DIFF_HEADS, L // tq),
        in_specs=[pl.BlockSpec(memory_space=pltpu.SMEM),
                  pl.BlockSpec((8, 128), lambda b, h, i: (0, 0)),
                  pl.BlockSpec((1, DIFF_HEAD_W, tq), lambda b, h, i: (b, h, i)),
                  pl.BlockSpec((1, L, DIFF_HEAD_W), lambda b, h, i: (b, 0, h)),
                  pl.BlockSpec((L, DIFF_HEAD_W), lambda b, h, i: (0, 0)),
                  pl.BlockSpec((1, DIFF_HEAD_W, L), lambda b, h, i: (b, h, 0)),
                  pl.BlockSpec((DIFF_HEAD_W, 1), lambda b, h, i: (0, 0))],
        out_specs=pl.BlockSpec((1, tq, DIFF_HEAD_W), lambda b, h, i: (b, i, h)),
        out_shape=jax.ShapeDtypeStruct((B, L, DIFF_WIDTH), BF16),
        scratch_shapes=[pltpu.VMEM((2, DIFF_HEAD_W, tq), F32),
                        pltpu.VMEM((2, 1, tq), F32),
                        pltpu.VMEM((2, 1, tq), F32),
                        pltpu.VMEM((tk, tq), F32),
                        pltpu.VMEM((2, DIFF_HEAD_W, tq), BF16),
                        pltpu.VMEM((2, 2, 2 * DIFF_HEAD_W, tq), BF16),
                        pltpu.VMEM((1, tq), F32),
                        pltpu.VMEM((2, 2, tk, tq), F32),
                        pltpu.VMEM((2, 2, tk, tq), BF16),
                        pltpu.VMEM((2, 2, 1, tq), F32)],
        compiler_params=pltpu.CompilerParams(
            dimension_semantics=("arbitrary", "arbitrary", "arbitrary"),
            vmem_limit_bytes=VMEM_LIMIT),
        name="diff_attn",
    )(coef, lam, qT, k, kaug, vT, g_sub.reshape(DIFF_HEAD_W, 1))


def _nat_bias_table(rpb):
    c = jnp.arange(GRID_W)
    col_start = jnp.clip(c - NAT_KW // 2, 0, GRID_W - NAT_KW)
    col_in = (c[None, :] >= col_start[:, None]) & (c[None, :] < col_start[:, None] + NAT_KW)
    dc = jnp.clip(c[None, :] - c[:, None], -(NAT_KW - 1), NAT_KW - 1) + (NAT_KW - 1)
    rpb = rpb.astype(F32)
    cols = jnp.zeros(rpb.shape[:2] + (GRID_W, GRID_W), F32)
    for j in range(2 * NAT_KW - 1):
        cols = cols + jnp.where(dc == j, rpb[:, :, j][:, :, None, None], 0.0)
    cols = jnp.where(col_in[None, None], cols, NEG_BIG)
    tbl = jnp.stack([cols[:, NAT_KH - 1 - v:2 * NAT_KH - 1 - v] for v in range(NAT_KH)], axis=1)
    tbl = tbl.transpose(0, 1, 3, 2, 4)
    return tbl.reshape(NAT_PAIRS, 2, NAT_KH, GRID_W, NAT_KH * GRID_W)


def _nat_kernel(n_rows, q_ref, k_ref, v_ref, bias_ref, o_ref):
    ri = pl.program_id(2)
    rows_per_step = q_ref.shape[1] // GRID_W
    win = NAT_KH * GRID_W
    lane = lax.broadcasted_iota(I32, (GRID_W, 2 * HEAD_DIM), 1)
    for j in range(rows_per_step):
        r = ri * rows_per_step + j
        rs = jnp.clip(r - NAT_KH // 2, 0, n_rows - NAT_KH)
        var = r - rs
        k0 = pl.multiple_of(rs * GRID_W, GRID_W)
        kw = k_ref[0, pl.ds(k0, win), :]
        vw = v_ref[0, pl.ds(k0, win), :]
        q = q_ref[0, j * GRID_W:(j + 1) * GRID_W, :]
        zero = jnp.zeros_like(q)
        outs = []
        for hh in range(2):
            keep = (lane < HEAD_DIM) if hh == 0 else (lane >= HEAD_DIM)
            s = lax.dot_general(jnp.where(keep, q, zero), kw, NT_DIMS, preferred_element_type=F32)
            s = s + bias_ref[0, hh, var]
            p = jnp.exp(s - jnp.max(s, axis=-1, keepdims=True))
            l = jnp.sum(p, axis=-1, keepdims=True)
            outs.append(jnp.dot(p.astype(BF16), vw, preferred_element_type=F32) / l)
        o_ref[0, j * GRID_W:(j + 1) * GRID_W, :] = jnp.where(lane < HEAD_DIM, outs[0], outs[1]).astype(BF16)


def _nat_attention(nq, nk, nv, rpb):
    B, L, _ = nq.shape
    n_rows = L // GRID_W
    assert n_rows >= NAT_KH
    rps = min(NAT_ROWS_PER_STEP, n_rows)
    tbl = _nat_bias_table(rpb)
    kv = pl.BlockSpec((1, L, 2 * HEAD_DIM), lambda b, p, i: (b, 0, p))
    qo = pl.BlockSpec((1, rps * GRID_W, 2 * HEAD_DIM), lambda b, p, i: (b, i, p))
    return pl.pallas_call(
        functools.partial(_nat_kernel, n_rows),
        grid=(B, NAT_PAIRS, n_rows // rps),
        in_specs=[qo, kv, kv,
                  pl.BlockSpec((1, 2, NAT_KH, GRID_W, NAT_KH * GRID_W), lambda b, p, i: (p, 0, 0, 0, 0))],
        out_specs=qo,
        out_shape=jax.ShapeDtypeStruct((B, L, NAT_WIDTH), BF16),
        compiler_params=pltpu.CompilerParams(
            dimension_semantics=("arbitrary", "arbitrary", "arbitrary"),
            vmem_limit_bytes=VMEM_LIMIT),
        name="nat_attn",
    )(nq, nk, nv, tbl)


def _out_kernel(od_ref, on_ref, x_ref, wt_ref, wb_ref, gpost_ref, gt1_ref, gpre_ref, sc2_ref, sh2_ref,
                wr_ref, br_ref, x1_ref, h2_ref, eidx_ref, gate_ref, rank_ref, cnt_ref, carry_ref):
    tm = x_ref.shape[0]

    @pl.when(pl.program_id(0) == 0)
    def _():
        carry_ref[...] = jnp.zeros_like(carry_ref)

    mix = (jnp.dot(od_ref[...], wt_ref[...], preferred_element_type=F32)
           + jnp.dot(on_ref[...], wb_ref[...], preferred_element_type=F32))
    x1 = x_ref[...] + gt1_ref[0] * (_rms(mix) * gpost_ref[...])
    x1_ref[...] = x1
    h2 = _rms(x1) * gpre_ref[...]
    h2 = h2 * (1.0 + sc2_ref[0]) + sh2_ref[0]
    h2_ref[...] = h2

    logits = jnp.dot(h2, wr_ref[...], preferred_element_type=F32,
                     precision=lax.Precision.HIGHEST) + br_ref[...]
    eio = lax.broadcasted_iota(I32, logits.shape, 1).astype(F32)
    onehot = jnp.zeros_like(logits)
    vals, idxs, sels = [], [], []
    cur = logits
    for _ in range(TOP_K):
        mx = jnp.max(cur, axis=-1, keepdims=True)
        idx = jnp.min(jnp.where(cur == mx, eio, float(N_EXPERTS)), axis=-1, keepdims=True)
        sel = eio == idx
        vals.append(mx)
        idxs.append(idx)
        sels.append(sel)
        cur = jnp.where(sel, -jnp.inf, cur)
        onehot = onehot + sel.astype(F32)

    ex = [jnp.exp(v - vals[0]) for v in vals]
    tot = ex[0] + ex[1] + ex[2] + ex[3]

    rr = lax.broadcasted_iota(I32, (tm, tm), 0)
    cc = lax.broadcasted_iota(I32, (tm, tm), 1)
    tri = (rr > cc).astype(BF16)
    carry = carry_ref[...]
    cum = jnp.dot(tri, onehot.astype(BF16), preferred_element_type=F32) + carry
    ranks = [jnp.sum(jnp.where(sel, cum, 0.0), axis=-1, keepdims=True) for sel in sels]
    carry = carry + jnp.sum(onehot, axis=0, keepdims=True)
    carry_ref[...] = carry
    cnt_ref[...] = carry

    kio = lax.broadcasted_iota(I32, (tm, TOP_K), 1)

    def pack(cols):
        out = jnp.broadcast_to(cols[TOP_K - 1], (tm, TOP_K))
        for k in range(TOP_K - 2, -1, -1):
            out = jnp.where(kio == k, cols[k], out)
        return out

    eidx_ref[...] = pack(idxs).astype(I32)
    gate_ref[...] = pack([e / tot for e in ex])
    rank_ref[...] = pack(ranks).astype(I32)


def _out_router(od, on, x, w_out, g_post, gt1, g_pre, sc2, sh2, w_router, b_router, tokens_per_batch):
    T, D = x.shape
    B = gt1.shape[0]
    tm = min(OUT_TM, tokens_per_batch)
    steps_per_batch = tokens_per_batch // tm
    wt = w_out[:DIFF_WIDTH].astype(BF16)
    wb = w_out[DIFF_WIDTH:].astype(BF16)
    rowblk = lambda w: pl.BlockSpec((tm, w), lambda i: (i, 0))
    const = lambda shape: pl.BlockSpec(shape, lambda i: (0,) * len(shape))
    modv = pl.BlockSpec((1, 1, D), lambda i: (i // steps_per_batch, 0, 0))
    return pl.pallas_call(
        _out_kernel,
        grid=(T // tm,),
        in_specs=[rowblk(DIFF_WIDTH), rowblk(NAT_WIDTH), rowblk(D),
                  const((DIFF_WIDTH, D)), const((NAT_WIDTH, D)), const((1, D)), modv, const((1, D)),
                  modv, modv, const((D, N_EXPERTS)), const((1, N_EXPERTS))],
        out_specs=[rowblk(D), rowblk(D), rowblk(TOP_K), rowblk(TOP_K), rowblk(TOP_K),
                   const((1, N_EXPERTS))],
        out_shape=[jax.ShapeDtypeStruct((T, D), F32), jax.ShapeDtypeStruct((T, D), F32),
                   jax.ShapeDtypeStruct((T, TOP_K), I32), jax.ShapeDtypeStruct((T, TOP_K), F32),
                   jax.ShapeDtypeStruct((T, TOP_K), I32), jax.ShapeDtypeStruct((1, N_EXPERTS), F32)],
        scratch_shapes=[pltpu.VMEM((1, N_EXPERTS), F32)],
        compiler_params=pltpu.CompilerParams(dimension_semantics=("arbitrary",),
                                             vmem_limit_bytes=VMEM_LIMIT),
        name="out_router",
    )(od, on, x, wt, wb, g_post.reshape(1, D), gt1.reshape(B, 1, D), g_pre.reshape(1, D),
      sc2.reshape(B, 1, D), sh2.reshape(B, 1, D), w_router, b_router.reshape(1, N_EXPERTS))


def _expert_kernel(be_ref, tok_ref, tok_next_ref, dst_prev_ref, dst_ref, h2_hbm, w1_ref, b1g_ref, b1l_ref,
                   w2_ref, b2_ref, y_hbm, xbuf, ybuf, xb, w1t, w2t, hh, gsem, ssem):
    i = pl.program_id(0)
    n = pl.num_programs(0)
    bm = xbuf.shape[1]
    ff = w2_ref.shape[1]
    slot = i % 2

    def gather_row(idx_ref, s, r):
        return pltpu.make_async_copy(h2_hbm.at[pl.ds(idx_ref[0, 0, r], 1), :],
                                     xbuf.at[s, pl.ds(r, 1), :], gsem.at[s])

    def scatter_row(idx_ref, s, r):
        return pltpu.make_async_copy(ybuf.at[s, pl.ds(r, 1), :],
                                     y_hbm.at[pl.ds(idx_ref[0, 0, r], 1), :], ssem.at[0])

    def wait_gather(s):
        pltpu.make_async_copy(h2_hbm.at[pl.ds(0, bm), :], xbuf.at[s], gsem.at[s]).wait()

    def wait_scatter(s):
        pltpu.make_async_copy(ybuf.at[s], y_hbm.at[pl.ds(0, bm), :], ssem.at[0]).wait()

    @pl.when(i == 0)
    def _():
        ybuf[1] = jnp.zeros(ybuf.shape[1:], F32)
        for r in range(bm):
            gather_row(tok_ref, 0, r).start(priority=r % 2)

    @pl.when(i > 0)
    def _():
        wait_scatter(slot)

    @pl.when((i == 0) | (be_ref[i] != be_ref[jnp.maximum(i - 1, 0)]))
    def _():
        cw = 256
        for c in range(w1_ref.shape[2] // cw):
            w1t[c * cw:(c + 1) * cw, :] = w1_ref[0, :, c * cw:(c + 1) * cw].T.astype(BF16)
        for c in range(w2_ref.shape[2] // cw):
            w2t[c * cw:(c + 1) * cw, :] = w2_ref[0, :, c * cw:(c + 1) * cw].T.astype(BF16)

    wait_gather(slot)
    xb[...] = xbuf[slot].astype(BF16)

    n_lane_tiles = bm // 128

    @pl.when(i >= 0)
    def _():
        for r in range(bm):
            gather_row(tok_next_ref, 1 - slot, r).start(priority=r % 2)
        hh_t = lax.dot_general(w1t[...], xb[...], NT_DIMS, preferred_element_type=F32)
        for j in range(n_lane_tiles):
            hh[j] = hh_t[:, j * 128:(j + 1) * 128]

    for r in range(bm):
        scatter_row(dst_prev_ref, 1 - slot, r).start(priority=r % 2)
    even = jnp.concatenate([hh[j, pl.ds(0, ff, stride=2), :] for j in range(n_lane_tiles)], axis=1)
    odd = jnp.concatenate([hh[j, pl.ds(1, ff, stride=2), :] for j in range(n_lane_tiles)], axis=1)
    glu = jnp.minimum(even + b1g_ref[0], SWIGLU_LIMIT)
    lin = jnp.clip(odd + b1l_ref[0], -SWIGLU_LIMIT, SWIGLU_LIMIT)
    act = glu * (1.0 / (1.0 + jnp.exp(-SWIGLU_ALPHA * glu))) * (lin + 1.0)
    y_t = jnp.dot(w2t[...], act.astype(BF16), preferred_element_type=F32)

    ybuf[slot] = y_t.T + b2_ref[0]

    @pl.when(i == n - 1)
    def _():
        wait_scatter(1 - slot)
        for r in range(bm):
            scatter_row(dst_ref, slot, r).start(priority=r % 2)
        wait_scatter(slot)
        wait_gather(1 - slot)


def _experts(h2, blk_expert, tok_buf, dst_buf, w1, b1, w2, b2, n_rows_out):
    T, D = h2.shape
    n_blocks = blk_expert.shape[0]
    bm = MOE_BM
    F = w2.shape[1]
    b1g = b1[:, 0::2].reshape(N_EXPERTS, F, 1)
    b1l = b1[:, 1::2].reshape(N_EXPERTS, F, 1)
    tok3 = tok_buf.reshape(n_blocks, 1, bm)
    spare = n_rows_out - bm + jnp.arange(bm, dtype=I32)
    dst3 = jnp.concatenate([spare, dst_buf]).reshape(n_blocks + 1, 1, bm)
    smem_blk = lambda fn: pl.BlockSpec((1, 1, bm), fn, memory_space=pltpu.SMEM)
    grid_spec = pltpu.PrefetchScalarGridSpec(
        num_scalar_prefetch=1,
        grid=(n_blocks,),
        in_specs=[smem_blk(lambda i, be: (i, 0, 0)),
                  smem_blk(lambda i, be: (jnp.minimum(i + 1, n_blocks - 1), 0, 0)),
                  smem_blk(lambda i, be: (i, 0, 0)),
                  smem_blk(lambda i, be: (i + 1, 0, 0)),
                  pl.BlockSpec(memory_space=pl.ANY),
                  pl.BlockSpec((1, D, 2 * F), lambda i, be: (be[i], 0, 0)),
                  pl.BlockSpec((1, F, 1), lambda i, be: (be[i], 0, 0)),
                  pl.BlockSpec((1, F, 1), lambda i, be: (be[i], 0, 0)),
                  pl.BlockSpec((1, F, D), lambda i, be: (be[i], 0, 0)),
                  pl.BlockSpec((1, 1, D), lambda i, be: (be[i], 0, 0))],
        out_specs=pl.BlockSpec(memory_space=pl.ANY),
        scratch_shapes=[pltpu.VMEM((2, bm, D), F32), pltpu.VMEM((2, bm, D), F32),
                        pltpu.VMEM((bm, D), BF16),
                        pltpu.VMEM((2 * F, D), BF16), pltpu.VMEM((D, F), BF16),
                        pltpu.VMEM((bm // 128, 2 * F, 128), F32),
                        pltpu.SemaphoreType.DMA((2,)), pltpu.SemaphoreType.DMA((1,))],
    )
    return pl.pallas_call(
        _expert_kernel,
        grid_spec=grid_spec,
        out_shape=jax.ShapeDtypeStruct((n_rows_out, D), F32),
        compiler_params=pltpu.CompilerParams(dimension_semantics=("arbitrary",),
                                             vmem_limit_bytes=EXPERT_VMEM_LIMIT),
        name="experts",
    )(blk_expert, tok3, tok3, dst3, dst3, h2, w1, b1g, b1l, w2, b2.reshape(N_EXPERTS, 1, D))


def _combine_kernel(y0_ref, y1_ref, y2_ref, y3_ref, gate_ref, x1_ref, gt2_ref, g_ref, o_ref):
    gates = gate_ref[...]
    f = gates[:, 0:1] * y0_ref[...]
    for k, y_ref in enumerate((y1_ref, y2_ref, y3_ref), start=1):
        f = f + gates[:, k:k + 1] * y_ref[...]
    o_ref[...] = x1_ref[...] + gt2_ref[0] * (_rms(f) * g_ref[...])


def _combine(y_tok, gates, x1, gt2, g_post, tokens_per_batch):
    T, D = x1.shape
    B = gt2.shape[0]
    tm = min(COMBINE_TM, tokens_per_batch)
    steps_per_batch = tokens_per_batch // tm
    steps = T // tm
    y_spec = lambda k: pl.BlockSpec((tm, D), lambda i: (k * steps + i, 0))
    return pl.pallas_call(
        _combine_kernel,
        grid=(steps,),
        in_specs=[y_spec(0), y_spec(1), y_spec(2), y_spec(3),
                  pl.BlockSpec((tm, TOP_K), lambda i: (i, 0)),
                  pl.BlockSpec((tm, D), lambda i: (i, 0)),
                  pl.BlockSpec((1, 1, D), lambda i: (i // steps_per_batch, 0, 0)),
                  pl.BlockSpec((1, D), lambda i: (0, 0))],
        out_specs=pl.BlockSpec((tm, D), lambda i: (i, 0)),
        out_shape=jax.ShapeDtypeStruct((T, D), F32),
        compiler_params=pltpu.CompilerParams(dimension_semantics=("arbitrary",),
                                             vmem_limit_bytes=VMEM_LIMIT),
        name="combine",
    )(y_tok, y_tok, y_tok, y_tok, gates, x1, gt2.reshape(B, 1, D), g_post.reshape(1, D))


def _dispatch_plan(eidx, rank, counts):
    T = eidx.shape[0]
    n_assign = eidx.size
    bm = MOE_BM
    cap = (n_assign + N_EXPERTS * (bm - 1) + bm - 1) // bm * bm
    n_blocks = cap // bm
    counts = counts.reshape(N_EXPERTS).astype(I32)
    padded = (counts + bm - 1) // bm * bm
    pad_end = jnp.cumsum(padded)
    pad_start = pad_end - padded
    eio = jnp.arange(N_EXPERTS, dtype=I32)
    start_of = jnp.sum(jnp.where(eidx[..., None] == eio, pad_start, 0), axis=-1)
    dest = (start_of + rank).reshape(-1)
    blk_start = jnp.arange(n_blocks, dtype=I32) * bm
    blk_expert = jnp.minimum(jnp.sum((blk_start[:, None] >= pad_end[None, :]).astype(I32), axis=-1),
                             N_EXPERTS - 1)
    inv = jnp.zeros((cap,), I32).at[dest].set(jnp.arange(1, n_assign + 1, dtype=I32))
    is_pad = inv == 0
    a = inv - 1
    tok_buf = jnp.where(is_pad, 0, a // TOP_K).astype(I32)
    pad_row = n_assign + jnp.cumsum(is_pad.astype(I32)) - 1
    dst_buf = jnp.where(is_pad, pad_row, (a % TOP_K) * T + a // TOP_K).astype(I32)
    return blk_expert, tok_buf, dst_buf, cap + bm


def _layer(x, c, l, w_ada, b_ada, g_pre_mix, g_post_mix, w_in, w_out, lam_q1, lam_k1, lam_q2, lam_k2,
           g_subln, nat_rpb, g_pre_ffn, g_post_ffn, w_router, b_router, w1, b1, w2, b2):
    B, L, D = x.shape
    lam_init = 0.8 - 0.6 * math.exp(-0.3 * l)
    mod, lam = _ada(c, w_ada, b_ada, lam_q1, lam_k1, lam_q2, lam_k2, lam_init)
    sh1, sc1, gt1, sh2, sc2, gt2 = jnp.split(mod, 6, axis=-1)

    qT, kd, vT, nq, nk, nv = _inproj(x, g_pre_mix, sc1, sh1, w_in)
    o_diff = _diff_attention(qT, kd, vT, lam, g_subln, lam_init)
    o_nat = _nat_attention(nq, nk, nv, nat_rpb)

    T = B * L
    x1, h2, eidx, gates, rank, counts = _out_router(
        o_diff.reshape(T, DIFF_WIDTH), o_nat.reshape(T, NAT_WIDTH), x.reshape(T, D), w_out,
        g_post_mix, gt1, g_pre_ffn, sc2, sh2, w_router, b_router, L)
    blk_expert, tok_buf, dst_buf, n_rows_out = _dispatch_plan(eidx, rank, counts)
    y_tok = _experts(h2, blk_expert, tok_buf, dst_buf, w1, b1, w2, b2, n_rows_out)
    out = _combine(y_tok, gates, x1, gt2, g_post_ffn, L)
    return out.reshape(B, L, D)


def kernel(x, c, w_ada, b_ada, g_pre_mix, g_post_mix, w_in, w_out, lam_q1, lam_k1, lam_q2, lam_k2,
           g_subln, nat_rpb, g_pre_ffn, g_post_ffn, w_router, b_router, w1, b1, w2, b2):
    depth = w_ada.shape[0]
    for l in range(depth):
        x = _layer(x, c, l, w_ada[l], b_ada[l], g_pre_mix[l], g_post_mix[l], w_in[l], w_out[l],
                   lam_q1[l], lam_k1[l], lam_q2[l], lam_k2[l], g_subln[l], nat_rpb[l],
                   g_pre_ffn[l], g_post_ffn[l], w_router[l], b_router[l], w1[l], b1[l], w2[l], b2[l])
    return x
```

```python
import functools
import math

import jax
import jax.numpy as jnp
from jax import lax
from jax.experimental import pallas as pl
from jax.experimental.pallas import tpu as pltpu

F32 = jnp.float32
BF16 = jnp.bfloat16
I32 = jnp.int32

HEAD_DIM = 64
N_DIFF_HEADS = 4
DIFF_HEAD_W = 2 * HEAD_DIM
DIFF_WIDTH = N_DIFF_HEADS * DIFF_HEAD_W
N_NAT_HEADS = 8
NAT_WIDTH = N_NAT_HEADS * HEAD_DIM
NAT_PAIRS = N_NAT_HEADS // 2
GRID_W = 64
NAT_KH = 8
NAT_KW = 16
N_EXPERTS = 32
TOP_K = 4
SWIGLU_LIMIT = 7.0
SWIGLU_ALPHA = 1.702
RMS_EPS = 1e-6
NEG_BIG = -1e30
LOG2E = 1.4426950408889634
ROW_TILE = 8

NT_DIMS = (((1,), (1,)), ((), ()))

ADA_TN = 1536
INPROJ_TM = 512
DIFF_TQ = 256
DIFF_TK = 512
NAT_ROWS_PER_STEP = 8
OUT_TM = 512
MOE_BM = 256
COMBINE_TM = 512
VMEM_LIMIT = 48 * 1024 * 1024
EXPERT_VMEM_LIMIT = 56 * 1024 * 1024


def _rms(x, axis=-1):
    return x * lax.rsqrt(jnp.mean(x * x, axis=axis, keepdims=True) + RMS_EPS)


def _ada_kernel(lam_init, c_ref, w_ref, b_ref, lq1_ref, lk1_ref, lq2_ref, lk2_ref, mod_ref, lam_ref):
    c = c_ref[...]
    s = c * (1.0 / (1.0 + jnp.exp(-c)))
    mod_ref[...] = jnp.dot(s, w_ref[...], preferred_element_type=F32,
                           precision=lax.Precision.HIGHEST) + b_ref[...]
    d1 = jnp.sum(lq1_ref[...] * lk1_ref[...], axis=-1, keepdims=True)
    d2 = jnp.sum(lq2_ref[...] * lk2_ref[...], axis=-1, keepdims=True)
    lam = jnp.exp(d1) - jnp.exp(d2) + lam_init
    lam_ref[...] = jnp.broadcast_to(lam, lam_ref.shape)


def _ada(c, w_ada, b_ada, lq1, lk1, lq2, lk2, lam_init):
    B, D = c.shape
    N = w_ada.shape[1]
    c8 = jnp.zeros((8, D), F32).at[:B].set(c)
    vec = pl.BlockSpec((1, HEAD_DIM), lambda j: (0, 0))
    mod, lam = pl.pallas_call(
        functools.partial(_ada_kernel, lam_init),
        grid=(N // ADA_TN,),
        in_specs=[pl.BlockSpec((8, D), lambda j: (0, 0)),
                  pl.BlockSpec((D, ADA_TN), lambda j: (0, j)),
                  pl.BlockSpec((1, ADA_TN), lambda j: (0, j)),
                  vec, vec, vec, vec],
        out_specs=[pl.BlockSpec((8, ADA_TN), lambda j: (0, j)),
                   pl.BlockSpec((8, 128), lambda j: (0, 0))],
        out_shape=[jax.ShapeDtypeStruct((8, N), F32), jax.ShapeDtypeStruct((8, 128), F32)],
        compiler_params=pltpu.CompilerParams(dimension_semantics=("arbitrary",),
                                             vmem_limit_bytes=VMEM_LIMIT),
        name="ada",
    )(c8, w_ada, b_ada.reshape(1, N), lq1.reshape(1, -1), lk1.reshape(1, -1),
      lq2.reshape(1, -1), lk2.reshape(1, -1))
    return mod[:B], lam


def _inproj_kernel(x_ref, g_ref, sc_ref, sh_ref, wqT_ref, wvT_ref, wn_ref,
                   qT_ref, k_ref, vT_ref, nq_ref, nk_ref, nv_ref):
    h = _rms(x_ref[0]) * g_ref[...]
    h = h * (1.0 + sc_ref[0]) + sh_ref[0]
    hb = h.astype(BF16)
    qT_ref[0] = lax.dot_general(wqT_ref[...], hb, NT_DIMS, preferred_element_type=F32).astype(BF16)
    vT_ref[0] = lax.dot_general(wvT_ref[...], hb, NT_DIMS, preferred_element_type=F32).astype(BF16)
    rest = jnp.dot(hb, wn_ref[...], preferred_element_type=F32).astype(BF16)
    k_ref[0] = rest[:, 0:512]
    nq_ref[0] = rest[:, 512:1024]
    nk_ref[0] = rest[:, 1024:1536]
    nv_ref[0] = rest[:, 1536:2048]


def _inproj(x, g_pre, sc1, sh1, w_in):
    B, L, D = x.shape
    tm = INPROJ_TM
    scale = HEAD_DIM ** -0.5
    wqT = (w_in[:, 0:512] * (scale * LOG2E)).T.astype(BF16)
    wvT = w_in[:, 1024:1536].T.astype(BF16)
    wn = jnp.concatenate([w_in[:, 512:1024], w_in[:, 1536:2048] * scale, w_in[:, 2048:3072]],
                         axis=1).astype(BF16)
    row_major = pl.BlockSpec((1, tm, 512), lambda b, i: (b, i, 0))
    col_major = pl.BlockSpec((1, 512, tm), lambda b, i: (b, 0, i))
    modv = pl.BlockSpec((1, 1, D), lambda b, i: (b, 0, 0))
    rm_shape = jax.ShapeDtypeStruct((B, L, 512), BF16)
    cm_shape = jax.ShapeDtypeStruct((B, 512, L), BF16)
    return pl.pallas_call(
        _inproj_kernel,
        grid=(B, L // tm),
        in_specs=[pl.BlockSpec((1, tm, D), lambda b, i: (b, i, 0)),
                  pl.BlockSpec((1, D), lambda b, i: (0, 0)),
                  modv, modv,
                  pl.BlockSpec((512, D), lambda b, i: (0, 0)),
                  pl.BlockSpec((512, D), lambda b, i: (0, 0)),
                  pl.BlockSpec((D, 2048), lambda b, i: (0, 0))],
        out_specs=[col_major, row_major, col_major, row_major, row_major, row_major],
        out_shape=[cm_shape, rm_shape, cm_shape, rm_shape, rm_shape, rm_shape],
        compiler_params=pltpu.CompilerParams(dimension_semantics=("arbitrary", "arbitrary"),
                                             vmem_limit_bytes=VMEM_LIMIT),
        name="inproj",
    )(x, g_pre.reshape(1, D), sc1.reshape(B, 1, D), sh1.reshape(B, 1, D), wqT, wvT, wn)


def _diff_kernel(lam_init, n_kchunks, coef_ref, lam_ref, qT_ref, k_ref, kaug_ref, vT_ref, g_ref, o_ref,
                 acc_ref, m_ref, l_ref, s0_ref, qd_ref, qm_ref, cq_ref, s_ref, p_ref, a_ref):
    tq = qT_ref.shape[2]
    tk = s0_ref.shape[0]
    h = pl.program_id(1)
    q0 = pl.program_id(2) * tq
    c_f = coef_ref[4 * h]
    c_parts = (coef_ref[4 * h + 1], coef_ref[4 * h + 2], coef_ref[4 * h + 3])
    c_diag = lax.div(q0, tk)

    qT = qT_ref[0]
    row = lax.broadcasted_iota(I32, qT.shape, 0)
    zero = jnp.zeros_like(qT)
    q_maps = (jnp.where(row < HEAD_DIM, qT, zero), jnp.where(row >= HEAD_DIM, qT, zero))
    aug = jnp.zeros(qT.shape, F32)
    for j in range(3):
        aug = jnp.where(row == j, 64.0 * c_parts[j], aug)
        aug = jnp.where(row == 3 + j, c_parts[j], aug)
    for mi in range(2):
        qd_ref[mi] = q_maps[mi]
        for sg, sign in enumerate((1.0, -1.0)):
            qm_ref[mi, sg, 0:DIFF_HEAD_W, :] = q_maps[mi]
            qm_ref[mi, sg, DIFF_HEAD_W:, :] = (sign * aug).astype(BF16)

    qpos = q0 + lax.broadcasted_iota(I32, (1, tq), 1)
    cq_ref[...] = c_f * qpos.astype(F32)
    kk = lax.broadcasted_iota(I32, (tk, tq), 0)
    qq = lax.broadcasted_iota(I32, (tk, tq), 1)
    s0_ref[...] = c_f * (kk - qq).astype(F32)
    acc_ref[...] = jnp.zeros_like(acc_ref)
    l_ref[...] = jnp.zeros_like(l_ref)
    m_ref[...] = jnp.full_like(m_ref, NEG_BIG)

    def chunk(i):
        j = i - 1
        c = j + (j >= c_diag).astype(I32)
        return c_diag if isinstance(i, int) and i == 0 else jnp.where(i == 0, c_diag, c)

    def start(c):
        return pl.multiple_of(c * tk, tk)

    def update(mi, slot, logits, shift):
        m_old = m_ref[mi]
        m_new = jnp.maximum(m_old, jnp.max(logits, axis=0, keepdims=True) - shift)
        alpha = jnp.exp2(m_old - m_new)
        p = jnp.exp2(logits - (m_new + shift))
        l_ref[mi] = alpha * l_ref[mi] + jnp.sum(p, axis=0, keepdims=True)
        m_ref[mi] = m_new
        a_ref[slot, mi] = alpha
        p_ref[slot, mi] = p.astype(BF16)

    def scores_diag(slot):
        kb = k_ref[0, pl.ds(start(c_diag), tk), :]
        for mi in range(2):
            s_ref[slot, mi] = jnp.dot(kb, qd_ref[mi], preferred_element_type=F32)

    def softmax_diag(slot):
        bias = jnp.abs(s0_ref[...] + c_f * (start(c_diag) - q0).astype(F32))
        for mi in range(2):
            update(mi, slot, s_ref[slot, mi] - bias, 0.0)

    def scores(c, slot):
        k0 = start(c)
        kb = jnp.concatenate([k_ref[0, pl.ds(k0, tk), :], kaug_ref[pl.ds(k0, tk), :]], axis=1)
        sg = (c > c_diag).astype(I32)
        for mi in range(2):
            s_ref[slot, mi] = jnp.dot(kb, qm_ref[mi, sg], preferred_element_type=F32)

    def softmax(c, slot):
        shift = jnp.where(c > c_diag, -1.0, 1.0) * cq_ref[...]
        for mi in range(2):
            update(mi, slot, s_ref[slot, mi], shift)

    def values(c, slot):
        vb = vT_ref[0, :, pl.ds(start(c), tk)]
        for mi in range(2):
            acc_ref[mi] = a_ref[slot, mi] * acc_ref[mi] + jnp.dot(vb, p_ref[slot, mi],
                                                                  preferred_element_type=F32)

    n = n_kchunks
    scores_diag(0)
    scores(chunk(1), 1)
    scores(chunk(2), 2)
    softmax_diag(0)
    scores(chunk(3), 0)
    softmax(chunk(1), 1)

    def three_ticks(j, carry):
        t0 = 3 * j + 2
        for k in range(3):
            t = t0 + k
            scores(chunk(t + 2), (k + 1) % 3)
            softmax(chunk(t), (k + 2) % 3)
            values(chunk(t - 2), k)
        return carry

    lax.fori_loop(0, (n - 4) // 3, three_ticks, 0)
    for t in (n - 2, n - 1):
        softmax(chunk(t), t % 3)
        values(chunk(t - 2), (t - 2) % 3)
    values(chunk(n - 2), (n - 2) % 3)
    values(chunk(n - 1), (n - 1) % 3)

    lam = lam_ref[0:1, 0:1]
    o = acc_ref[0] / l_ref[0] - lam * (acc_ref[1] / l_ref[1])
    y = _rms(o, axis=0) * g_ref[...] * (1.0 - lam_init)
    o_ref[0] = y.T.astype(BF16)


def _diff_attention(qT, k, vT, lam, g_sub, lam_init):
    B, _, L = qT.shape
    tq, tk = min(DIFF_TQ, L), min(DIFF_TK, L)
    assert L // tk >= 4 and (L // tk - 4) % 3 == 0 and tk % tq == 0 and L <= 64 * 256
    i = jnp.arange(1, N_DIFF_HEADS + 1, dtype=F32)
    c = jnp.exp2(-8.0 * i / N_DIFF_HEADS) * LOG2E
    c1 = c.astype(BF16).astype(F32)
    c2 = (c - c1).astype(BF16).astype(F32)
    c3 = (c - c1 - c2).astype(BF16).astype(F32)
    coef = jnp.stack([c, c1, c2, c3], axis=1).reshape(-1)
    kpos = jnp.arange(L, dtype=I32)
    lane = jnp.arange(DIFF_HEAD_W, dtype=I32)
    kaug = jnp.where(lane[None, :] < 3, (kpos // 64)[:, None],
                     jnp.where(lane[None, :] < 6, (kpos % 64)[:, None], 0)).astype(BF16)
    return pl.pallas_call(
        functools.partial(_diff_kernel, lam_init, L // tk),
        grid=(B, N_DIFF_HEADS, L // tq),
        in_specs=[pl.BlockSpec(memory_space=pltpu.SMEM),
                  pl.BlockSpec((8, 128), lambda b, h, i: (0, 0)),
                  pl.BlockSpec((1, DIFF_HEAD_W, tq), lambda b, h, i: (b, h, i)),
                  pl.BlockSpec((1, L, DIFF_HEAD_W), lambda b, h, i: (b, 0, h)),
                  pl.BlockSpec((L, DIFF_HEAD_W), lambda b, h, i: (0, 0)),
                  pl.BlockSpec((1, DIFF_HEAD_W, L), lambda b, h, i: (b, h, 0)),
                  pl.BlockSpec((DIFF_HEAD_W, 1), lambda b, h, i: (0, 0))],
        out_specs=pl.BlockSpec((1, tq, DIFF_HEAD_W), lambda b, h, i: (b, i, h)),
        out_shape=jax.ShapeDtypeStruct((B, L, DIFF_WIDTH), BF16),
        scratch_shapes=[pltpu.VMEM((2, DIFF_HEAD_W, tq), F32),
                        pltpu.VMEM((2, 1, tq), F32),
                        pltpu.VMEM((2, 1, tq), F32),
                        pltpu.VMEM((tk, tq), F32),
                        pltpu.VMEM((2, DIFF_HEAD_W, tq), BF16),
                        pltpu.VMEM((2, 2, 2 * DIFF_HEAD_W, tq), BF16),
                        pltpu.VMEM((1, tq), F32),
                        pltpu.VMEM((3, 2, tk, tq), F32),
                        pltpu.VMEM((3, 2, tk, tq), BF16),
                        pltpu.VMEM((3, 2, 1, tq), F32)],
        compiler_params=pltpu.CompilerParams(
            dimension_semantics=("arbitrary", "arbitrary", "arbitrary"),
            vmem_limit_bytes=VMEM_LIMIT),
        name="diff_attn",
    )(coef, lam, qT, k, kaug, vT, g_sub.reshape(DIFF_HEAD_W, 1))


def _nat_bias_table(rpb):
    c = jnp.arange(GRID_W)
    col_start = jnp.clip(c - NAT_KW // 2, 0, GRID_W - NAT_KW)
    col_in = (c[None, :] >= col_start[:, None]) & (c[None, :] < col_start[:, None] + NAT_KW)
    dc = jnp.clip(c[None, :] - c[:, None], -(NAT_KW - 1), NAT_KW - 1) + (NAT_KW - 1)
    rpb = rpb.astype(F32)
    cols = jnp.zeros(rpb.shape[:2] + (GRID_W, GRID_W), F32)
    for j in range(2 * NAT_KW - 1):
        cols = cols + jnp.where(dc == j, rpb[:, :, j][:, :, None, None], 0.0)
    cols = jnp.where(col_in[None, None], cols, NEG_BIG)
    tbl = jnp.stack([cols[:, NAT_KH - 1 - v:2 * NAT_KH - 1 - v] for v in range(NAT_KH)], axis=1)
    tbl = tbl.transpose(0, 1, 3, 2, 4)
    return tbl.reshape(NAT_PAIRS, 2, NAT_KH, GRID_W, NAT_KH * GRID_W)


def _nat_kernel(n_rows, q_ref, k_ref, v_ref, bias_ref, o_ref):
    ri = pl.program_id(2)
    rows_per_step = q_ref.shape[1] // GRID_W
    win = NAT_KH * GRID_W
    lane = lax.broadcasted_iota(I32, (GRID_W, 2 * HEAD_DIM), 1)
    for j in range(rows_per_step):
        r = ri * rows_per_step + j
        rs = jnp.clip(r - NAT_KH // 2, 0, n_rows - NAT_KH)
        var = r - rs
        k0 = pl.multiple_of(rs * GRID_W, GRID_W)
        kw = k_ref[0, pl.ds(k0, win), :]
        vw = v_ref[0, pl.ds(k0, win), :]
        q = q_ref[0, j * GRID_W:(j + 1) * GRID_W, :]
        zero = jnp.zeros_like(q)
        outs = []
        for hh in range(2):
            keep = (lane < HEAD_DIM) if hh == 0 else (lane >= HEAD_DIM)
            s = lax.dot_general(jnp.where(keep, q, zero), kw, NT_DIMS, preferred_element_type=F32)
            s = s + bias_ref[0, hh, var]
            p = jnp.exp(s - jnp.max(s, axis=-1, keepdims=True))
            l = jnp.sum(p, axis=-1, keepdims=True)
            outs.append(jnp.dot(p.astype(BF16), vw, preferred_element_type=F32) / l)
        o_ref[0, j * GRID_W:(j + 1) * GRID_W, :] = jnp.where(lane < HEAD_DIM, outs[0], outs[1]).astype(BF16)


def _nat_attention(nq, nk, nv, rpb):
    B, L, _ = nq.shape
    n_rows = L // GRID_W
    assert n_rows >= NAT_KH
    rps = min(NAT_ROWS_PER_STEP, n_rows)
    tbl = _nat_bias_table(rpb)
    kv = pl.BlockSpec((1, L, 2 * HEAD_DIM), lambda b, p, i: (b, 0, p))
    qo = pl.BlockSpec((1, rps * GRID_W, 2 * HEAD_DIM), lambda b, p, i: (b, i, p))
    return pl.pallas_call(
        functools.partial(_nat_kernel, n_rows),
        grid=(B, NAT_PAIRS, n_rows // rps),
        in_specs=[qo, kv, kv,
                  pl.BlockSpec((1, 2, NAT_KH, GRID_W, NAT_KH * GRID_W), lambda b, p, i: (p, 0, 0, 0, 0))],
        out_specs=qo,
        out_shape=jax.ShapeDtypeStruct((B, L, NAT_WIDTH), BF16),
        compiler_params=pltpu.CompilerParams(
            dimension_semantics=("arbitrary", "arbitrary", "arbitrary"),
            vmem_limit_bytes=VMEM_LIMIT),
        name="nat_attn",
    )(nq, nk, nv, tbl)


def _out_kernel(od_ref, on_ref, x_ref, wt_ref, wb_ref, gpost_ref, gt1_ref, gpre_ref, sc2_ref, sh2_ref,
                wr_ref, br_ref, x1_ref, h2_ref, eidx_ref, gate_ref, rank_ref, cnt_ref, carry_ref):
    tm = x_ref.shape[0]

    @pl.when(pl.program_id(0) == 0)
    def _():
        carry_ref[...] = jnp.zeros_like(carry_ref)

    mix = (jnp.dot(od_ref[...], wt_ref[...], preferred_element_type=F32)
           + jnp.dot(on_ref[...], wb_ref[...], preferred_element_type=F32))
    x1 = x_ref[...] + gt1_ref[0] * (_rms(mix) * gpost_ref[...])
    x1_ref[...] = x1
    h2 = _rms(x1) * gpre_ref[...]
    h2 = h2 * (1.0 + sc2_ref[0]) + sh2_ref[0]
    for s in range(ROW_TILE):
        h2_ref[pl.ds(s, tm, stride=ROW_TILE), :] = h2[:, s * 128:(s + 1) * 128]

    logits = jnp.dot(h2, wr_ref[...], preferred_element_type=F32,
                     precision=lax.Precision.HIGHEST) + br_ref[...]
    eio = lax.broadcasted_iota(I32, logits.shape, 1).astype(F32)
    onehot = jnp.zeros_like(logits)
    vals, idxs, sels = [], [], []
    cur = logits
    for _ in range(TOP_K):
        mx = jnp.max(cur, axis=-1, keepdims=True)
        idx = jnp.min(jnp.where(cur == mx, eio, float(N_EXPERTS)), axis=-1, keepdims=True)
        sel = eio == idx
        vals.append(mx)
        idxs.append(idx)
        sels.append(sel)
        cur = jnp.where(sel, -jnp.inf, cur)
        onehot = onehot + sel.astype(F32)

    ex = [jnp.exp(v - vals[0]) for v in vals]
    tot = ex[0] + ex[1] + ex[2] + ex[3]

    rr = lax.broadcasted_iota(I32, (tm, tm), 0)
    cc = lax.broadcasted_iota(I32, (tm, tm), 1)
    tri = (rr > cc).astype(BF16)
    carry = carry_ref[...]
    cum = jnp.dot(tri, onehot.astype(BF16), preferred_element_type=F32) + carry
    ranks = [jnp.sum(jnp.where(sel, cum, 0.0), axis=-1, keepdims=True) for sel in sels]
    carry = carry + jnp.sum(onehot, axis=0, keepdims=True)
    carry_ref[...] = carry
    cnt_ref[...] = carry

    kio = lax.broadcasted_iota(I32, (tm, TOP_K), 1)

    def pack(cols):
        out = jnp.broadcast_to(cols[TOP_K - 1], (tm, TOP_K))
        for k in range(TOP_K - 2, -1, -1):
            out = jnp.where(kio == k, cols[k], out)
        return out

    eidx_ref[...] = pack(idxs).astype(I32)
    gate_ref[...] = pack([e / tot for e in ex])
    rank_ref[...] = pack(ranks).astype(I32)


def _out_router(od, on, x, w_out, g_post, gt1, g_pre, sc2, sh2, w_router, b_router, tokens_per_batch):
    T, D = x.shape
    B = gt1.shape[0]
    tm = min(OUT_TM, tokens_per_batch)
    steps_per_batch = tokens_per_batch // tm
    wt = w_out[:DIFF_WIDTH].astype(BF16)
    wb = w_out[DIFF_WIDTH:].astype(BF16)
    rowblk = lambda w: pl.BlockSpec((tm, w), lambda i: (i, 0))
    const = lambda shape: pl.BlockSpec(shape, lambda i: (0,) * len(shape))
    modv = pl.BlockSpec((1, 1, D), lambda i: (i // steps_per_batch, 0, 0))
    return pl.pallas_call(
        _out_kernel,
        grid=(T // tm,),
        in_specs=[rowblk(DIFF_WIDTH), rowblk(NAT_WIDTH), rowblk(D),
                  const((DIFF_WIDTH, D)), const((NAT_WIDTH, D)), const((1, D)), modv, const((1, D)),
                  modv, modv, const((D, N_EXPERTS)), const((1, N_EXPERTS))],
        out_specs=[rowblk(D), pl.BlockSpec((tm * ROW_TILE, 128), lambda i: (i, 0)),
                   rowblk(TOP_K), rowblk(TOP_K), rowblk(TOP_K), const((1, N_EXPERTS))],
        out_shape=[jax.ShapeDtypeStruct((T, D), F32), jax.ShapeDtypeStruct((T * ROW_TILE, 128), F32),
                   jax.ShapeDtypeStruct((T, TOP_K), I32), jax.ShapeDtypeStruct((T, TOP_K), F32),
                   jax.ShapeDtypeStruct((T, TOP_K), I32), jax.ShapeDtypeStruct((1, N_EXPERTS), F32)],
        scratch_shapes=[pltpu.VMEM((1, N_EXPERTS), F32)],
        compiler_params=pltpu.CompilerParams(dimension_semantics=("arbitrary",),
                                             vmem_limit_bytes=VMEM_LIMIT),
        name="out_router",
    )(od, on, x, wt, wb, g_post.reshape(1, D), gt1.reshape(B, 1, D), g_pre.reshape(1, D),
      sc2.reshape(B, 1, D), sh2.reshape(B, 1, D), w_router, b_router.reshape(1, N_EXPERTS))


def _expert_kernel(be_ref, tok_ref, tok_next_ref, dst_prev_ref, dst_ref, h2_hbm, w1_ref, b1g_ref, b1l_ref,
                   w2_ref, b2_ref, y_hbm, xbuf, ybuf, xb, w1t, w2t, hh, gsem, ssem):
    i = pl.program_id(0)
    n = pl.num_programs(0)
    bm = xbuf.shape[1] // ROW_TILE

    def tile_rows(t):
        t = t * ROW_TILE
        return pl.ds(t if isinstance(t, int) else pl.multiple_of(t, ROW_TILE), ROW_TILE)
    ff = w2_ref.shape[1]
    slot = i % 2

    def gather_row(idx_ref, s, r):
        return pltpu.make_async_copy(h2_hbm.at[tile_rows(idx_ref[0, 0, r]), :],
                                     xbuf.at[s, tile_rows(r), :], gsem.at[s])

    def scatter_row(idx_ref, s, r):
        return pltpu.make_async_copy(ybuf.at[s, tile_rows(r), :],
                                     y_hbm.at[tile_rows(idx_ref[0, 0, r]), :], ssem.at[0])

    def wait_gather(s):
        pltpu.make_async_copy(h2_hbm.at[pl.ds(0, bm * ROW_TILE), :], xbuf.at[s], gsem.at[s]).wait()

    def wait_scatter(s):
        pltpu.make_async_copy(ybuf.at[s], y_hbm.at[pl.ds(0, bm * ROW_TILE), :], ssem.at[0]).wait()

    @pl.when(i == 0)
    def _():
        ybuf[1] = jnp.zeros(ybuf.shape[1:], F32)
        for r in range(bm):
            gather_row(tok_ref, 0, r).start(priority=r % 2)

    @pl.when(i > 0)
    def _():
        wait_scatter(slot)

    @pl.when((i == 0) | (be_ref[i] != be_ref[jnp.maximum(i - 1, 0)]))
    def _():
        cw = 256
        for c in range(w1_ref.shape[2] // cw):
            w1t[c * cw:(c + 1) * cw, :] = w1_ref[0, :, c * cw:(c + 1) * cw].T.astype(BF16)
        for c in range(w2_ref.shape[2] // cw):
            w2t[c * cw:(c + 1) * cw, :] = w2_ref[0, :, c * cw:(c + 1) * cw].T.astype(BF16)

    wait_gather(slot)
    for s8 in range(ROW_TILE):
        xb[:, s8 * 128:(s8 + 1) * 128] = xbuf[slot, pl.ds(s8, bm, stride=ROW_TILE), :].astype(BF16)

    n_lane_tiles = bm // 128

    @pl.when(i >= 0)
    def _():
        for r in range(bm):
            gather_row(tok_next_ref, 1 - slot, r).start(priority=r % 2)
        hh_t = lax.dot_general(w1t[...], xb[...], NT_DIMS, preferred_element_type=F32)
        for j in range(n_lane_tiles):
            hh[j] = hh_t[:, j * 128:(j + 1) * 128]

    for r in range(bm):
        scatter_row(dst_prev_ref, 1 - slot, r).start(priority=r % 2)
    even = jnp.concatenate([hh[j, pl.ds(0, ff, stride=2), :] for j in range(n_lane_tiles)], axis=1)
    odd = jnp.concatenate([hh[j, pl.ds(1, ff, stride=2), :] for j in range(n_lane_tiles)], axis=1)
    glu = jnp.minimum(even + b1g_ref[0], SWIGLU_LIMIT)
    lin = jnp.clip(odd + b1l_ref[0], -SWIGLU_LIMIT, SWIGLU_LIMIT)
    act = glu * (1.0 / (1.0 + jnp.exp(-SWIGLU_ALPHA * glu))) * (lin + 1.0)
    y_t = jnp.dot(w2t[...], act.astype(BF16), preferred_element_type=F32)

    y = y_t.T + b2_ref[0]
    for s8 in range(ROW_TILE):
        ybuf[slot, pl.ds(s8, bm, stride=ROW_TILE), :] = y[:, s8 * 128:(s8 + 1) * 128]

    @pl.when(i == n - 1)
    def _():
        wait_scatter(1 - slot)
        for r in range(bm):
            scatter_row(dst_ref, slot, r).start(priority=r % 2)
        wait_scatter(slot)
        wait_gather(1 - slot)


def _experts(h2, blk_expert, tok_buf, dst_buf, w1, b1, w2, b2, n_rows_out):
    D = w1.shape[1]
    assert D == ROW_TILE * 128 and h2.shape[1] == 128
    n_blocks = blk_expert.shape[0]
    bm = MOE_BM
    F = w2.shape[1]
    b1g = b1[:, 0::2].reshape(N_EXPERTS, F, 1)
    b1l = b1[:, 1::2].reshape(N_EXPERTS, F, 1)
    tok3 = tok_buf.reshape(n_blocks, 1, bm)
    spare = n_rows_out - bm + jnp.arange(bm, dtype=I32)
    dst3 = jnp.concatenate([spare, dst_buf]).reshape(n_blocks + 1, 1, bm)
    smem_blk = lambda fn: pl.BlockSpec((1, 1, bm), fn, memory_space=pltpu.SMEM)
    grid_spec = pltpu.PrefetchScalarGridSpec(
        num_scalar_prefetch=1,
        grid=(n_blocks,),
        in_specs=[smem_blk(lambda i, be: (i, 0, 0)),
                  smem_blk(lambda i, be: (jnp.minimum(i + 1, n_blocks - 1), 0, 0)),
                  smem_blk(lambda i, be: (i, 0, 0)),
                  smem_blk(lambda i, be: (i + 1, 0, 0)),
                  pl.BlockSpec(memory_space=pl.ANY),
                  pl.BlockSpec((1, D, 2 * F), lambda i, be: (be[i], 0, 0)),
                  pl.BlockSpec((1, F, 1), lambda i, be: (be[i], 0, 0)),
                  pl.BlockSpec((1, F, 1), lambda i, be: (be[i], 0, 0)),
                  pl.BlockSpec((1, F, D), lambda i, be: (be[i], 0, 0)),
                  pl.BlockSpec((1, 1, D), lambda i, be: (be[i], 0, 0))],
        out_specs=pl.BlockSpec(memory_space=pl.ANY),
        scratch_shapes=[pltpu.VMEM((2, bm * ROW_TILE, 128), F32), pltpu.VMEM((2, bm * ROW_TILE, 128), F32),
                        pltpu.VMEM((bm, D), BF16),
                        pltpu.VMEM((2 * F, D), BF16), pltpu.VMEM((D, F), BF16),
                        pltpu.VMEM((bm // 128, 2 * F, 128), F32),
                        pltpu.SemaphoreType.DMA((2,)), pltpu.SemaphoreType.DMA((1,))],
    )
    return pl.pallas_call(
        _expert_kernel,
        grid_spec=grid_spec,
        out_shape=jax.ShapeDtypeStruct((n_rows_out * ROW_TILE, 128), F32),
        compiler_params=pltpu.CompilerParams(dimension_semantics=("arbitrary",),
                                             vmem_limit_bytes=EXPERT_VMEM_LIMIT),
        name="experts",
    )(blk_expert, tok3, tok3, dst3, dst3, h2, w1, b1g, b1l, w2, b2.reshape(N_EXPERTS, 1, D))


def _combine_kernel(y0_ref, y1_ref, y2_ref, y3_ref, gate_ref, x1_ref, gt2_ref, g_ref, o_ref):
    tm = x1_ref.shape[0]
    gates = gate_ref[...]

    def rows(y_ref):
        return jnp.concatenate([y_ref[pl.ds(s, tm, stride=ROW_TILE), :] for s in range(ROW_TILE)], axis=1)

    f = gates[:, 0:1] * rows(y0_ref)
    for k, y_ref in enumerate((y1_ref, y2_ref, y3_ref), start=1):
        f = f + gates[:, k:k + 1] * rows(y_ref)
    o_ref[...] = x1_ref[...] + gt2_ref[0] * (_rms(f) * g_ref[...])


def _combine(y_tok, gates, x1, gt2, g_post, tokens_per_batch):
    T, D = x1.shape
    B = gt2.shape[0]
    tm = min(COMBINE_TM, tokens_per_batch)
    steps_per_batch = tokens_per_batch // tm
    steps = T // tm
    y_spec = lambda k: pl.BlockSpec((tm * ROW_TILE, 128), lambda i: (k * steps + i, 0))
    return pl.pallas_call(
        _combine_kernel,
        grid=(steps,),
        in_specs=[y_spec(0), y_spec(1), y_spec(2), y_spec(3),
                  pl.BlockSpec((tm, TOP_K), lambda i: (i, 0)),
                  pl.BlockSpec((tm, D), lambda i: (i, 0)),
                  pl.BlockSpec((1, 1, D), lambda i: (i // steps_per_batch, 0, 0)),
                  pl.BlockSpec((1, D), lambda i: (0, 0))],
        out_specs=pl.BlockSpec((tm, D), lambda i: (i, 0)),
        out_shape=jax.ShapeDtypeStruct((T, D), F32),
        compiler_params=pltpu.CompilerParams(dimension_semantics=("arbitrary",),
                                             vmem_limit_bytes=VMEM_LIMIT),
        name="combine",
    )(y_tok, y_tok, y_tok, y_tok, gates, x1, gt2.reshape(B, 1, D), g_post.reshape(1, D))


def _dispatch_plan(eidx, rank, counts):
    T = eidx.shape[0]
    n_assign = eidx.size
    bm = MOE_BM
    cap = (n_assign + N_EXPERTS * (bm - 1) + bm - 1) // bm * bm
    n_blocks = cap // bm
    counts = counts.reshape(N_EXPERTS).astype(I32)
    padded = (counts + bm - 1) // bm * bm
    pad_end = jnp.cumsum(padded)
    pad_start = pad_end - padded
    eio = jnp.arange(N_EXPERTS, dtype=I32)
    start_of = jnp.sum(jnp.where(eidx[..., None] == eio, pad_start, 0), axis=-1)
    dest = (start_of + rank).reshape(-1)
    blk_start = jnp.arange(n_blocks, dtype=I32) * bm
    blk_expert = jnp.minimum(jnp.sum((blk_start[:, None] >= pad_end[None, :]).astype(I32), axis=-1),
                             N_EXPERTS - 1)
    inv = jnp.zeros((cap,), I32).at[dest].set(jnp.arange(1, n_assign + 1, dtype=I32))
    is_pad = inv == 0
    a = inv - 1
    tok_buf = jnp.where(is_pad, 0, a // TOP_K).astype(I32)
    pad_row = n_assign + jnp.cumsum(is_pad.astype(I32)) - 1
    dst_buf = jnp.where(is_pad, pad_row, (a % TOP_K) * T + a // TOP_K).astype(I32)
    return blk_expert, tok_buf, dst_buf, cap + bm


def _layer(x, c, l, w_ada, b_ada, g_pre_mix, g_post_mix, w_in, w_out, lam_q1, lam_k1, lam_q2, lam_k2,
           g_subln, nat_rpb, g_pre_ffn, g_post_ffn, w_router, b_router, w1, b1, w2, b2):
    B, L, D = x.shape
    lam_init = 0.8 - 0.6 * math.exp(-0.3 * l)
    mod, lam = _ada(c, w_ada, b_ada, lam_q1, lam_k1, lam_q2, lam_k2, lam_init)
    sh1, sc1, gt1, sh2, sc2, gt2 = jnp.split(mod, 6, axis=-1)

    qT, kd, vT, nq, nk, nv = _inproj(x, g_pre_mix, sc1, sh1, w_in)
    o_diff = _diff_attention(qT, kd, vT, lam, g_subln, lam_init)
    o_nat = _nat_attention(nq, nk, nv, nat_rpb)

    T = B * L
    x1, h2, eidx, gates, rank, counts = _out_router(
        o_diff.reshape(T, DIFF_WIDTH), o_nat.reshape(T, NAT_WIDTH), x.reshape(T, D), w_out,
        g_post_mix, gt1, g_pre_ffn, sc2, sh2, w_router, b_router, L)
    blk_expert, tok_buf, dst_buf, n_rows_out = _dispatch_plan(eidx, rank, counts)
    y_tok = _experts(h2, blk_expert, tok_buf, dst_buf, w1, b1, w2, b2, n_rows_out)
    out = _combine(y_tok, gates, x1, gt2, g_post_ffn, L)
    return out.reshape(B, L, D)


def kernel(x, c, w_ada, b_ada, g_pre_mix, g_post_mix, w_in, w_out, lam_q1, lam_k1, lam_q2, lam_k2,
           g_subln, nat_rpb, g_pre_ffn, g_post_ffn, w_router, b_router, w1, b1, w2, b2):
    depth = w_ada.shape[0]
    for l in range(depth):
        x = _layer(x, c, l, w_ada[l], b_ada[l], g_pre_mix[l], g_post_mix[l], w_in[l], w_out[l],
                   lam_q1[l], lam_k1[l], lam_q2[l], lam_k2[l], g_subln[l], nat_rpb[l],
                   g_pre_ffn[l], g_post_ffn[l], w_router[l], b_router[l], w1[l], b1[l], w2[l], b2[l])
    return x
```

```python
import functools
import math

import jax
import jax.numpy as jnp
from jax import lax
from jax.experimental import pallas as pl
from jax.experimental.pallas import tpu as pltpu

F32 = jnp.float32
BF16 = jnp.bfloat16
I32 = jnp.int32

HEAD_DIM = 64
N_DIFF_HEADS = 4
DIFF_HEAD_W = 2 * HEAD_DIM
DIFF_WIDTH = N_DIFF_HEADS * DIFF_HEAD_W
N_NAT_HEADS = 8
NAT_WIDTH = N_NAT_HEADS * HEAD_DIM
NAT_PAIRS = N_NAT_HEADS // 2
GRID_W = 64
NAT_KH = 8
NAT_KW = 16
N_EXPERTS = 32
TOP_K = 4
SWIGLU_LIMIT = 7.0
SWIGLU_ALPHA = 1.702
RMS_EPS = 1e-6
NEG_BIG = -1e30
LOG2E = 1.4426950408889634
ROW_TILE = 8

NT_DIMS = (((1,), (1,)), ((), ()))

ADA_TN = 1536
INPROJ_TM = 512
DIFF_TQ = 256
DIFF_TK = 512
NAT_GROUP = 8
NAT_KEY_ROWS = 16
OUT_TM = 512
MOE_BM = 256
COMBINE_TM = 512
VMEM_LIMIT = 48 * 1024 * 1024
EXPERT_VMEM_LIMIT = 56 * 1024 * 1024


def _rms(x, axis=-1):
    return x * lax.rsqrt(jnp.mean(x * x, axis=axis, keepdims=True) + RMS_EPS)


def _ada_kernel(lam_init, c_ref, w_ref, b_ref, lq1_ref, lk1_ref, lq2_ref, lk2_ref, mod_ref, lam_ref):
    c = c_ref[...]
    s = c * (1.0 / (1.0 + jnp.exp(-c)))
    mod_ref[...] = jnp.dot(s, w_ref[...], preferred_element_type=F32,
                           precision=lax.Precision.HIGHEST) + b_ref[...]
    d1 = jnp.sum(lq1_ref[...] * lk1_ref[...], axis=-1, keepdims=True)
    d2 = jnp.sum(lq2_ref[...] * lk2_ref[...], axis=-1, keepdims=True)
    lam = jnp.exp(d1) - jnp.exp(d2) + lam_init
    lam_ref[...] = jnp.broadcast_to(lam, lam_ref.shape)


def _ada(c, w_ada, b_ada, lq1, lk1, lq2, lk2, lam_init):
    B, D = c.shape
    N = w_ada.shape[1]
    c8 = jnp.zeros((8, D), F32).at[:B].set(c)
    vec = pl.BlockSpec((1, HEAD_DIM), lambda j: (0, 0))
    mod, lam = pl.pallas_call(
        functools.partial(_ada_kernel, lam_init),
        grid=(N // ADA_TN,),
        in_specs=[pl.BlockSpec((8, D), lambda j: (0, 0)),
                  pl.BlockSpec((D, ADA_TN), lambda j: (0, j)),
                  pl.BlockSpec((1, ADA_TN), lambda j: (0, j)),
                  vec, vec, vec, vec],
        out_specs=[pl.BlockSpec((8, ADA_TN), lambda j: (0, j)),
                   pl.BlockSpec((8, 128), lambda j: (0, 0))],
        out_shape=[jax.ShapeDtypeStruct((8, N), F32), jax.ShapeDtypeStruct((8, 128), F32)],
        compiler_params=pltpu.CompilerParams(dimension_semantics=("arbitrary",),
                                             vmem_limit_bytes=VMEM_LIMIT),
        name="ada",
    )(c8, w_ada, b_ada.reshape(1, N), lq1.reshape(1, -1), lk1.reshape(1, -1),
      lq2.reshape(1, -1), lk2.reshape(1, -1))
    return mod[:B], lam


def _inproj_kernel(x_ref, g_ref, sc_ref, sh_ref, wqT_ref, wvT_ref, wn_ref,
                   qT_ref, k_ref, vT_ref, nq_ref, nk_ref, nv_ref):
    h = _rms(x_ref[0]) * g_ref[...]
    h = h * (1.0 + sc_ref[0]) + sh_ref[0]
    hb = h.astype(BF16)
    qT_ref[0] = lax.dot_general(wqT_ref[...], hb, NT_DIMS, preferred_element_type=F32).astype(BF16)
    vT_ref[0] = lax.dot_general(wvT_ref[...], hb, NT_DIMS, preferred_element_type=F32).astype(BF16)
    rest = jnp.dot(hb, wn_ref[...], preferred_element_type=F32).astype(BF16)
    k_ref[0] = rest[:, 0:512]
    nq_ref[0] = rest[:, 512:1024]
    nk_ref[0] = rest[:, 1024:1536]
    nv_ref[0] = rest[:, 1536:2048]


def _inproj(x, g_pre, sc1, sh1, w_in):
    B, L, D = x.shape
    tm = INPROJ_TM
    scale = HEAD_DIM ** -0.5
    wqT = (w_in[:, 0:512] * (scale * LOG2E)).T.astype(BF16)
    wvT = w_in[:, 1024:1536].T.astype(BF16)
    wn = jnp.concatenate([w_in[:, 512:1024], w_in[:, 1536:2048] * scale, w_in[:, 2048:3072]],
                         axis=1).astype(BF16)
    row_major = pl.BlockSpec((1, tm, 512), lambda b, i: (b, i, 0))
    col_major = pl.BlockSpec((1, 512, tm), lambda b, i: (b, 0, i))
    modv = pl.BlockSpec((1, 1, D), lambda b, i: (b, 0, 0))
    rm_shape = jax.ShapeDtypeStruct((B, L, 512), BF16)
    cm_shape = jax.ShapeDtypeStruct((B, 512, L), BF16)
    return pl.pallas_call(
        _inproj_kernel,
        grid=(B, L // tm),
        in_specs=[pl.BlockSpec((1, tm, D), lambda b, i: (b, i, 0)),
                  pl.BlockSpec((1, D), lambda b, i: (0, 0)),
                  modv, modv,
                  pl.BlockSpec((512, D), lambda b, i: (0, 0)),
                  pl.BlockSpec((512, D), lambda b, i: (0, 0)),
                  pl.BlockSpec((D, 2048), lambda b, i: (0, 0))],
        out_specs=[col_major, row_major, col_major, row_major, row_major, row_major],
        out_shape=[cm_shape, rm_shape, cm_shape, rm_shape, rm_shape, rm_shape],
        compiler_params=pltpu.CompilerParams(dimension_semantics=("arbitrary", "arbitrary"),
                                             vmem_limit_bytes=VMEM_LIMIT),
        name="inproj",
    )(x, g_pre.reshape(1, D), sc1.reshape(B, 1, D), sh1.reshape(B, 1, D), wqT, wvT, wn)


def _diff_kernel(lam_init, n_kchunks, coef_ref, lam_ref, qT_ref, k_ref, kaug_ref, vT_ref, g_ref, o_ref,
                 acc_ref, m_ref, l_ref, s0_ref, qd_ref, qm_ref, cq_ref, s_ref, p_ref, a_ref):
    tq = qT_ref.shape[2]
    tk = s0_ref.shape[0]
    h = pl.program_id(1)
    q0 = pl.program_id(2) * tq
    c_f = coef_ref[4 * h]
    c_parts = (coef_ref[4 * h + 1], coef_ref[4 * h + 2], coef_ref[4 * h + 3])
    c_diag = lax.div(q0, tk)

    qT = qT_ref[0]
    row = lax.broadcasted_iota(I32, qT.shape, 0)
    zero = jnp.zeros_like(qT)
    q_maps = (jnp.where(row < HEAD_DIM, qT, zero), jnp.where(row >= HEAD_DIM, qT, zero))
    aug = jnp.zeros(qT.shape, F32)
    for j in range(3):
        aug = jnp.where(row == j, 64.0 * c_parts[j], aug)
        aug = jnp.where(row == 3 + j, c_parts[j], aug)
    for mi in range(2):
        qd_ref[mi] = q_maps[mi]
        for sg, sign in enumerate((1.0, -1.0)):
            qm_ref[mi, sg, 0:DIFF_HEAD_W, :] = q_maps[mi]
            qm_ref[mi, sg, DIFF_HEAD_W:, :] = (sign * aug).astype(BF16)

    qpos = q0 + lax.broadcasted_iota(I32, (1, tq), 1)
    cq_ref[...] = c_f * qpos.astype(F32)
    kk = lax.broadcasted_iota(I32, (tk, tq), 0)
    qq = lax.broadcasted_iota(I32, (tk, tq), 1)
    s0_ref[...] = c_f * (kk - qq).astype(F32)
    acc_ref[...] = jnp.zeros_like(acc_ref)
    l_ref[...] = jnp.zeros_like(l_ref)
    m_ref[...] = jnp.full_like(m_ref, NEG_BIG)

    def chunk(i):
        j = i - 1
        c = j + (j >= c_diag).astype(I32)
        return c_diag if isinstance(i, int) and i == 0 else jnp.where(i == 0, c_diag, c)

    def start(c):
        return pl.multiple_of(c * tk, tk)

    def update(mi, slot, logits, shift):
        m_old = m_ref[mi]
        m_new = jnp.maximum(m_old, jnp.max(logits, axis=0, keepdims=True) - shift)
        alpha = jnp.exp2(m_old - m_new)
        p = jnp.exp2(logits - (m_new + shift))
        l_ref[mi] = alpha * l_ref[mi] + jnp.sum(p, axis=0, keepdims=True)
        m_ref[mi] = m_new
        a_ref[slot, mi] = alpha
        p_ref[slot, mi] = p.astype(BF16)

    def scores_diag(slot):
        kb = k_ref[0, pl.ds(start(c_diag), tk), :]
        for mi in range(2):
            s_ref[slot, mi] = jnp.dot(kb, qd_ref[mi], preferred_element_type=F32)

    def softmax_diag(slot):
        bias = jnp.abs(s0_ref[...] + c_f * (start(c_diag) - q0).astype(F32))
        for mi in range(2):
            update(mi, slot, s_ref[slot, mi] - bias, 0.0)

    def scores(c, slot):
        k0 = start(c)
        kb = jnp.concatenate([k_ref[0, pl.ds(k0, tk), :], kaug_ref[pl.ds(k0, tk), :]], axis=1)
        sg = (c > c_diag).astype(I32)
        for mi in range(2):
            s_ref[slot, mi] = jnp.dot(kb, qm_ref[mi, sg], preferred_element_type=F32)

    def softmax(c, slot):
        shift = jnp.where(c > c_diag, -1.0, 1.0) * cq_ref[...]
        for mi in range(2):
            update(mi, slot, s_ref[slot, mi], shift)

    def values(c, slot):
        vb = vT_ref[0, :, pl.ds(start(c), tk)]
        for mi in range(2):
            acc_ref[mi] = a_ref[slot, mi] * acc_ref[mi] + jnp.dot(vb, p_ref[slot, mi],
                                                                  preferred_element_type=F32)

    n = n_kchunks
    scores_diag(0)
    scores(chunk(1), 1)
    scores(chunk(2), 2)
    softmax_diag(0)
    scores(chunk(3), 0)
    softmax(chunk(1), 1)

    def three_ticks(j, carry):
        t0 = 3 * j + 2
        for k in range(3):
            t = t0 + k
            scores(chunk(t + 2), (k + 1) % 3)
            softmax(chunk(t), (k + 2) % 3)
            values(chunk(t - 2), k)
        return carry

    lax.fori_loop(0, (n - 4) // 3, three_ticks, 0)
    for t in (n - 2, n - 1):
        softmax(chunk(t), t % 3)
        values(chunk(t - 2), (t - 2) % 3)
    values(chunk(n - 2), (n - 2) % 3)
    values(chunk(n - 1), (n - 1) % 3)

    lam = lam_ref[0:1, 0:1]
    o = acc_ref[0] / l_ref[0] - lam * (acc_ref[1] / l_ref[1])
    y = _rms(o, axis=0) * g_ref[...] * (1.0 - lam_init)
    o_ref[0] = y.T.astype(BF16)


def _diff_attention(qT, k, vT, lam, g_sub, lam_init):
    B, _, L = qT.shape
    tq, tk = min(DIFF_TQ, L), min(DIFF_TK, L)
    assert L // tk >= 4 and (L // tk - 4) % 3 == 0 and tk % tq == 0 and L <= 64 * 256
    i = jnp.arange(1, N_DIFF_HEADS + 1, dtype=F32)
    c = jnp.exp2(-8.0 * i / N_DIFF_HEADS) * LOG2E
    c1 = c.astype(BF16).astype(F32)
    c2 = (c - c1).astype(BF16).astype(F32)
    c3 = (c - c1 - c2).astype(BF16).astype(F32)
    coef = jnp.stack([c, c1, c2, c3], axis=1).reshape(-1)
    kpos = jnp.arange(L, dtype=I32)
    lane = jnp.arange(DIFF_HEAD_W, dtype=I32)
    kaug = jnp.where(lane[None, :] < 3, (kpos // 64)[:, None],
                     jnp.where(lane[None, :] < 6, (kpos % 64)[:, None], 0)).astype(BF16)
    return pl.pallas_call(
        functools.partial(_diff_kernel, lam_init, L // tk),
        grid=(B, N_DIFF_HEADS, L // tq),
        in_specs=[pl.BlockSpec(memory_space=pltpu.SMEM),
                  pl.BlockSpec((8, 128), lambda b, h, i: (0, 0)),
                  pl.BlockSpec((1, DIFF_HEAD_W, tq), lambda b, h, i: (b, h, i)),
                  pl.BlockSpec((1, L, DIFF_HEAD_W), lambda b, h, i: (b, 0, h)),
                  pl.BlockSpec((L, DIFF_HEAD_W), lambda b, h, i: (0, 0)),
                  pl.BlockSpec((1, DIFF_HEAD_W, L), lambda b, h, i: (b, h, 0)),
                  pl.BlockSpec((DIFF_HEAD_W, 1), lambda b, h, i: (0, 0))],
        out_specs=pl.BlockSpec((1, tq, DIFF_HEAD_W), lambda b, h, i: (b, i, h)),
        out_shape=jax.ShapeDtypeStruct((B, L, DIFF_WIDTH), BF16),
        scratch_shapes=[pltpu.VMEM((2, DIFF_HEAD_W, tq), F32),
                        pltpu.VMEM((2, 1, tq), F32),
                        pltpu.VMEM((2, 1, tq), F32),
                        pltpu.VMEM((tk, tq), F32),
                        pltpu.VMEM((2, DIFF_HEAD_W, tq), BF16),
                        pltpu.VMEM((2, 2, 2 * DIFF_HEAD_W, tq), BF16),
                        pltpu.VMEM((1, tq), F32),
                        pltpu.VMEM((3, 2, tk, tq), F32),
                        pltpu.VMEM((3, 2, tk, tq), BF16),
                        pltpu.VMEM((3, 2, 1, tq), F32)],
        compiler_params=pltpu.CompilerParams(
            dimension_semantics=("arbitrary", "arbitrary", "arbitrary"),
            vmem_limit_bytes=VMEM_LIMIT),
        name="diff_attn",
    )(coef, lam, qT, k, kaug, vT, g_sub.reshape(DIFF_HEAD_W, 1))


def _nat_group_span(g, n_rows):
    r0 = g * NAT_GROUP
    lo, hi = 0, n_rows - NAT_KEY_ROWS
    kb0 = r0 - NAT_KH // 2
    kb0 = min(max(kb0, lo), hi) if isinstance(g, int) else jnp.clip(kb0, lo, hi)
    return r0, kb0


def _nat_bias_table(rpb, n_rows):
    c = jnp.arange(GRID_W)
    col_start = jnp.clip(c - NAT_KW // 2, 0, GRID_W - NAT_KW)
    col_in = (c[None, :] >= col_start[:, None]) & (c[None, :] < col_start[:, None] + NAT_KW)
    dc = jnp.clip(c[None, :] - c[:, None], -(NAT_KW - 1), NAT_KW - 1) + (NAT_KW - 1)
    rpb = rpb.astype(F32)
    cols = jnp.zeros(rpb.shape[:2] + (GRID_W, GRID_W), F32)
    for j in range(2 * NAT_KW - 1):
        cols = cols + jnp.where(dc == j, rpb[:, :, j][:, :, None, None], 0.0)
    cols = jnp.where(col_in[None, None], cols, NEG_BIG)
    cols = cols.transpose(0, 2, 1, 3)
    outside = jnp.full(cols.shape[:2] + (1, GRID_W), NEG_BIG, F32)
    n_groups = n_rows // NAT_GROUP
    kinds = []
    for g in (0, 1, n_groups - 1):
        r0, kb0 = _nat_group_span(min(g, n_groups - 1), n_rows)
        per_row = []
        for j in range(NAT_GROUP):
            r = r0 + j
            rs = min(max(r - NAT_KH // 2, 0), n_rows - NAT_KH)
            pieces = []
            for kr in range(NAT_KEY_ROWS):
                krow = kb0 + kr
                dr = krow - r + (NAT_KH - 1)
                pieces.append(cols[:, :, dr:dr + 1] if rs <= krow < rs + NAT_KH else outside)
            per_row.append(jnp.concatenate(pieces, axis=2))
        kinds.append(jnp.stack(per_row, axis=1))
    tbl = jnp.stack(kinds, axis=1)
    tbl = tbl.reshape(NAT_PAIRS, 2, 3, NAT_GROUP * GRID_W, NAT_KEY_ROWS * GRID_W)
    return tbl.transpose(0, 2, 1, 3, 4)


def _nat_kernel(n_rows, q_ref, k_ref, v_ref, bias_ref, o_ref):
    _, kb0 = _nat_group_span(pl.program_id(2), n_rows)
    k0 = pl.multiple_of(kb0 * GRID_W, GRID_W)
    win = NAT_KEY_ROWS * GRID_W
    kw = k_ref[0, pl.ds(k0, win), :]
    vw = v_ref[0, pl.ds(k0, win), :]
    q = q_ref[0]
    lane = lax.broadcasted_iota(I32, q.shape, 1)
    zero = jnp.zeros_like(q)
    outs = []
    for hh in range(2):
        keep = (lane < HEAD_DIM) if hh == 0 else (lane >= HEAD_DIM)
        s = lax.dot_general(jnp.where(keep, q, zero), kw, NT_DIMS, preferred_element_type=F32)
        s = s + bias_ref[0, 0, hh]
        p = jnp.exp(s - jnp.max(s, axis=-1, keepdims=True))
        l = jnp.sum(p, axis=-1, keepdims=True)
        outs.append(jnp.dot(p.astype(BF16), vw, preferred_element_type=F32) / l)
    o_ref[0] = jnp.where(lane < HEAD_DIM, outs[0], outs[1]).astype(BF16)


def _nat_attention(nq, nk, nv, rpb):
    B, L, _ = nq.shape
    n_rows = L // GRID_W
    assert n_rows >= NAT_KEY_ROWS and n_rows % NAT_GROUP == 0
    n_groups = n_rows // NAT_GROUP
    tbl = _nat_bias_table(rpb, n_rows)
    kv = pl.BlockSpec((1, L, 2 * HEAD_DIM), lambda b, p, i: (b, 0, p))
    qo = pl.BlockSpec((1, NAT_GROUP * GRID_W, 2 * HEAD_DIM), lambda b, p, i: (b, i, p))
    kind = lambda i: jnp.where(i == 0, 0, jnp.where(i == n_groups - 1, 2, 1))
    return pl.pallas_call(
        functools.partial(_nat_kernel, n_rows),
        grid=(B, NAT_PAIRS, n_groups),
        in_specs=[qo, kv, kv,
                  pl.BlockSpec((1, 1, 2, NAT_GROUP * GRID_W, NAT_KEY_ROWS * GRID_W),
                               lambda b, p, i: (p, kind(i), 0, 0, 0))],
        out_specs=qo,
        out_shape=jax.ShapeDtypeStruct((B, L, NAT_WIDTH), BF16),
        compiler_params=pltpu.CompilerParams(
            dimension_semantics=("arbitrary", "arbitrary", "arbitrary"),
            vmem_limit_bytes=VMEM_LIMIT),
        name="nat_attn",
    )(nq, nk, nv, tbl)


def _out_kernel(od_ref, on_ref, x_ref, wt_ref, wb_ref, gpost_ref, gt1_ref, gpre_ref, sc2_ref, sh2_ref,
                wr_ref, br_ref, x1_ref, h2_ref, eidx_ref, gate_ref, rank_ref, cnt_ref, carry_ref):
    tm = x_ref.shape[0]

    @pl.when(pl.program_id(0) == 0)
    def _():
        carry_ref[...] = jnp.zeros_like(carry_ref)

    mix = (jnp.dot(od_ref[...], wt_ref[...], preferred_element_type=F32)
           + jnp.dot(on_ref[...], wb_ref[...], preferred_element_type=F32))
    x1 = x_ref[...] + gt1_ref[0] * (_rms(mix) * gpost_ref[...])
    x1_ref[...] = x1
    h2 = _rms(x1) * gpre_ref[...]
    h2 = h2 * (1.0 + sc2_ref[0]) + sh2_ref[0]
    for s in range(ROW_TILE):
        h2_ref[pl.ds(s, tm, stride=ROW_TILE), :] = h2[:, s * 128:(s + 1) * 128]

    logits = jnp.dot(h2, wr_ref[...], preferred_element_type=F32,
                     precision=lax.Precision.HIGHEST) + br_ref[...]
    eio = lax.broadcasted_iota(I32, logits.shape, 1).astype(F32)
    onehot = jnp.zeros_like(logits)
    vals, idxs, sels = [], [], []
    cur = logits
    for _ in range(TOP_K):
        mx = jnp.max(cur, axis=-1, keepdims=True)
        idx = jnp.min(jnp.where(cur == mx, eio, float(N_EXPERTS)), axis=-1, keepdims=True)
        sel = eio == idx
        vals.append(mx)
        idxs.append(idx)
        sels.append(sel)
        cur = jnp.where(sel, -jnp.inf, cur)
        onehot = onehot + sel.astype(F32)

    ex = [jnp.exp(v - vals[0]) for v in vals]
    tot = ex[0] + ex[1] + ex[2] + ex[3]

    rr = lax.broadcasted_iota(I32, (tm, tm), 0)
    cc = lax.broadcasted_iota(I32, (tm, tm), 1)
    tri = (rr > cc).astype(BF16)
    carry = carry_ref[...]
    cum = jnp.dot(tri, onehot.astype(BF16), preferred_element_type=F32) + carry
    ranks = [jnp.sum(jnp.where(sel, cum, 0.0), axis=-1, keepdims=True) for sel in sels]
    carry = carry + jnp.sum(onehot, axis=0, keepdims=True)
    carry_ref[...] = carry
    cnt_ref[...] = carry

    kio = lax.broadcasted_iota(I32, (tm, TOP_K), 1)

    def pack(cols):
        out = jnp.broadcast_to(cols[TOP_K - 1], (tm, TOP_K))
        for k in range(TOP_K - 2, -1, -1):
            out = jnp.where(kio == k, cols[k], out)
        return out

    eidx_ref[...] = pack(idxs).astype(I32)
    gate_ref[...] = pack([e / tot for e in ex])
    rank_ref[...] = pack(ranks).astype(I32)


def _out_router(od, on, x, w_out, g_post, gt1, g_pre, sc2, sh2, w_router, b_router, tokens_per_batch):
    T, D = x.shape
    B = gt1.shape[0]
    tm = min(OUT_TM, tokens_per_batch)
    steps_per_batch = tokens_per_batch // tm
    wt = w_out[:DIFF_WIDTH].astype(BF16)
    wb = w_out[DIFF_WIDTH:].astype(BF16)
    rowblk = lambda w: pl.BlockSpec((tm, w), lambda i: (i, 0))
    const = lambda shape: pl.BlockSpec(shape, lambda i: (0,) * len(shape))
    modv = pl.BlockSpec((1, 1, D), lambda i: (i // steps_per_batch, 0, 0))
    return pl.pallas_call(
        _out_kernel,
        grid=(T // tm,),
        in_specs=[rowblk(DIFF_WIDTH), rowblk(NAT_WIDTH), rowblk(D),
                  const((DIFF_WIDTH, D)), const((NAT_WIDTH, D)), const((1, D)), modv, const((1, D)),
                  modv, modv, const((D, N_EXPERTS)), const((1, N_EXPERTS))],
        out_specs=[rowblk(D), pl.BlockSpec((tm * ROW_TILE, 128), lambda i: (i, 0)),
                   rowblk(TOP_K), rowblk(TOP_K), rowblk(TOP_K), const((1, N_EXPERTS))],
        out_shape=[jax.ShapeDtypeStruct((T, D), F32), jax.ShapeDtypeStruct((T * ROW_TILE, 128), F32),
                   jax.ShapeDtypeStruct((T, TOP_K), I32), jax.ShapeDtypeStruct((T, TOP_K), F32),
                   jax.ShapeDtypeStruct((T, TOP_K), I32), jax.ShapeDtypeStruct((1, N_EXPERTS), F32)],
        scratch_shapes=[pltpu.VMEM((1, N_EXPERTS), F32)],
        compiler_params=pltpu.CompilerParams(dimension_semantics=("arbitrary",),
                                             vmem_limit_bytes=VMEM_LIMIT),
        name="out_router",
    )(od, on, x, wt, wb, g_post.reshape(1, D), gt1.reshape(B, 1, D), g_pre.reshape(1, D),
      sc2.reshape(B, 1, D), sh2.reshape(B, 1, D), w_router, b_router.reshape(1, N_EXPERTS))


def _expert_kernel(spare_base, be_ref, tok_ref, tok_next_ref, dst_prev_ref, dst_ref, h2_hbm, w1_ref, b1g_ref, b1l_ref,
                   w2_ref, b2_ref, y_hbm, xbuf, ybuf, xb, w1t, w2t, hh, gsem, ssem):
    i = pl.program_id(0)
    n = pl.num_programs(0)
    bm = xbuf.shape[1] // ROW_TILE

    def tile_rows(t):
        t = t * ROW_TILE
        return pl.ds(t if isinstance(t, int) else pl.multiple_of(t, ROW_TILE), ROW_TILE)
    ff = w2_ref.shape[1]
    slot = i % 2

    def gather_row(idx_ref, s, r):
        return pltpu.make_async_copy(h2_hbm.at[tile_rows(idx_ref[0, 0, r]), :],
                                     xbuf.at[s, tile_rows(r), :], gsem.at[s])

    def scatter_row(idx_ref, s, r):
        return pltpu.make_async_copy(ybuf.at[s, tile_rows(r), :],
                                     y_hbm.at[tile_rows(idx_ref[0, 0, r]), :], ssem.at[s])

    def wait_gather(s):
        pltpu.make_async_copy(h2_hbm.at[pl.ds(0, bm * ROW_TILE), :], xbuf.at[s], gsem.at[s]).wait()

    def wait_scatter(s):
        pltpu.make_async_copy(ybuf.at[s], y_hbm.at[pl.ds(0, bm * ROW_TILE), :], ssem.at[s]).wait()

    @pl.when(i == 0)
    def _():
        ybuf[...] = jnp.zeros(ybuf.shape, F32)
        for r in range(bm):
            pltpu.make_async_copy(ybuf.at[0, tile_rows(r), :], y_hbm.at[tile_rows(spare_base + r), :],
                                  ssem.at[0]).start(priority=r % 2)
        for r in range(bm):
            gather_row(tok_ref, 0, r).start(priority=r % 2)

    @pl.when((i == 0) | (be_ref[i] != be_ref[jnp.maximum(i - 1, 0)]))
    def _():
        cw = 256
        for c in range(w1_ref.shape[2] // cw):
            w1t[c * cw:(c + 1) * cw, :] = w1_ref[0, :, c * cw:(c + 1) * cw].T.astype(BF16)
        for c in range(w2_ref.shape[2] // cw):
            w2t[c * cw:(c + 1) * cw, :] = w2_ref[0, :, c * cw:(c + 1) * cw].T.astype(BF16)

    wait_gather(slot)
    for s8 in range(ROW_TILE):
        xb[:, s8 * 128:(s8 + 1) * 128] = xbuf[slot, pl.ds(s8, bm, stride=ROW_TILE), :].astype(BF16)

    n_lane_tiles = bm // 128

    @pl.when(i >= 0)
    def _():
        for r in range(bm):
            gather_row(tok_next_ref, 1 - slot, r).start(priority=r % 2)
        hh_t = lax.dot_general(w1t[...], xb[...], NT_DIMS, preferred_element_type=F32)
        for j in range(n_lane_tiles):
            hh[j] = hh_t[:, j * 128:(j + 1) * 128]

    for r in range(bm):
        scatter_row(dst_prev_ref, 1 - slot, r).start(priority=r % 2)
    even = jnp.concatenate([hh[j, pl.ds(0, ff, stride=2), :] for j in range(n_lane_tiles)], axis=1)
    odd = jnp.concatenate([hh[j, pl.ds(1, ff, stride=2), :] for j in range(n_lane_tiles)], axis=1)
    glu = jnp.minimum(even + b1g_ref[0], SWIGLU_LIMIT)
    lin = jnp.clip(odd + b1l_ref[0], -SWIGLU_LIMIT, SWIGLU_LIMIT)
    act = glu * (1.0 / (1.0 + jnp.exp(-SWIGLU_ALPHA * glu))) * (lin + 1.0)
    y_t = jnp.dot(w2t[...], act.astype(BF16), preferred_element_type=F32)

    y = y_t.T + b2_ref[0]
    wait_scatter(slot)
    for s8 in range(ROW_TILE):
        ybuf[slot, pl.ds(s8, bm, stride=ROW_TILE), :] = y[:, s8 * 128:(s8 + 1) * 128]

    @pl.when(i == n - 1)
    def _():
        wait_scatter(1 - slot)
        for r in range(bm):
            scatter_row(dst_ref, slot, r).start(priority=r % 2)
        wait_scatter(slot)
        wait_gather(1 - slot)


def _experts(h2, blk_expert, tok_buf, dst_buf, w1, b1, w2, b2, n_rows_out):
    D = w1.shape[1]
    assert D == ROW_TILE * 128 and h2.shape[1] == 128
    n_blocks = blk_expert.shape[0]
    bm = MOE_BM
    F = w2.shape[1]
    b1g = b1[:, 0::2].reshape(N_EXPERTS, F, 1)
    b1l = b1[:, 1::2].reshape(N_EXPERTS, F, 1)
    tok3 = tok_buf.reshape(n_blocks, 1, bm)
    spare = n_rows_out - 2 * bm + jnp.arange(bm, dtype=I32)
    dst3 = jnp.concatenate([spare, dst_buf]).reshape(n_blocks + 1, 1, bm)
    smem_blk = lambda fn: pl.BlockSpec((1, 1, bm), fn, memory_space=pltpu.SMEM)
    grid_spec = pltpu.PrefetchScalarGridSpec(
        num_scalar_prefetch=1,
        grid=(n_blocks,),
        in_specs=[smem_blk(lambda i, be: (i, 0, 0)),
                  smem_blk(lambda i, be: (jnp.minimum(i + 1, n_blocks - 1), 0, 0)),
                  smem_blk(lambda i, be: (i, 0, 0)),
                  smem_blk(lambda i, be: (i + 1, 0, 0)),
                  pl.BlockSpec(memory_space=pl.ANY),
                  pl.BlockSpec((1, D, 2 * F), lambda i, be: (be[i], 0, 0)),
                  pl.BlockSpec((1, F, 1), lambda i, be: (be[i], 0, 0)),
                  pl.BlockSpec((1, F, 1), lambda i, be: (be[i], 0, 0)),
                  pl.BlockSpec((1, F, D), lambda i, be: (be[i], 0, 0)),
                  pl.BlockSpec((1, 1, D), lambda i, be: (be[i], 0, 0))],
        out_specs=pl.BlockSpec(memory_space=pl.ANY),
        scratch_shapes=[pltpu.VMEM((2, bm * ROW_TILE, 128), F32), pltpu.VMEM((2, bm * ROW_TILE, 128), F32),
                        pltpu.VMEM((bm, D), BF16),
                        pltpu.VMEM((2 * F, D), BF16), pltpu.VMEM((D, F), BF16),
                        pltpu.VMEM((bm // 128, 2 * F, 128), F32),
                        pltpu.SemaphoreType.DMA((2,)), pltpu.SemaphoreType.DMA((2,))],
    )
    return pl.pallas_call(
        functools.partial(_expert_kernel, n_rows_out - bm),
        grid_spec=grid_spec,
        out_shape=jax.ShapeDtypeStruct((n_rows_out * ROW_TILE, 128), F32),
        compiler_params=pltpu.CompilerParams(dimension_semantics=("arbitrary",),
                                             vmem_limit_bytes=EXPERT_VMEM_LIMIT),
        name="experts",
    )(blk_expert, tok3, tok3, dst3, dst3, h2, w1, b1g, b1l, w2, b2.reshape(N_EXPERTS, 1, D))


def _combine_kernel(y0_ref, y1_ref, y2_ref, y3_ref, gate_ref, x1_ref, gt2_ref, g_ref, o_ref):
    tm = x1_ref.shape[0]
    gates = gate_ref[...]

    def rows(y_ref):
        return jnp.concatenate([y_ref[pl.ds(s, tm, stride=ROW_TILE), :] for s in range(ROW_TILE)], axis=1)

    f = gates[:, 0:1] * rows(y0_ref)
    for k, y_ref in enumerate((y1_ref, y2_ref, y3_ref), start=1):
        f = f + gates[:, k:k + 1] * rows(y_ref)
    o_ref[...] = x1_ref[...] + gt2_ref[0] * (_rms(f) * g_ref[...])


def _combine(y_tok, gates, x1, gt2, g_post, tokens_per_batch):
    T, D = x1.shape
    B = gt2.shape[0]
    tm = min(COMBINE_TM, tokens_per_batch)
    steps_per_batch = tokens_per_batch // tm
    steps = T // tm
    y_spec = lambda k: pl.BlockSpec((tm * ROW_TILE, 128), lambda i: (k * steps + i, 0))
    return pl.pallas_call(
        _combine_kernel,
        grid=(steps,),
        in_specs=[y_spec(0), y_spec(1), y_spec(2), y_spec(3),
                  pl.BlockSpec((tm, TOP_K), lambda i: (i, 0)),
                  pl.BlockSpec((tm, D), lambda i: (i, 0)),
                  pl.BlockSpec((1, 1, D), lambda i: (i // steps_per_batch, 0, 0)),
                  pl.BlockSpec((1, D), lambda i: (0, 0))],
        out_specs=pl.BlockSpec((tm, D), lambda i: (i, 0)),
        out_shape=jax.ShapeDtypeStruct((T, D), F32),
        compiler_params=pltpu.CompilerParams(dimension_semantics=("arbitrary",),
                                             vmem_limit_bytes=VMEM_LIMIT),
        name="combine",
    )(y_tok, y_tok, y_tok, y_tok, gates, x1, gt2.reshape(B, 1, D), g_post.reshape(1, D))


def _dispatch_plan(eidx, rank, counts):
    T = eidx.shape[0]
    n_assign = eidx.size
    bm = MOE_BM
    cap = (n_assign + N_EXPERTS * (bm - 1) + bm - 1) // bm * bm
    n_blocks = cap // bm
    counts = counts.reshape(N_EXPERTS).astype(I32)
    padded = (counts + bm - 1) // bm * bm
    pad_end = jnp.cumsum(padded)
    pad_start = pad_end - padded
    eio = jnp.arange(N_EXPERTS, dtype=I32)
    start_of = jnp.sum(jnp.where(eidx[..., None] == eio, pad_start, 0), axis=-1)
    dest = (start_of + rank).reshape(-1)
    blk_start = jnp.arange(n_blocks, dtype=I32) * bm
    blk_expert = jnp.minimum(jnp.sum((blk_start[:, None] >= pad_end[None, :]).astype(I32), axis=-1),
                             N_EXPERTS - 1)
    inv = jnp.zeros((cap,), I32).at[dest].set(jnp.arange(1, n_assign + 1, dtype=I32))
    is_pad = inv == 0
    a = inv - 1
    tok_buf = jnp.where(is_pad, 0, a // TOP_K).astype(I32)
    pad_row = n_assign + jnp.cumsum(is_pad.astype(I32)) - 1
    dst_buf = jnp.where(is_pad, pad_row, (a % TOP_K) * T + a // TOP_K).astype(I32)
    return blk_expert, tok_buf, dst_buf, cap + 2 * bm


def _layer(x, c, l, w_ada, b_ada, g_pre_mix, g_post_mix, w_in, w_out, lam_q1, lam_k1, lam_q2, lam_k2,
           g_subln, nat_rpb, g_pre_ffn, g_post_ffn, w_router, b_router, w1, b1, w2, b2):
    B, L, D = x.shape
    lam_init = 0.8 - 0.6 * math.exp(-0.3 * l)
    mod, lam = _ada(c, w_ada, b_ada, lam_q1, lam_k1, lam_q2, lam_k2, lam_init)
    sh1, sc1, gt1, sh2, sc2, gt2 = jnp.split(mod, 6, axis=-1)

    qT, kd, vT, nq, nk, nv = _inproj(x, g_pre_mix, sc1, sh1, w_in)
    o_diff = _diff_attention(qT, kd, vT, lam, g_subln, lam_init)
    o_nat = _nat_attention(nq, nk, nv, nat_rpb)

    T = B * L
    x1, h2, eidx, gates, rank, counts = _out_router(
        o_diff.reshape(T, DIFF_WIDTH), o_nat.reshape(T, NAT_WIDTH), x.reshape(T, D), w_out,
        g_post_mix, gt1, g_pre_ffn, sc2, sh2, w_router, b_router, L)
    blk_expert, tok_buf, dst_buf, n_rows_out = _dispatch_plan(eidx, rank, counts)
    y_tok = _experts(h2, blk_expert, tok_buf, dst_buf, w1, b1, w2, b2, n_rows_out)
    out = _combine(y_tok, gates, x1, gt2, g_post_ffn, L)
    return out.reshape(B, L, D)


def kernel(x, c, w_ada, b_ada, g_pre_mix, g_post_mix, w_in, w_out, lam_q1, lam_k1, lam_q2, lam_k2,
           g_subln, nat_rpb, g_pre_ffn, g_post_ffn, w_router, b_router, w1, b1, w2, b2):
    depth = w_ada.shape[0]
    for l in range(depth):
        x = _layer(x, c, l, w_ada[l], b_ada[l], g_pre_mix[l], g_post_mix[l], w_in[l], w_out[l],
                   lam_q1[l], lam_k1[l], lam_q2[l], lam_k2[l], g_subln[l], nat_rpb[l],
                   g_pre_ffn[l], g_post_ffn[l], w_router[l], b_router[l], w1[l], b1[l], w2[l], b2[l])
    return x
```

```python
import functools
import math

import jax
import jax.numpy as jnp
from jax import lax
from jax.experimental import pallas as pl
from jax.experimental.pallas import tpu as pltpu

F32 = jnp.float32
BF16 = jnp.bfloat16
I32 = jnp.int32

HEAD_DIM = 64
N_DIFF_HEADS = 4
DIFF_HEAD_W = 2 * HEAD_DIM
DIFF_WIDTH = N_DIFF_HEADS * DIFF_HEAD_W
N_NAT_HEADS = 8
NAT_WIDTH = N_NAT_HEADS * HEAD_DIM
NAT_PAIRS = N_NAT_HEADS // 2
GRID_W = 64
NAT_KH = 8
NAT_KW = 16
N_EXPERTS = 32
TOP_K = 4
SWIGLU_LIMIT = 7.0
SWIGLU_ALPHA = 1.702
RMS_EPS = 1e-6
NEG_BIG = -1e30
LOG2E = 1.4426950408889634
ROW_TILE = 8

NT_DIMS = (((1,), (1,)), ((), ()))

ADA_TN = 1536
INPROJ_TM = 512
DIFF_TQ = 256
DIFF_TK = 512
DIFF_TICKS_PER_BODY = 3
NAT_GROUP = 8
NAT_KEY_ROWS = 16
OUT_TM = 512
MOE_BM = 256
COMBINE_TM = 512
VMEM_LIMIT = 48 * 1024 * 1024
EXPERT_VMEM_LIMIT = 56 * 1024 * 1024


def _rms(x, axis=-1):
    return x * lax.rsqrt(jnp.mean(x * x, axis=axis, keepdims=True) + RMS_EPS)


def _ada_kernel(lam_init, c_ref, w_ref, b_ref, lq1_ref, lk1_ref, lq2_ref, lk2_ref, mod_ref, lam_ref):
    c = c_ref[...]
    s = c * (1.0 / (1.0 + jnp.exp(-c)))
    mod_ref[...] = jnp.dot(s, w_ref[...], preferred_element_type=F32,
                           precision=lax.Precision.HIGHEST) + b_ref[...]
    d1 = jnp.sum(lq1_ref[...] * lk1_ref[...], axis=-1, keepdims=True)
    d2 = jnp.sum(lq2_ref[...] * lk2_ref[...], axis=-1, keepdims=True)
    lam = jnp.exp(d1) - jnp.exp(d2) + lam_init
    lam_ref[...] = jnp.broadcast_to(lam, lam_ref.shape)


def _ada(c, w_ada, b_ada, lq1, lk1, lq2, lk2, lam_init):
    B, D = c.shape
    N = w_ada.shape[1]
    c8 = jnp.zeros((8, D), F32).at[:B].set(c)
    vec = pl.BlockSpec((1, HEAD_DIM), lambda j: (0, 0))
    mod, lam = pl.pallas_call(
        functools.partial(_ada_kernel, lam_init),
        grid=(N // ADA_TN,),
        in_specs=[pl.BlockSpec((8, D), lambda j: (0, 0)),
                  pl.BlockSpec((D, ADA_TN), lambda j: (0, j)),
                  pl.BlockSpec((1, ADA_TN), lambda j: (0, j)),
                  vec, vec, vec, vec],
        out_specs=[pl.BlockSpec((8, ADA_TN), lambda j: (0, j)),
                   pl.BlockSpec((8, 128), lambda j: (0, 0))],
        out_shape=[jax.ShapeDtypeStruct((8, N), F32), jax.ShapeDtypeStruct((8, 128), F32)],
        compiler_params=pltpu.CompilerParams(dimension_semantics=("arbitrary",),
                                             vmem_limit_bytes=VMEM_LIMIT),
        name="ada",
    )(c8, w_ada, b_ada.reshape(1, N), lq1.reshape(1, -1), lk1.reshape(1, -1),
      lq2.reshape(1, -1), lk2.reshape(1, -1))
    return mod[:B], lam


def _inproj_kernel(x_ref, g_ref, sc_ref, sh_ref, wqT_ref, wvT_ref, wn_ref,
                   qT_ref, k_ref, vT_ref, nq_ref, nk_ref, nv_ref):
    h = _rms(x_ref[0]) * g_ref[...]
    h = h * (1.0 + sc_ref[0]) + sh_ref[0]
    hb = h.astype(BF16)
    qT_ref[0] = lax.dot_general(wqT_ref[...], hb, NT_DIMS, preferred_element_type=F32).astype(BF16)
    vT_ref[0] = lax.dot_general(wvT_ref[...], hb, NT_DIMS, preferred_element_type=F32).astype(BF16)
    rest = jnp.dot(hb, wn_ref[...], preferred_element_type=F32).astype(BF16)
    k_ref[0] = rest[:, 0:512]
    nq_ref[0] = rest[:, 512:1024]
    nk_ref[0] = rest[:, 1024:1536]
    nv_ref[0] = rest[:, 1536:2048]


def _inproj(x, g_pre, sc1, sh1, w_in):
    B, L, D = x.shape
    tm = INPROJ_TM
    scale = HEAD_DIM ** -0.5
    wqT = (w_in[:, 0:512] * (scale * LOG2E)).T.astype(BF16)
    wvT = w_in[:, 1024:1536].T.astype(BF16)
    wn = jnp.concatenate([w_in[:, 512:1024], w_in[:, 1536:2048] * scale, w_in[:, 2048:3072]],
                         axis=1).astype(BF16)
    row_major = pl.BlockSpec((1, tm, 512), lambda b, i: (b, i, 0))
    col_major = pl.BlockSpec((1, 512, tm), lambda b, i: (b, 0, i))
    modv = pl.BlockSpec((1, 1, D), lambda b, i: (b, 0, 0))
    rm_shape = jax.ShapeDtypeStruct((B, L, 512), BF16)
    cm_shape = jax.ShapeDtypeStruct((B, 512, L), BF16)
    return pl.pallas_call(
        _inproj_kernel,
        grid=(B, L // tm),
        in_specs=[pl.BlockSpec((1, tm, D), lambda b, i: (b, i, 0)),
                  pl.BlockSpec((1, D), lambda b, i: (0, 0)),
                  modv, modv,
                  pl.BlockSpec((512, D), lambda b, i: (0, 0)),
                  pl.BlockSpec((512, D), lambda b, i: (0, 0)),
                  pl.BlockSpec((D, 2048), lambda b, i: (0, 0))],
        out_specs=[col_major, row_major, col_major, row_major, row_major, row_major],
        out_shape=[cm_shape, rm_shape, cm_shape, rm_shape, rm_shape, rm_shape],
        compiler_params=pltpu.CompilerParams(dimension_semantics=("arbitrary", "arbitrary"),
                                             vmem_limit_bytes=VMEM_LIMIT),
        name="inproj",
    )(x, g_pre.reshape(1, D), sc1.reshape(B, 1, D), sh1.reshape(B, 1, D), wqT, wvT, wn)


def _diff_kernel(lam_init, n_kchunks, coef_ref, lam_ref, qT_ref, k_ref, kaug_ref, vT_ref, g_ref, o_ref,
                 acc_ref, m_ref, l_ref, s0_ref, qd_ref, qm_ref, cq_ref, s_ref, p_ref, a_ref):
    tq = qT_ref.shape[2]
    tk = s0_ref.shape[0]
    h = pl.program_id(1)
    q0 = pl.program_id(2) * tq
    c_f = coef_ref[4 * h]
    c_parts = (coef_ref[4 * h + 1], coef_ref[4 * h + 2], coef_ref[4 * h + 3])
    c_diag = lax.div(q0, tk)

    qT = qT_ref[0]
    row = lax.broadcasted_iota(I32, qT.shape, 0)
    zero = jnp.zeros_like(qT)
    q_maps = (jnp.where(row < HEAD_DIM, qT, zero), jnp.where(row >= HEAD_DIM, qT, zero))
    aug = jnp.zeros(qT.shape, F32)
    for j in range(3):
        aug = jnp.where(row == j, 64.0 * c_parts[j], aug)
        aug = jnp.where(row == 3 + j, c_parts[j], aug)
    for mi in range(2):
        qd_ref[mi] = q_maps[mi]
        for sg, sign in enumerate((1.0, -1.0)):
            qm_ref[mi, sg, 0:DIFF_HEAD_W, :] = q_maps[mi]
            qm_ref[mi, sg, DIFF_HEAD_W:, :] = (sign * aug).astype(BF16)

    qpos = q0 + lax.broadcasted_iota(I32, (1, tq), 1)
    cq_ref[...] = c_f * qpos.astype(F32)
    kk = lax.broadcasted_iota(I32, (tk, tq), 0)
    qq = lax.broadcasted_iota(I32, (tk, tq), 1)
    s0_ref[...] = c_f * (kk - qq).astype(F32)
    acc_ref[...] = jnp.zeros_like(acc_ref)
    l_ref[...] = jnp.zeros_like(l_ref)
    m_ref[...] = jnp.full_like(m_ref, NEG_BIG)

    def chunk(i):
        j = i - 1
        c = j + (j >= c_diag).astype(I32)
        return c_diag if isinstance(i, int) and i == 0 else jnp.where(i == 0, c_diag, c)

    def start(c):
        return pl.multiple_of(c * tk, tk)

    def update(mi, slot, logits, shift):
        m_old = m_ref[mi]
        m_new = jnp.maximum(m_old, jnp.max(logits, axis=0, keepdims=True) - shift)
        alpha = jnp.exp2(m_old - m_new)
        p = jnp.exp2(logits - (m_new + shift))
        l_ref[mi] = alpha * l_ref[mi] + jnp.sum(p, axis=0, keepdims=True)
        m_ref[mi] = m_new
        a_ref[slot, mi] = alpha
        p_ref[slot, mi] = p.astype(BF16)

    def scores_diag(slot):
        kb = k_ref[0, pl.ds(start(c_diag), tk), :]
        for mi in range(2):
            s_ref[slot, mi] = jnp.dot(kb, qd_ref[mi], preferred_element_type=F32)

    def softmax_diag(slot):
        bias = jnp.abs(s0_ref[...] + c_f * (start(c_diag) - q0).astype(F32))
        for mi in range(2):
            update(mi, slot, s_ref[slot, mi] - bias, 0.0)

    def scores(c, slot):
        k0 = start(c)
        kb = jnp.concatenate([k_ref[0, pl.ds(k0, tk), :], kaug_ref[pl.ds(k0, tk), :]], axis=1)
        sg = (c > c_diag).astype(I32)
        for mi in range(2):
            s_ref[slot, mi] = jnp.dot(kb, qm_ref[mi, sg], preferred_element_type=F32)

    def softmax(c, slot):
        shift = jnp.where(c > c_diag, -1.0, 1.0) * cq_ref[...]
        for mi in range(2):
            update(mi, slot, s_ref[slot, mi], shift)

    def values(c, slot):
        vb = vT_ref[0, :, pl.ds(start(c), tk)]
        for mi in range(2):
            acc_ref[mi] = a_ref[slot, mi] * acc_ref[mi] + jnp.dot(vb, p_ref[slot, mi],
                                                                  preferred_element_type=F32)

    n = n_kchunks
    scores_diag(0)
    scores(chunk(1), 1)
    scores(chunk(2), 2)
    softmax_diag(0)
    scores(chunk(3), 0)
    softmax(chunk(1), 1)

    def ticks(j, carry):
        t0 = DIFF_TICKS_PER_BODY * j + 2
        for k in range(DIFF_TICKS_PER_BODY):
            t = t0 + k
            scores(chunk(t + 2), (k + 1) % 3)
            softmax(chunk(t), (k + 2) % 3)
            values(chunk(t - 2), k % 3)
        return carry

    lax.fori_loop(0, (n - 4) // DIFF_TICKS_PER_BODY, ticks, 0)
    for t in (n - 2, n - 1):
        softmax(chunk(t), t % 3)
        values(chunk(t - 2), (t - 2) % 3)
    values(chunk(n - 2), (n - 2) % 3)
    values(chunk(n - 1), (n - 1) % 3)

    lam = lam_ref[0:1, 0:1]
    o = acc_ref[0] / l_ref[0] - lam * (acc_ref[1] / l_ref[1])
    y = _rms(o, axis=0) * g_ref[...] * (1.0 - lam_init)
    o_ref[0] = y.T.astype(BF16)


def _diff_attention(qT, k, vT, lam, g_sub, lam_init):
    B, _, L = qT.shape
    tq, tk = min(DIFF_TQ, L), min(DIFF_TK, L)
    assert DIFF_TICKS_PER_BODY % 3 == 0 and L // tk >= 4 and (L // tk - 4) % DIFF_TICKS_PER_BODY == 0
    assert tk % tq == 0 and L <= 64 * 256
    i = jnp.arange(1, N_DIFF_HEADS + 1, dtype=F32)
    c = jnp.exp2(-8.0 * i / N_DIFF_HEADS) * LOG2E
    c1 = c.astype(BF16).astype(F32)
    c2 = (c - c1).astype(BF16).astype(F32)
    c3 = (c - c1 - c2).astype(BF16).astype(F32)
    coef = jnp.stack([c, c1, c2, c3], axis=1).reshape(-1)
    kpos = jnp.arange(L, dtype=I32)
    lane = jnp.arange(DIFF_HEAD_W, dtype=I32)
    kaug = jnp.where(lane[None, :] < 3, (kpos // 64)[:, None],
                     jnp.where(lane[None, :] < 6, (kpos % 64)[:, None], 0)).astype(BF16)
    return pl.pallas_call(
        functools.partial(_diff_kernel, lam_init, L // tk),
        grid=(B, N_DIFF_HEADS, L // tq),
        in_specs=[pl.BlockSpec(memory_space=pltpu.SMEM),
                  pl.BlockSpec((8, 128), lambda b, h, i: (0, 0)),
                  pl.BlockSpec((1, DIFF_HEAD_W, tq), lambda b, h, i: (b, h, i)),
                  pl.BlockSpec((1, L, DIFF_HEAD_W), lambda b, h, i: (b, 0, h)),
                  pl.BlockSpec((L, DIFF_HEAD_W), lambda b, h, i: (0, 0)),
                  pl.BlockSpec((1, DIFF_HEAD_W, L), lambda b, h, i: (b, h, 0)),
                  pl.BlockSpec((DIFF_HEAD_W, 1), lambda b, h, i: (0, 0))],
        out_specs=pl.BlockSpec((1, tq, DIFF_HEAD_W), lambda b, h, i: (b, i, h)),
        out_shape=jax.ShapeDtypeStruct((B, L, DIFF_WIDTH), BF16),
        scratch_shapes=[pltpu.VMEM((2, DIFF_HEAD_W, tq), F32),
                        pltpu.VMEM((2, 1, tq), F32),
                        pltpu.VMEM((2, 1, tq), F32),
                        pltpu.VMEM((tk, tq), F32),
                        pltpu.VMEM((2, DIFF_HEAD_W, tq), BF16),
                        pltpu.VMEM((2, 2, 2 * DIFF_HEAD_W, tq), BF16),
                        pltpu.VMEM((1, tq), F32),
                        pltpu.VMEM((3, 2, tk, tq), F32),
                        pltpu.VMEM((3, 2, tk, tq), BF16),
                        pltpu.VMEM((3, 2, 1, tq), F32)],
        compiler_params=pltpu.CompilerParams(
            dimension_semantics=("arbitrary", "arbitrary", "arbitrary"),
            vmem_limit_bytes=VMEM_LIMIT),
        name="diff_attn",
    )(coef, lam, qT, k, kaug, vT, g_sub.reshape(DIFF_HEAD_W, 1))


def _nat_group_span(g, n_rows):
    r0 = g * NAT_GROUP
    lo, hi = 0, n_rows - NAT_KEY_ROWS
    kb0 = r0 - NAT_KH // 2
    kb0 = min(max(kb0, lo), hi) if isinstance(g, int) else jnp.clip(kb0, lo, hi)
    return r0, kb0


def _nat_bias_table(rpb, n_rows):
    c = jnp.arange(GRID_W)
    col_start = jnp.clip(c - NAT_KW // 2, 0, GRID_W - NAT_KW)
    col_in = (c[None, :] >= col_start[:, None]) & (c[None, :] < col_start[:, None] + NAT_KW)
    dc = jnp.clip(c[None, :] - c[:, None], -(NAT_KW - 1), NAT_KW - 1) + (NAT_KW - 1)
    rpb = rpb.astype(F32)
    cols = jnp.zeros(rpb.shape[:2] + (GRID_W, GRID_W), F32)
    for j in range(2 * NAT_KW - 1):
        cols = cols + jnp.where(dc == j, rpb[:, :, j][:, :, None, None], 0.0)
    cols = jnp.where(col_in[None, None], cols, NEG_BIG)
    cols = cols.transpose(0, 2, 1, 3)
    outside = jnp.full(cols.shape[:2] + (1, GRID_W), NEG_BIG, F32)
    n_groups = n_rows // NAT_GROUP
    kinds = []
    for g in (0, 1, n_groups - 1):
        r0, kb0 = _nat_group_span(min(g, n_groups - 1), n_rows)
        per_row = []
        for j in range(NAT_GROUP):
            r = r0 + j
            rs = min(max(r - NAT_KH // 2, 0), n_rows - NAT_KH)
            pieces = []
            for kr in range(NAT_KEY_ROWS):
                krow = kb0 + kr
                dr = krow - r + (NAT_KH - 1)
                pieces.append(cols[:, :, dr:dr + 1] if rs <= krow < rs + NAT_KH else outside)
            per_row.append(jnp.concatenate(pieces, axis=2))
        kinds.append(jnp.stack(per_row, axis=1))
    tbl = jnp.stack(kinds, axis=1)
    tbl = tbl.reshape(NAT_PAIRS, 2, 3, NAT_GROUP * GRID_W, NAT_KEY_ROWS * GRID_W)
    return tbl.transpose(0, 2, 1, 3, 4)


def _nat_kernel(n_rows, q_ref, k_ref, v_ref, bias_ref, o_ref):
    _, kb0 = _nat_group_span(pl.program_id(2), n_rows)
    k0 = pl.multiple_of(kb0 * GRID_W, GRID_W)
    win = NAT_KEY_ROWS * GRID_W
    kw = k_ref[0, pl.ds(k0, win), :]
    vw = v_ref[0, pl.ds(k0, win), :]
    q = q_ref[0]
    lane = lax.broadcasted_iota(I32, q.shape, 1)
    zero = jnp.zeros_like(q)
    outs = []
    for hh in range(2):
        keep = (lane < HEAD_DIM) if hh == 0 else (lane >= HEAD_DIM)
        s = lax.dot_general(jnp.where(keep, q, zero), kw, NT_DIMS, preferred_element_type=F32)
        s = s + bias_ref[0, 0, hh]
        p = jnp.exp(s - jnp.max(s, axis=-1, keepdims=True))
        l = jnp.sum(p, axis=-1, keepdims=True)
        outs.append(jnp.dot(p.astype(BF16), vw, preferred_element_type=F32) / l)
    o_ref[0] = jnp.where(lane < HEAD_DIM, outs[0], outs[1]).astype(BF16)


def _nat_attention(nq, nk, nv, rpb):
    B, L, _ = nq.shape
    n_rows = L // GRID_W
    assert n_rows >= NAT_KEY_ROWS and n_rows % NAT_GROUP == 0
    n_groups = n_rows // NAT_GROUP
    tbl = _nat_bias_table(rpb, n_rows)
    kv = pl.BlockSpec((1, L, 2 * HEAD_DIM), lambda b, p, i: (b, 0, p))
    qo = pl.BlockSpec((1, NAT_GROUP * GRID_W, 2 * HEAD_DIM), lambda b, p, i: (b, i, p))
    kind = lambda i: jnp.where(i == 0, 0, jnp.where(i == n_groups - 1, 2, 1))
    return pl.pallas_call(
        functools.partial(_nat_kernel, n_rows),
        grid=(B, NAT_PAIRS, n_groups),
        in_specs=[qo, kv, kv,
                  pl.BlockSpec((1, 1, 2, NAT_GROUP * GRID_W, NAT_KEY_ROWS * GRID_W),
                               lambda b, p, i: (p, kind(i), 0, 0, 0))],
        out_specs=qo,
        out_shape=jax.ShapeDtypeStruct((B, L, NAT_WIDTH), BF16),
        compiler_params=pltpu.CompilerParams(
            dimension_semantics=("arbitrary", "arbitrary", "arbitrary"),
            vmem_limit_bytes=VMEM_LIMIT),
        name="nat_attn",
    )(nq, nk, nv, tbl)


def _out_kernel(od_ref, on_ref, x_ref, wt_ref, wb_ref, gpost_ref, gt1_ref, gpre_ref, sc2_ref, sh2_ref,
                wr_ref, br_ref, x1_ref, h2_ref, eidx_ref, gate_ref, rank_ref, cnt_ref, carry_ref):
    tm = x_ref.shape[0]

    @pl.when(pl.program_id(0) == 0)
    def _():
        carry_ref[...] = jnp.zeros_like(carry_ref)

    mix = (jnp.dot(od_ref[...], wt_ref[...], preferred_element_type=F32)
           + jnp.dot(on_ref[...], wb_ref[...], preferred_element_type=F32))
    x1 = x_ref[...] + gt1_ref[0] * (_rms(mix) * gpost_ref[...])
    x1_ref[...] = x1
    h2 = _rms(x1) * gpre_ref[...]
    h2 = h2 * (1.0 + sc2_ref[0]) + sh2_ref[0]
    for s in range(ROW_TILE):
        h2_ref[pl.ds(s, tm, stride=ROW_TILE), :] = h2[:, s * 128:(s + 1) * 128]

    wr = wr_ref[...]
    h_hi = h2.astype(BF16)
    h_lo = (h2 - h_hi.astype(F32)).astype(BF16)
    w_hi = wr.astype(BF16)
    w_lo = (wr - w_hi.astype(F32)).astype(BF16)
    logits = (jnp.dot(h_hi, w_hi, preferred_element_type=F32)
              + (jnp.dot(h_hi, w_lo, preferred_element_type=F32)
                 + jnp.dot(h_lo, w_hi, preferred_element_type=F32))) + br_ref[...]
    eio = lax.broadcasted_iota(I32, logits.shape, 1).astype(F32)
    onehot = jnp.zeros_like(logits)
    vals, idxs, sels = [], [], []
    cur = logits
    for _ in range(TOP_K):
        mx = jnp.max(cur, axis=-1, keepdims=True)
        idx = jnp.min(jnp.where(cur == mx, eio, float(N_EXPERTS)), axis=-1, keepdims=True)
        sel = eio == idx
        vals.append(mx)
        idxs.append(idx)
        sels.append(sel)
        cur = jnp.where(sel, -jnp.inf, cur)
        onehot = onehot + sel.astype(F32)

    ex = [jnp.exp(v - vals[0]) for v in vals]
    tot = ex[0] + ex[1] + ex[2] + ex[3]

    rr = lax.broadcasted_iota(I32, (tm, tm), 0)
    cc = lax.broadcasted_iota(I32, (tm, tm), 1)
    tri = (rr > cc).astype(BF16)
    carry = carry_ref[...]
    cum = jnp.dot(tri, onehot.astype(BF16), preferred_element_type=F32) + carry
    ranks = [jnp.sum(jnp.where(sel, cum, 0.0), axis=-1, keepdims=True) for sel in sels]
    carry = carry + jnp.sum(onehot, axis=0, keepdims=True)
    carry_ref[...] = carry
    cnt_ref[...] = carry

    kio = lax.broadcasted_iota(I32, (tm, TOP_K), 1)

    def pack(cols):
        out = jnp.broadcast_to(cols[TOP_K - 1], (tm, TOP_K))
        for k in range(TOP_K - 2, -1, -1):
            out = jnp.where(kio == k, cols[k], out)
        return out

    eidx_ref[...] = pack(idxs).astype(I32)
    gate_ref[...] = pack([e / tot for e in ex])
    rank_ref[...] = pack(ranks).astype(I32)


def _out_router(od, on, x, w_out, g_post, gt1, g_pre, sc2, sh2, w_router, b_router, tokens_per_batch):
    T, D = x.shape
    B = gt1.shape[0]
    tm = min(OUT_TM, tokens_per_batch)
    steps_per_batch = tokens_per_batch // tm
    wt = w_out[:DIFF_WIDTH].astype(BF16)
    wb = w_out[DIFF_WIDTH:].astype(BF16)
    rowblk = lambda w: pl.BlockSpec((tm, w), lambda i: (i, 0))
    const = lambda shape: pl.BlockSpec(shape, lambda i: (0,) * len(shape))
    modv = pl.BlockSpec((1, 1, D), lambda i: (i // steps_per_batch, 0, 0))
    return pl.pallas_call(
        _out_kernel,
        grid=(T // tm,),
        in_specs=[rowblk(DIFF_WIDTH), rowblk(NAT_WIDTH), rowblk(D),
                  const((DIFF_WIDTH, D)), const((NAT_WIDTH, D)), const((1, D)), modv, const((1, D)),
                  modv, modv, const((D, N_EXPERTS)), const((1, N_EXPERTS))],
        out_specs=[rowblk(D), pl.BlockSpec((tm * ROW_TILE, 128), lambda i: (i, 0)),
                   rowblk(TOP_K), rowblk(TOP_K), rowblk(TOP_K), const((1, N_EXPERTS))],
        out_shape=[jax.ShapeDtypeStruct((T, D), F32), jax.ShapeDtypeStruct((T * ROW_TILE, 128), F32),
                   jax.ShapeDtypeStruct((T, TOP_K), I32), jax.ShapeDtypeStruct((T, TOP_K), F32),
                   jax.ShapeDtypeStruct((T, TOP_K), I32), jax.ShapeDtypeStruct((1, N_EXPERTS), F32)],
        scratch_shapes=[pltpu.VMEM((1, N_EXPERTS), F32)],
        compiler_params=pltpu.CompilerParams(dimension_semantics=("arbitrary",),
                                             vmem_limit_bytes=VMEM_LIMIT),
        name="out_router",
    )(od, on, x, wt, wb, g_post.reshape(1, D), gt1.reshape(B, 1, D), g_pre.reshape(1, D),
      sc2.reshape(B, 1, D), sh2.reshape(B, 1, D), w_router, b_router.reshape(1, N_EXPERTS))


def _expert_kernel(spare_base, be_ref, tok_ref, tok_next_ref, dst_prev_ref, dst_ref, h2_hbm, w1_ref, b1g_ref, b1l_ref,
                   w2_ref, b2_ref, y_hbm, xbuf, ybuf, xb, w1t, w2t, hh, gsem, ssem):
    i = pl.program_id(0)
    n = pl.num_programs(0)
    bm = xbuf.shape[1] // ROW_TILE

    def tile_rows(t):
        t = t * ROW_TILE
        return pl.ds(t if isinstance(t, int) else pl.multiple_of(t, ROW_TILE), ROW_TILE)
    ff = w2_ref.shape[1]
    slot = i % 2

    def gather_row(idx_ref, s, r):
        return pltpu.make_async_copy(h2_hbm.at[tile_rows(idx_ref[0, 0, r]), :],
                                     xbuf.at[s, tile_rows(r), :], gsem.at[s])

    def scatter_row(idx_ref, s, r):
        return pltpu.make_async_copy(ybuf.at[s, tile_rows(r), :],
                                     y_hbm.at[tile_rows(idx_ref[0, 0, r]), :], ssem.at[s])

    def wait_gather(s):
        pltpu.make_async_copy(h2_hbm.at[pl.ds(0, bm * ROW_TILE), :], xbuf.at[s], gsem.at[s]).wait()

    def wait_scatter(s):
        pltpu.make_async_copy(ybuf.at[s], y_hbm.at[pl.ds(0, bm * ROW_TILE), :], ssem.at[s]).wait()

    @pl.when(i == 0)
    def _():
        ybuf[...] = jnp.zeros(ybuf.shape, F32)
        for r in range(bm):
            pltpu.make_async_copy(ybuf.at[0, tile_rows(r), :], y_hbm.at[tile_rows(spare_base + r), :],
                                  ssem.at[0]).start(priority=r % 2)
        for r in range(bm):
            gather_row(tok_ref, 0, r).start(priority=r % 2)

    @pl.when((i == 0) | (be_ref[i] != be_ref[jnp.maximum(i - 1, 0)]))
    def _():
        cw = 256
        for c in range(w1_ref.shape[2] // cw):
            w1t[c * cw:(c + 1) * cw, :] = w1_ref[0, :, c * cw:(c + 1) * cw].T.astype(BF16)
        for c in range(w2_ref.shape[2] // cw):
            w2t[c * cw:(c + 1) * cw, :] = w2_ref[0, :, c * cw:(c + 1) * cw].T.astype(BF16)

    wait_gather(slot)
    for s8 in range(ROW_TILE):
        xb[:, s8 * 128:(s8 + 1) * 128] = xbuf[slot, pl.ds(s8, bm, stride=ROW_TILE), :].astype(BF16)

    n_lane_tiles = bm // 128

    @pl.when(i >= 0)
    def _():
        for r in range(bm):
            gather_row(tok_next_ref, 1 - slot, r).start(priority=r % 2)
        hh_t = lax.dot_general(w1t[...], xb[...], NT_DIMS, preferred_element_type=F32)
        for j in range(n_lane_tiles):
            hh[j] = hh_t[:, j * 128:(j + 1) * 128]

    for r in range(bm):
        scatter_row(dst_prev_ref, 1 - slot, r).start(priority=r % 2)
    even = jnp.concatenate([hh[j, pl.ds(0, ff, stride=2), :] for j in range(n_lane_tiles)], axis=1)
    odd = jnp.concatenate([hh[j, pl.ds(1, ff, stride=2), :] for j in range(n_lane_tiles)], axis=1)
    glu = jnp.minimum(even + b1g_ref[0], SWIGLU_LIMIT)
    lin = jnp.clip(odd + b1l_ref[0], -SWIGLU_LIMIT, SWIGLU_LIMIT)
    act = glu * (1.0 / (1.0 + jnp.exp(-SWIGLU_ALPHA * glu))) * (lin + 1.0)
    y_t = jnp.dot(w2t[...], act.astype(BF16), preferred_element_type=F32)

    y = y_t.T + b2_ref[0]
    wait_scatter(slot)
    for s8 in range(ROW_TILE):
        ybuf[slot, pl.ds(s8, bm, stride=ROW_TILE), :] = y[:, s8 * 128:(s8 + 1) * 128]

    @pl.when(i == n - 1)
    def _():
        wait_scatter(1 - slot)
        for r in range(bm):
            scatter_row(dst_ref, slot, r).start(priority=r % 2)
        wait_scatter(slot)
        wait_gather(1 - slot)


def _experts(h2, blk_expert, tok_buf, dst_buf, w1, b1, w2, b2, n_rows_out):
    D = w1.shape[1]
    assert D == ROW_TILE * 128 and h2.shape[1] == 128
    n_blocks = blk_expert.shape[0]
    bm = MOE_BM
    F = w2.shape[1]
    b1g = b1[:, 0::2].reshape(N_EXPERTS, F, 1)
    b1l = b1[:, 1::2].reshape(N_EXPERTS, F, 1)
    tok3 = tok_buf.reshape(n_blocks, 1, bm)
    spare = n_rows_out - 2 * bm + jnp.arange(bm, dtype=I32)
    dst3 = jnp.concatenate([spare, dst_buf]).reshape(n_blocks + 1, 1, bm)
    smem_blk = lambda fn: pl.BlockSpec((1, 1, bm), fn, memory_space=pltpu.SMEM)
    grid_spec = pltpu.PrefetchScalarGridSpec(
        num_scalar_prefetch=1,
        grid=(n_blocks,),
        in_specs=[smem_blk(lambda i, be: (i, 0, 0)),
                  smem_blk(lambda i, be: (jnp.minimum(i + 1, n_blocks - 1), 0, 0)),
                  smem_blk(lambda i, be: (i, 0, 0)),
                  smem_blk(lambda i, be: (i + 1, 0, 0)),
                  pl.BlockSpec(memory_space=pl.ANY),
                  pl.BlockSpec((1, D, 2 * F), lambda i, be: (be[i], 0, 0)),
                  pl.BlockSpec((1, F, 1), lambda i, be: (be[i], 0, 0)),
                  pl.BlockSpec((1, F, 1), lambda i, be: (be[i], 0, 0)),
                  pl.BlockSpec((1, F, D), lambda i, be: (be[i], 0, 0)),
                  pl.BlockSpec((1, 1, D), lambda i, be: (be[i], 0, 0))],
        out_specs=pl.BlockSpec(memory_space=pl.ANY),
        scratch_shapes=[pltpu.VMEM((2, bm * ROW_TILE, 128), F32), pltpu.VMEM((2, bm * ROW_TILE, 128), F32),
                        pltpu.VMEM((bm, D), BF16),
                        pltpu.VMEM((2 * F, D), BF16), pltpu.VMEM((D, F), BF16),
                        pltpu.VMEM((bm // 128, 2 * F, 128), F32),
                        pltpu.SemaphoreType.DMA((2,)), pltpu.SemaphoreType.DMA((2,))],
    )
    return pl.pallas_call(
        functools.partial(_expert_kernel, n_rows_out - bm),
        grid_spec=grid_spec,
        out_shape=jax.ShapeDtypeStruct((n_rows_out * ROW_TILE, 128), F32),
        compiler_params=pltpu.CompilerParams(dimension_semantics=("arbitrary",),
                                             vmem_limit_bytes=EXPERT_VMEM_LIMIT),
        name="experts",
    )(blk_expert, tok3, tok3, dst3, dst3, h2, w1, b1g, b1l, w2, b2.reshape(N_EXPERTS, 1, D))


def _combine_kernel(y0_ref, y1_ref, y2_ref, y3_ref, gate_ref, x1_ref, gt2_ref, g_ref, o_ref):
    tm = x1_ref.shape[0]
    gates = gate_ref[...]

    def rows(y_ref):
        return jnp.concatenate([y_ref[pl.ds(s, tm, stride=ROW_TILE), :] for s in range(ROW_TILE)], axis=1)

    f = gates[:, 0:1] * rows(y0_ref)
    for k, y_ref in enumerate((y1_ref, y2_ref, y3_ref), start=1):
        f = f + gates[:, k:k + 1] * rows(y_ref)
    o_ref[...] = x1_ref[...] + gt2_ref[0] * (_rms(f) * g_ref[...])


def _combine(y_tok, gates, x1, gt2, g_post, tokens_per_batch):
    T, D = x1.shape
    B = gt2.shape[0]
    tm = min(COMBINE_TM, tokens_per_batch)
    steps_per_batch = tokens_per_batch // tm
    steps = T // tm
    y_spec = lambda k: pl.BlockSpec((tm * ROW_TILE, 128), lambda i: (k * steps + i, 0))
    return pl.pallas_call(
        _combine_kernel,
        grid=(steps,),
        in_specs=[y_spec(0), y_spec(1), y_spec(2), y_spec(3),
                  pl.BlockSpec((tm, TOP_K), lambda i: (i, 0)),
                  pl.BlockSpec((tm, D), lambda i: (i, 0)),
                  pl.BlockSpec((1, 1, D), lambda i: (i // steps_per_batch, 0, 0)),
                  pl.BlockSpec((1, D), lambda i: (0, 0))],
        out_specs=pl.BlockSpec((tm, D), lambda i: (i, 0)),
        out_shape=jax.ShapeDtypeStruct((T, D), F32),
        compiler_params=pltpu.CompilerParams(dimension_semantics=("arbitrary",),
                                             vmem_limit_bytes=VMEM_LIMIT),
        name="combine",
    )(y_tok, y_tok, y_tok, y_tok, gates, x1, gt2.reshape(B, 1, D), g_post.reshape(1, D))


def _dispatch_plan(eidx, rank, counts):
    T = eidx.shape[0]
    n_assign = eidx.size
    bm = MOE_BM
    cap = (n_assign + N_EXPERTS * (bm - 1) + bm - 1) // bm * bm
    n_blocks = cap // bm
    counts = counts.reshape(N_EXPERTS).astype(I32)
    padded = (counts + bm - 1) // bm * bm
    pad_end = jnp.cumsum(padded)
    pad_start = pad_end - padded
    eio = jnp.arange(N_EXPERTS, dtype=I32)
    start_of = jnp.sum(jnp.where(eidx[..., None] == eio, pad_start, 0), axis=-1)
    dest = (start_of + rank).reshape(-1)
    blk_start = jnp.arange(n_blocks, dtype=I32) * bm
    blk_expert = jnp.minimum(jnp.sum((blk_start[:, None] >= pad_end[None, :]).astype(I32), axis=-1),
                             N_EXPERTS - 1)
    inv = jnp.zeros((cap,), I32).at[dest].set(jnp.arange(1, n_assign + 1, dtype=I32))
    is_pad = inv == 0
    a = inv - 1
    tok_buf = jnp.where(is_pad, 0, a // TOP_K).astype(I32)
    pad_row = n_assign + jnp.cumsum(is_pad.astype(I32)) - 1
    dst_buf = jnp.where(is_pad, pad_row, (a % TOP_K) * T + a // TOP_K).astype(I32)
    return blk_expert, tok_buf, dst_buf, cap + 2 * bm


def _layer(x, c, l, w_ada, b_ada, g_pre_mix, g_post_mix, w_in, w_out, lam_q1, lam_k1, lam_q2, lam_k2,
           g_subln, nat_rpb, g_pre_ffn, g_post_ffn, w_router, b_router, w1, b1, w2, b2):
    B, L, D = x.shape
    lam_init = 0.8 - 0.6 * math.exp(-0.3 * l)
    mod, lam = _ada(c, w_ada, b_ada, lam_q1, lam_k1, lam_q2, lam_k2, lam_init)
    sh1, sc1, gt1, sh2, sc2, gt2 = jnp.split(mod, 6, axis=-1)

    qT, kd, vT, nq, nk, nv = _inproj(x, g_pre_mix, sc1, sh1, w_in)
    o_diff = _diff_attention(qT, kd, vT, lam, g_subln, lam_init)
    o_nat = _nat_attention(nq, nk, nv, nat_rpb)

    T = B * L
    x1, h2, eidx, gates, rank, counts = _out_router(
        o_diff.reshape(T, DIFF_WIDTH), o_nat.reshape(T, NAT_WIDTH), x.reshape(T, D), w_out,
        g_post_mix, gt1, g_pre_ffn, sc2, sh2, w_router, b_router, L)
    blk_expert, tok_buf, dst_buf, n_rows_out = _dispatch_plan(eidx, rank, counts)
    y_tok = _experts(h2, blk_expert, tok_buf, dst_buf, w1, b1, w2, b2, n_rows_out)
    out = _combine(y_tok, gates, x1, gt2, g_post_ffn, L)
    return out.reshape(B, L, D)


def kernel(x, c, w_ada, b_ada, g_pre_mix, g_post_mix, w_in, w_out, lam_q1, lam_k1, lam_q2, lam_k2,
           g_subln, nat_rpb, g_pre_ffn, g_post_ffn, w_router, b_router, w1, b1, w2, b2):
    depth = w_ada.shape[0]
    for l in range(depth):
        x = _layer(x, c, l, w_ada[l], b_ada[l], g_pre_mix[l], g_post_mix[l], w_in[l], w_out[l],
                   lam_q1[l], lam_k1[l], lam_q2[l], lam_k2[l], g_subln[l], nat_rpb[l],
                   g_pre_ffn[l], g_post_ffn[l], w_router[l], b_router[l], w1[l], b1[l], w2[l], b2[l])
    return x
```

```python
import functools
import math

import jax
import jax.numpy as jnp
from jax import lax
from jax.experimental import pallas as pl
from jax.experimental.pallas import tpu as pltpu

F32 = jnp.float32
BF16 = jnp.bfloat16
I32 = jnp.int32

HEAD_DIM = 64
N_DIFF_HEADS = 4
DIFF_HEAD_W = 2 * HEAD_DIM
DIFF_WIDTH = N_DIFF_HEADS * DIFF_HEAD_W
N_NAT_HEADS = 8
NAT_WIDTH = N_NAT_HEADS * HEAD_DIM
NAT_PAIRS = N_NAT_HEADS // 2
GRID_W = 64
NAT_KH = 8
NAT_KW = 16
N_EXPERTS = 32
TOP_K = 4
SWIGLU_LIMIT = 7.0
SWIGLU_ALPHA = 1.702
RMS_EPS = 1e-6
NEG_BIG = -1e30
LOG2E = 1.4426950408889634
ROW_TILE = 8

NT_DIMS = (((1,), (1,)), ((), ()))

ADA_TN = 1536
INPROJ_TM = 512
DIFF_TQ = 256
DIFF_TK = 512
DIFF_TICKS_PER_BODY = 3
NAT_GROUP = 8
NAT_KEY_ROWS = 16
OUT_TM = 512
MOE_BM = 256
COMBINE_TM = 512
INVERSE_CHUNK = 4096
VMEM_LIMIT = 48 * 1024 * 1024
EXPERT_VMEM_LIMIT = 56 * 1024 * 1024


def _rms(x, axis=-1):
    return x * lax.rsqrt(jnp.mean(x * x, axis=axis, keepdims=True) + RMS_EPS)


def _ada_kernel(lam_init, c_ref, w_ref, b_ref, lq1_ref, lk1_ref, lq2_ref, lk2_ref, mod_ref, lam_ref):
    c = c_ref[...]
    s = c * (1.0 / (1.0 + jnp.exp(-c)))
    mod_ref[...] = jnp.dot(s, w_ref[...], preferred_element_type=F32,
                           precision=lax.Precision.HIGHEST) + b_ref[...]
    d1 = jnp.sum(lq1_ref[...] * lk1_ref[...], axis=-1, keepdims=True)
    d2 = jnp.sum(lq2_ref[...] * lk2_ref[...], axis=-1, keepdims=True)
    lam = jnp.exp(d1) - jnp.exp(d2) + lam_init
    lam_ref[...] = jnp.broadcast_to(lam, lam_ref.shape)


def _ada(c, w_ada, b_ada, lq1, lk1, lq2, lk2, lam_init):
    B, D = c.shape
    N = w_ada.shape[1]
    c8 = jnp.zeros((8, D), F32).at[:B].set(c)
    vec = pl.BlockSpec((1, HEAD_DIM), lambda j: (0, 0))
    mod, lam = pl.pallas_call(
        functools.partial(_ada_kernel, lam_init),
        grid=(N // ADA_TN,),
        in_specs=[pl.BlockSpec((8, D), lambda j: (0, 0)),
                  pl.BlockSpec((D, ADA_TN), lambda j: (0, j)),
                  pl.BlockSpec((1, ADA_TN), lambda j: (0, j)),
                  vec, vec, vec, vec],
        out_specs=[pl.BlockSpec((8, ADA_TN), lambda j: (0, j)),
                   pl.BlockSpec((8, 128), lambda j: (0, 0))],
        out_shape=[jax.ShapeDtypeStruct((8, N), F32), jax.ShapeDtypeStruct((8, 128), F32)],
        compiler_params=pltpu.CompilerParams(dimension_semantics=("arbitrary",),
                                             vmem_limit_bytes=VMEM_LIMIT),
        name="ada",
    )(c8, w_ada, b_ada.reshape(1, N), lq1.reshape(1, -1), lk1.reshape(1, -1),
      lq2.reshape(1, -1), lk2.reshape(1, -1))
    return mod[:B], lam


def _inproj_kernel(x_ref, g_ref, sc_ref, sh_ref, wqT_ref, wvT_ref, wn_ref,
                   qT_ref, k_ref, vT_ref, nq_ref, nk_ref, nv_ref):
    h = _rms(x_ref[0]) * g_ref[...]
    h = h * (1.0 + sc_ref[0]) + sh_ref[0]
    hb = h.astype(BF16)
    qT_ref[0] = lax.dot_general(wqT_ref[...], hb, NT_DIMS, preferred_element_type=F32).astype(BF16)
    vT_ref[0] = lax.dot_general(wvT_ref[...], hb, NT_DIMS, preferred_element_type=F32).astype(BF16)
    rest = jnp.dot(hb, wn_ref[...], preferred_element_type=F32).astype(BF16)
    k_ref[0] = rest[:, 0:512]
    nq_ref[0] = rest[:, 512:1024]
    nk_ref[0] = rest[:, 1024:1536]
    nv_ref[0] = rest[:, 1536:2048]


def _inproj(x, g_pre, sc1, sh1, w_in):
    B, L, D = x.shape
    tm = INPROJ_TM
    scale = HEAD_DIM ** -0.5
    wqT = (w_in[:, 0:512] * (scale * LOG2E)).T.astype(BF16)
    wvT = w_in[:, 1024:1536].T.astype(BF16)
    wn = jnp.concatenate([w_in[:, 512:1024], w_in[:, 1536:2048] * scale, w_in[:, 2048:3072]],
                         axis=1).astype(BF16)
    row_major = pl.BlockSpec((1, tm, 512), lambda b, i: (b, i, 0))
    col_major = pl.BlockSpec((1, 512, tm), lambda b, i: (b, 0, i))
    modv = pl.BlockSpec((1, 1, D), lambda b, i: (b, 0, 0))
    rm_shape = jax.ShapeDtypeStruct((B, L, 512), BF16)
    cm_shape = jax.ShapeDtypeStruct((B, 512, L), BF16)
    return pl.pallas_call(
        _inproj_kernel,
        grid=(B, L // tm),
        in_specs=[pl.BlockSpec((1, tm, D), lambda b, i: (b, i, 0)),
                  pl.BlockSpec((1, D), lambda b, i: (0, 0)),
                  modv, modv,
                  pl.BlockSpec((512, D), lambda b, i: (0, 0)),
                  pl.BlockSpec((512, D), lambda b, i: (0, 0)),
                  pl.BlockSpec((D, 2048), lambda b, i: (0, 0))],
        out_specs=[col_major, row_major, col_major, row_major, row_major, row_major],
        out_shape=[cm_shape, rm_shape, cm_shape, rm_shape, rm_shape, rm_shape],
        compiler_params=pltpu.CompilerParams(dimension_semantics=("arbitrary", "arbitrary"),
                                             vmem_limit_bytes=VMEM_LIMIT),
        name="inproj",
    )(x, g_pre.reshape(1, D), sc1.reshape(B, 1, D), sh1.reshape(B, 1, D), wqT, wvT, wn)


def _diff_kernel(lam_init, n_kchunks, coef_ref, lam_ref, qT_ref, k_ref, kaug_ref, vT_ref, g_ref, o_ref,
                 acc_ref, m_ref, l_ref, s0_ref, qd_ref, qm_ref, cq_ref, s_ref, p_ref, a_ref):
    tq = qT_ref.shape[2]
    tk = s0_ref.shape[0]
    h = pl.program_id(1)
    q0 = pl.program_id(2) * tq
    c_f = coef_ref[4 * h]
    c_parts = (coef_ref[4 * h + 1], coef_ref[4 * h + 2], coef_ref[4 * h + 3])
    c_diag = lax.div(q0, tk)

    qT = qT_ref[0]
    row = lax.broadcasted_iota(I32, qT.shape, 0)
    zero = jnp.zeros_like(qT)
    q_maps = (jnp.where(row < HEAD_DIM, qT, zero), jnp.where(row >= HEAD_DIM, qT, zero))
    aug = jnp.zeros(qT.shape, F32)
    for j in range(3):
        aug = jnp.where(row == j, 64.0 * c_parts[j], aug)
        aug = jnp.where(row == 3 + j, c_parts[j], aug)
    for mi in range(2):
        qd_ref[mi] = q_maps[mi]
        for sg, sign in enumerate((1.0, -1.0)):
            qm_ref[mi, sg, 0:DIFF_HEAD_W, :] = q_maps[mi]
            qm_ref[mi, sg, DIFF_HEAD_W:, :] = (sign * aug).astype(BF16)

    qpos = q0 + lax.broadcasted_iota(I32, (1, tq), 1)
    cq_ref[...] = c_f * qpos.astype(F32)
    kk = lax.broadcasted_iota(I32, (tk, tq), 0)
    qq = lax.broadcasted_iota(I32, (tk, tq), 1)
    s0_ref[...] = c_f * (kk - qq).astype(F32)
    acc_ref[...] = jnp.zeros_like(acc_ref)
    l_ref[...] = jnp.zeros_like(l_ref)
    m_ref[...] = jnp.full_like(m_ref, NEG_BIG)

    def chunk(i):
        j = i - 1
        c = j + (j >= c_diag).astype(I32)
        return c_diag if isinstance(i, int) and i == 0 else jnp.where(i == 0, c_diag, c)

    def start(c):
        return pl.multiple_of(c * tk, tk)

    def update(mi, slot, logits, shift):
        m_old = m_ref[mi]
        m_new = jnp.maximum(m_old, jnp.max(logits, axis=0, keepdims=True) - shift)
        alpha = jnp.exp2(m_old - m_new)
        p = jnp.exp2(logits - (m_new + shift))
        l_ref[mi] = alpha * l_ref[mi] + jnp.sum(p, axis=0, keepdims=True)
        m_ref[mi] = m_new
        a_ref[slot, mi] = alpha
        p_ref[slot, mi] = p.astype(BF16)

    def scores_diag(slot):
        kb = k_ref[0, pl.ds(start(c_diag), tk), :]
        for mi in range(2):
            s_ref[slot, mi] = jnp.dot(kb, qd_ref[mi], preferred_element_type=F32)

    def softmax_diag(slot):
        bias = jnp.abs(s0_ref[...] + c_f * (start(c_diag) - q0).astype(F32))
        for mi in range(2):
            update(mi, slot, s_ref[slot, mi] - bias, 0.0)

    def scores(c, slot):
        k0 = start(c)
        kb = jnp.concatenate([k_ref[0, pl.ds(k0, tk), :], kaug_ref[pl.ds(k0, tk), :]], axis=1)
        sg = (c > c_diag).astype(I32)
        for mi in range(2):
            s_ref[slot, mi] = jnp.dot(kb, qm_ref[mi, sg], preferred_element_type=F32)

    def softmax(c, slot):
        shift = jnp.where(c > c_diag, -1.0, 1.0) * cq_ref[...]
        for mi in range(2):
            update(mi, slot, s_ref[slot, mi], shift)

    def values(c, slot):
        vb = vT_ref[0, :, pl.ds(start(c), tk)]
        for mi in range(2):
            acc_ref[mi] = a_ref[slot, mi] * acc_ref[mi] + jnp.dot(vb, p_ref[slot, mi],
                                                                  preferred_element_type=F32)

    n = n_kchunks
    scores_diag(0)
    scores(chunk(1), 1)
    scores(chunk(2), 2)
    softmax_diag(0)
    scores(chunk(3), 0)
    softmax(chunk(1), 1)

    def ticks(j, carry):
        t0 = DIFF_TICKS_PER_BODY * j + 2
        for k in range(DIFF_TICKS_PER_BODY):
            t = t0 + k
            scores(chunk(t + 2), (k + 1) % 3)
            softmax(chunk(t), (k + 2) % 3)
            values(chunk(t - 2), k % 3)
        return carry

    lax.fori_loop(0, (n - 4) // DIFF_TICKS_PER_BODY, ticks, 0)
    for t in (n - 2, n - 1):
        softmax(chunk(t), t % 3)
        values(chunk(t - 2), (t - 2) % 3)
    values(chunk(n - 2), (n - 2) % 3)
    values(chunk(n - 1), (n - 1) % 3)

    lam = lam_ref[0:1, 0:1]
    o = acc_ref[0] / l_ref[0] - lam * (acc_ref[1] / l_ref[1])
    y = _rms(o, axis=0) * g_ref[...] * (1.0 - lam_init)
    o_ref[0] = y.T.astype(BF16)


def _diff_attention(qT, k, vT, lam, g_sub, lam_init):
    B, _, L = qT.shape
    tq, tk = min(DIFF_TQ, L), min(DIFF_TK, L)
    assert DIFF_TICKS_PER_BODY % 3 == 0 and L // tk >= 4 and (L // tk - 4) % DIFF_TICKS_PER_BODY == 0
    assert tk % tq == 0 and L <= 64 * 256
    i = jnp.arange(1, N_DIFF_HEADS + 1, dtype=F32)
    c = jnp.exp2(-8.0 * i / N_DIFF_HEADS) * LOG2E
    c1 = c.astype(BF16).astype(F32)
    c2 = (c - c1).astype(BF16).astype(F32)
    c3 = (c - c1 - c2).astype(BF16).astype(F32)
    coef = jnp.stack([c, c1, c2, c3], axis=1).reshape(-1)
    kpos = jnp.arange(L, dtype=I32)
    lane = jnp.arange(DIFF_HEAD_W, dtype=I32)
    kaug = jnp.where(lane[None, :] < 3, (kpos // 64)[:, None],
                     jnp.where(lane[None, :] < 6, (kpos % 64)[:, None], 0)).astype(BF16)
    return pl.pallas_call(
        functools.partial(_diff_kernel, lam_init, L // tk),
        grid=(B, N_DIFF_HEADS, L // tq),
        in_specs=[pl.BlockSpec(memory_space=pltpu.SMEM),
                  pl.BlockSpec((8, 128), lambda b, h, i: (0, 0)),
                  pl.BlockSpec((1, DIFF_HEAD_W, tq), lambda b, h, i: (b, h, i)),
                  pl.BlockSpec((1, L, DIFF_HEAD_W), lambda b, h, i: (b, 0, h)),
                  pl.BlockSpec((L, DIFF_HEAD_W), lambda b, h, i: (0, 0)),
                  pl.BlockSpec((1, DIFF_HEAD_W, L), lambda b, h, i: (b, h, 0)),
                  pl.BlockSpec((DIFF_HEAD_W, 1), lambda b, h, i: (0, 0))],
        out_specs=pl.BlockSpec((1, tq, DIFF_HEAD_W), lambda b, h, i: (b, i, h)),
        out_shape=jax.ShapeDtypeStruct((B, L, DIFF_WIDTH), BF16),
        scratch_shapes=[pltpu.VMEM((2, DIFF_HEAD_W, tq), F32),
                        pltpu.VMEM((2, 1, tq), F32),
                        pltpu.VMEM((2, 1, tq), F32),
                        pltpu.VMEM((tk, tq), F32),
                        pltpu.VMEM((2, DIFF_HEAD_W, tq), BF16),
                        pltpu.VMEM((2, 2, 2 * DIFF_HEAD_W, tq), BF16),
                        pltpu.VMEM((1, tq), F32),
                        pltpu.VMEM((3, 2, tk, tq), F32),
                        pltpu.VMEM((3, 2, tk, tq), BF16),
                        pltpu.VMEM((3, 2, 1, tq), F32)],
        compiler_params=pltpu.CompilerParams(
            dimension_semantics=("arbitrary", "arbitrary", "arbitrary"),
            vmem_limit_bytes=VMEM_LIMIT),
        name="diff_attn",
    )(coef, lam, qT, k, kaug, vT, g_sub.reshape(DIFF_HEAD_W, 1))


def _nat_group_span(g, n_rows):
    r0 = g * NAT_GROUP
    lo, hi = 0, n_rows - NAT_KEY_ROWS
    kb0 = r0 - NAT_KH // 2
    kb0 = min(max(kb0, lo), hi) if isinstance(g, int) else jnp.clip(kb0, lo, hi)
    return r0, kb0


def _nat_bias_table(rpb, n_rows):
    c = jnp.arange(GRID_W)
    col_start = jnp.clip(c - NAT_KW // 2, 0, GRID_W - NAT_KW)
    col_in = (c[None, :] >= col_start[:, None]) & (c[None, :] < col_start[:, None] + NAT_KW)
    dc = jnp.clip(c[None, :] - c[:, None], -(NAT_KW - 1), NAT_KW - 1) + (NAT_KW - 1)
    rpb = rpb.astype(F32)
    cols = jnp.zeros(rpb.shape[:2] + (GRID_W, GRID_W), F32)
    for j in range(2 * NAT_KW - 1):
        cols = cols + jnp.where(dc == j, rpb[:, :, j][:, :, None, None], 0.0)
    cols = jnp.where(col_in[None, None], cols, NEG_BIG)
    n_dr = 2 * NAT_KH - 1
    neg = jnp.full_like(cols, NEG_BIG)
    tiles = jnp.concatenate([jnp.concatenate([cols[:, :n_dr - 1], cols[:, 1:]], axis=-1),
                             jnp.concatenate([neg, cols], axis=-1),
                             jnp.concatenate([cols, neg], axis=-1),
                             jnp.concatenate([neg[:, :1], neg[:, :1]], axis=-1)], axis=1)
    right0, left0, none = n_dr - 1, 2 * n_dr - 1, 3 * n_dr - 1
    n_groups = n_rows // NAT_GROUP
    idx = []
    for g in (0, 1, n_groups - 1):
        r0, kb0 = _nat_group_span(min(g, n_groups - 1), n_rows)
        for j in range(NAT_GROUP):
            r = r0 + j
            rs = min(max(r - NAT_KH // 2, 0), n_rows - NAT_KH)
            for m in range(NAT_KEY_ROWS // 2):
                k_l = kb0 + 2 * m
                in_l = rs <= k_l < rs + NAT_KH
                in_r = rs <= k_l + 1 < rs + NAT_KH
                d_l = k_l - r + (NAT_KH - 1)
                idx.append(d_l if in_l and in_r else right0 + d_l + 1 if in_r else left0 + d_l if in_l else none)
    tbl = jnp.take(tiles, jnp.asarray(idx, I32), axis=1)
    return tbl.reshape(NAT_PAIRS, 2, 3, NAT_GROUP * NAT_KEY_ROWS // 2, GRID_W, 2 * GRID_W)


def _nat_kernel(n_rows, q_ref, k_ref, v_ref, bias_ref, o_ref, sb_ref):
    _, kb0 = _nat_group_span(pl.program_id(2), n_rows)
    pairs = NAT_KEY_ROWS // 2
    k0 = pl.multiple_of(kb0 * GRID_W, GRID_W)
    win = NAT_KEY_ROWS * GRID_W
    kw = k_ref[0, pl.ds(k0, win), :]
    vw = v_ref[0, pl.ds(k0, win), :]
    q = q_ref[0]
    lane = lax.broadcasted_iota(I32, q.shape, 1)
    zero = jnp.zeros_like(q)
    outs = []
    for hh in range(2):
        keep = (lane < HEAD_DIM) if hh == 0 else (lane >= HEAD_DIM)
        s = lax.dot_general(jnp.where(keep, q, zero), kw, NT_DIMS, preferred_element_type=F32)
        for j in range(NAT_GROUP):
            rows = slice(j * GRID_W, (j + 1) * GRID_W)
            for m in range(pairs):
                cols = slice(m * 2 * GRID_W, (m + 1) * 2 * GRID_W)
                sb_ref[rows, cols] = s[rows, cols] + bias_ref[0, hh, 0, j * pairs + m]
        s = sb_ref[...]
        p = jnp.exp(s - jnp.max(s, axis=-1, keepdims=True))
        l = jnp.sum(p, axis=-1, keepdims=True)
        outs.append(jnp.dot(p.astype(BF16), vw, preferred_element_type=F32) / l)
    o_ref[0] = jnp.where(lane < HEAD_DIM, outs[0], outs[1]).astype(BF16)


def _nat_attention(nq, nk, nv, rpb):
    B, L, _ = nq.shape
    n_rows = L // GRID_W
    assert n_rows >= NAT_KEY_ROWS and n_rows % NAT_GROUP == 0
    n_groups = n_rows // NAT_GROUP
    tbl = _nat_bias_table(rpb, n_rows)
    kv = pl.BlockSpec((1, L, 2 * HEAD_DIM), lambda b, p, i: (b, 0, p))
    qo = pl.BlockSpec((1, NAT_GROUP * GRID_W, 2 * HEAD_DIM), lambda b, p, i: (b, i, p))
    kind = lambda i: jnp.where(i == 0, 0, jnp.where(i == n_groups - 1, 2, 1))
    return pl.pallas_call(
        functools.partial(_nat_kernel, n_rows),
        grid=(B, NAT_PAIRS, n_groups),
        in_specs=[qo, kv, kv,
                  pl.BlockSpec((1, 2, 1, NAT_GROUP * NAT_KEY_ROWS // 2, GRID_W, 2 * GRID_W),
                               lambda b, p, i: (p, 0, kind(i), 0, 0, 0))],
        out_specs=qo,
        scratch_shapes=[pltpu.VMEM((NAT_GROUP * GRID_W, NAT_KEY_ROWS * GRID_W), F32)],
        out_shape=jax.ShapeDtypeStruct((B, L, NAT_WIDTH), BF16),
        compiler_params=pltpu.CompilerParams(
            dimension_semantics=("arbitrary", "arbitrary", "arbitrary"),
            vmem_limit_bytes=VMEM_LIMIT),
        name="nat_attn",
    )(nq, nk, nv, tbl)


def _out_kernel(od_ref, on_ref, x_ref, wt_ref, wb_ref, gpost_ref, gt1_ref, gpre_ref, sc2_ref, sh2_ref,
                wr_ref, br_ref, x1_ref, h2_ref, eidx_ref, gate_ref, rank_ref, cnt_ref, carry_ref):
    tm = x_ref.shape[0]

    @pl.when(pl.program_id(0) == 0)
    def _():
        carry_ref[...] = jnp.zeros_like(carry_ref)

    mix = (jnp.dot(od_ref[...], wt_ref[...], preferred_element_type=F32)
           + jnp.dot(on_ref[...], wb_ref[...], preferred_element_type=F32))
    x1 = x_ref[...] + gt1_ref[0] * (_rms(mix) * gpost_ref[...])
    x1_ref[...] = x1
    h2 = _rms(x1) * gpre_ref[...]
    h2 = h2 * (1.0 + sc2_ref[0]) + sh2_ref[0]
    for s in range(ROW_TILE):
        h2_ref[pl.ds(s, tm, stride=ROW_TILE), :] = h2[:, s * 128:(s + 1) * 128]

    wr = wr_ref[...]
    h_hi = h2.astype(BF16)
    h_lo = (h2 - h_hi.astype(F32)).astype(BF16)
    w_hi = wr.astype(BF16)
    w_lo = (wr - w_hi.astype(F32)).astype(BF16)
    logits = (jnp.dot(h_hi, w_hi, preferred_element_type=F32)
              + (jnp.dot(h_hi, w_lo, preferred_element_type=F32)
                 + jnp.dot(h_lo, w_hi, preferred_element_type=F32))) + br_ref[...]
    eio = lax.broadcasted_iota(I32, logits.shape, 1).astype(F32)
    onehot = jnp.zeros_like(logits)
    vals, idxs, sels = [], [], []
    cur = logits
    for _ in range(TOP_K):
        mx = jnp.max(cur, axis=-1, keepdims=True)
        idx = jnp.min(jnp.where(cur == mx, eio, float(N_EXPERTS)), axis=-1, keepdims=True)
        sel = eio == idx
        vals.append(mx)
        idxs.append(idx)
        sels.append(sel)
        cur = jnp.where(sel, -jnp.inf, cur)
        onehot = onehot + sel.astype(F32)

    ex = [jnp.exp(v - vals[0]) for v in vals]
    tot = ex[0] + ex[1] + ex[2] + ex[3]

    rr = lax.broadcasted_iota(I32, (tm, tm), 0)
    cc = lax.broadcasted_iota(I32, (tm, tm), 1)
    tri = (rr > cc).astype(BF16)
    carry = carry_ref[...]
    cum = jnp.dot(tri, onehot.astype(BF16), preferred_element_type=F32) + carry
    ranks = [jnp.sum(jnp.where(sel, cum, 0.0), axis=-1, keepdims=True) for sel in sels]
    carry = carry + jnp.sum(onehot, axis=0, keepdims=True)
    carry_ref[...] = carry
    cnt_ref[...] = carry

    kio = lax.broadcasted_iota(I32, (tm, TOP_K), 1)

    def pack(cols):
        out = jnp.broadcast_to(cols[TOP_K - 1], (tm, TOP_K))
        for k in range(TOP_K - 2, -1, -1):
            out = jnp.where(kio == k, cols[k], out)
        return out

    eidx_ref[...] = pack(idxs).astype(I32)
    gate_ref[...] = pack([e / tot for e in ex])
    rank_ref[...] = pack(ranks).astype(I32)


def _out_router(od, on, x, w_out, g_post, gt1, g_pre, sc2, sh2, w_router, b_router, tokens_per_batch):
    T, D = x.shape
    B = gt1.shape[0]
    tm = min(OUT_TM, tokens_per_batch)
    steps_per_batch = tokens_per_batch // tm
    wt = w_out[:DIFF_WIDTH].astype(BF16)
    wb = w_out[DIFF_WIDTH:].astype(BF16)
    rowblk = lambda w: pl.BlockSpec((tm, w), lambda i: (i, 0))
    const = lambda shape: pl.BlockSpec(shape, lambda i: (0,) * len(shape))
    modv = pl.BlockSpec((1, 1, D), lambda i: (i // steps_per_batch, 0, 0))
    return pl.pallas_call(
        _out_kernel,
        grid=(T // tm,),
        in_specs=[rowblk(DIFF_WIDTH), rowblk(NAT_WIDTH), rowblk(D),
                  const((DIFF_WIDTH, D)), const((NAT_WIDTH, D)), const((1, D)), modv, const((1, D)),
                  modv, modv, const((D, N_EXPERTS)), const((1, N_EXPERTS))],
        out_specs=[rowblk(D), pl.BlockSpec((tm * ROW_TILE, 128), lambda i: (i, 0)),
                   rowblk(TOP_K), rowblk(TOP_K), rowblk(TOP_K), const((1, N_EXPERTS))],
        out_shape=[jax.ShapeDtypeStruct((T, D), F32), jax.ShapeDtypeStruct((T * ROW_TILE, 128), F32),
                   jax.ShapeDtypeStruct((T, TOP_K), I32), jax.ShapeDtypeStruct((T, TOP_K), F32),
                   jax.ShapeDtypeStruct((T, TOP_K), I32), jax.ShapeDtypeStruct((1, N_EXPERTS), F32)],
        scratch_shapes=[pltpu.VMEM((1, N_EXPERTS), F32)],
        compiler_params=pltpu.CompilerParams(dimension_semantics=("arbitrary",),
                                             vmem_limit_bytes=VMEM_LIMIT),
        name="out_router",
    )(od, on, x, wt, wb, g_post.reshape(1, D), gt1.reshape(B, 1, D), g_pre.reshape(1, D),
      sc2.reshape(B, 1, D), sh2.reshape(B, 1, D), w_router, b_router.reshape(1, N_EXPERTS))


def _expert_kernel(spare_base, be_ref, tok_ref, tok_next_ref, dst_prev_ref, dst_ref, h2_hbm, w1_ref, b1g_ref, b1l_ref,
                   w2_ref, b2_ref, y_hbm, xbuf, ybuf, xb, w1t, w2t, hh, gsem, ssem):
    i = pl.program_id(0)
    n = pl.num_programs(0)
    bm = xbuf.shape[1] // ROW_TILE

    def tile_rows(t):
        t = t * ROW_TILE
        return pl.ds(t if isinstance(t, int) else pl.multiple_of(t, ROW_TILE), ROW_TILE)
    ff = w2_ref.shape[1]
    slot = i % 2

    def gather_row(idx_ref, s, r):
        return pltpu.make_async_copy(h2_hbm.at[tile_rows(idx_ref[0, 0, r]), :],
                                     xbuf.at[s, tile_rows(r), :], gsem.at[s])

    def scatter_row(idx_ref, s, r):
        return pltpu.make_async_copy(ybuf.at[s, tile_rows(r), :],
                                     y_hbm.at[tile_rows(idx_ref[0, 0, r]), :], ssem.at[s])

    def wait_gather(s):
        pltpu.make_async_copy(h2_hbm.at[pl.ds(0, bm * ROW_TILE), :], xbuf.at[s], gsem.at[s]).wait()

    def wait_scatter(s):
        pltpu.make_async_copy(ybuf.at[s], y_hbm.at[pl.ds(0, bm * ROW_TILE), :], ssem.at[s]).wait()

    @pl.when(i == 0)
    def _():
        ybuf[...] = jnp.zeros(ybuf.shape, F32)
        for r in range(bm):
            pltpu.make_async_copy(ybuf.at[0, tile_rows(r), :], y_hbm.at[tile_rows(spare_base + r), :],
                                  ssem.at[0]).start(priority=r % 2)
        for r in range(bm):
            gather_row(tok_ref, 0, r).start(priority=r % 2)

    @pl.when((i == 0) | (be_ref[i] != be_ref[jnp.maximum(i - 1, 0)]))
    def _():
        cw = 256
        for c in range(w1_ref.shape[2] // cw):
            w1t[c * cw:(c + 1) * cw, :] = w1_ref[0, :, c * cw:(c + 1) * cw].T.astype(BF16)
        for c in range(w2_ref.shape[2] // cw):
            w2t[c * cw:(c + 1) * cw, :] = w2_ref[0, :, c * cw:(c + 1) * cw].T.astype(BF16)

    wait_gather(slot)
    for s8 in range(ROW_TILE):
        xb[:, s8 * 128:(s8 + 1) * 128] = xbuf[slot, pl.ds(s8, bm, stride=ROW_TILE), :].astype(BF16)

    n_lane_tiles = bm // 128

    @pl.when(i >= 0)
    def _():
        for r in range(bm):
            gather_row(tok_next_ref, 1 - slot, r).start(priority=r % 2)
        hh_t = lax.dot_general(w1t[...], xb[...], NT_DIMS, preferred_element_type=F32)
        for j in range(n_lane_tiles):
            hh[j] = hh_t[:, j * 128:(j + 1) * 128]

    for r in range(bm):
        scatter_row(dst_prev_ref, 1 - slot, r).start(priority=r % 2)
    even = jnp.concatenate([hh[j, pl.ds(0, ff, stride=2), :] for j in range(n_lane_tiles)], axis=1)
    odd = jnp.concatenate([hh[j, pl.ds(1, ff, stride=2), :] for j in range(n_lane_tiles)], axis=1)
    glu = jnp.minimum(even + b1g_ref[0], SWIGLU_LIMIT)
    lin = jnp.clip(odd + b1l_ref[0], -SWIGLU_LIMIT, SWIGLU_LIMIT)
    act = glu * (1.0 / (1.0 + jnp.exp(-SWIGLU_ALPHA * glu))) * (lin + 1.0)
    y_t = jnp.dot(w2t[...], act.astype(BF16), preferred_element_type=F32)

    y = y_t.T + b2_ref[0]
    wait_scatter(slot)
    for s8 in range(ROW_TILE):
        ybuf[slot, pl.ds(s8, bm, stride=ROW_TILE), :] = y[:, s8 * 128:(s8 + 1) * 128]

    @pl.when(i == n - 1)
    def _():
        wait_scatter(1 - slot)
        for r in range(bm):
            scatter_row(dst_ref, slot, r).start(priority=r % 2)
        wait_scatter(slot)
        wait_gather(1 - slot)


def _experts(h2, blk_expert, tok_buf, dst_buf, w1, b1, w2, b2, n_rows_out):
    D = w1.shape[1]
    assert D == ROW_TILE * 128 and h2.shape[1] == 128
    n_blocks = blk_expert.shape[0]
    bm = MOE_BM
    F = w2.shape[1]
    b1g = b1[:, 0::2].reshape(N_EXPERTS, F, 1)
    b1l = b1[:, 1::2].reshape(N_EXPERTS, F, 1)
    tok3 = tok_buf.reshape(n_blocks, 1, bm)
    spare = n_rows_out - 2 * bm + jnp.arange(bm, dtype=I32)
    dst3 = jnp.concatenate([spare, dst_buf]).reshape(n_blocks + 1, 1, bm)
    smem_blk = lambda fn: pl.BlockSpec((1, 1, bm), fn, memory_space=pltpu.SMEM)
    grid_spec = pltpu.PrefetchScalarGridSpec(
        num_scalar_prefetch=1,
        grid=(n_blocks,),
        in_specs=[smem_blk(lambda i, be: (i, 0, 0)),
                  smem_blk(lambda i, be: (jnp.minimum(i + 1, n_blocks - 1), 0, 0)),
                  smem_blk(lambda i, be: (i, 0, 0)),
                  smem_blk(lambda i, be: (i + 1, 0, 0)),
                  pl.BlockSpec(memory_space=pl.ANY),
                  pl.BlockSpec((1, D, 2 * F), lambda i, be: (be[i], 0, 0)),
                  pl.BlockSpec((1, F, 1), lambda i, be: (be[i], 0, 0)),
                  pl.BlockSpec((1, F, 1), lambda i, be: (be[i], 0, 0)),
                  pl.BlockSpec((1, F, D), lambda i, be: (be[i], 0, 0)),
                  pl.BlockSpec((1, 1, D), lambda i, be: (be[i], 0, 0))],
        out_specs=pl.BlockSpec(memory_space=pl.ANY),
        scratch_shapes=[pltpu.VMEM((2, bm * ROW_TILE, 128), F32), pltpu.VMEM((2, bm * ROW_TILE, 128), F32),
                        pltpu.VMEM((bm, D), BF16),
                        pltpu.VMEM((2 * F, D), BF16), pltpu.VMEM((D, F), BF16),
                        pltpu.VMEM((bm // 128, 2 * F, 128), F32),
                        pltpu.SemaphoreType.DMA((2,)), pltpu.SemaphoreType.DMA((2,))],
    )
    return pl.pallas_call(
        functools.partial(_expert_kernel, n_rows_out - bm),
        grid_spec=grid_spec,
        out_shape=jax.ShapeDtypeStruct((n_rows_out * ROW_TILE, 128), F32),
        compiler_params=pltpu.CompilerParams(dimension_semantics=("arbitrary",),
                                             vmem_limit_bytes=EXPERT_VMEM_LIMIT),
        name="experts",
    )(blk_expert, tok3, tok3, dst3, dst3, h2, w1, b1g, b1l, w2, b2.reshape(N_EXPERTS, 1, D))


def _combine_kernel(y0_ref, y1_ref, y2_ref, y3_ref, gate_ref, x1_ref, gt2_ref, g_ref, o_ref):
    tm = x1_ref.shape[0]
    gates = gate_ref[...]

    def rows(y_ref):
        return jnp.concatenate([y_ref[pl.ds(s, tm, stride=ROW_TILE), :] for s in range(ROW_TILE)], axis=1)

    f = gates[:, 0:1] * rows(y0_ref)
    for k, y_ref in enumerate((y1_ref, y2_ref, y3_ref), start=1):
        f = f + gates[:, k:k + 1] * rows(y_ref)
    o_ref[...] = x1_ref[...] + gt2_ref[0] * (_rms(f) * g_ref[...])


def _combine(y_tok, gates, x1, gt2, g_post, tokens_per_batch):
    T, D = x1.shape
    B = gt2.shape[0]
    tm = min(COMBINE_TM, tokens_per_batch)
    steps_per_batch = tokens_per_batch // tm
    steps = T // tm
    y_spec = lambda k: pl.BlockSpec((tm * ROW_TILE, 128), lambda i: (k * steps + i, 0))
    return pl.pallas_call(
        _combine_kernel,
        grid=(steps,),
        in_specs=[y_spec(0), y_spec(1), y_spec(2), y_spec(3),
                  pl.BlockSpec((tm, TOP_K), lambda i: (i, 0)),
                  pl.BlockSpec((tm, D), lambda i: (i, 0)),
                  pl.BlockSpec((1, 1, D), lambda i: (i // steps_per_batch, 0, 0)),
                  pl.BlockSpec((1, D), lambda i: (0, 0))],
        out_specs=pl.BlockSpec((tm, D), lambda i: (i, 0)),
        out_shape=jax.ShapeDtypeStruct((T, D), F32),
        compiler_params=pltpu.CompilerParams(dimension_semantics=("arbitrary",),
                                             vmem_limit_bytes=VMEM_LIMIT),
        name="combine",
    )(y_tok, y_tok, y_tok, y_tok, gates, x1, gt2.reshape(B, 1, D), g_post.reshape(1, D))


def _inverse_kernel(n_clear_steps, dest_ref, inv_ref):
    i = pl.program_id(0)
    chunk = dest_ref.shape[2]

    @pl.when(i < n_clear_steps)
    def _():
        def clear(s, carry):
            inv_ref[i * chunk + s] = 0
            return carry
        lax.fori_loop(0, chunk, clear, 0, unroll=8)

    @pl.when(i >= n_clear_steps)
    def _():
        base = (i - n_clear_steps) * chunk

        def put(a, carry):
            inv_ref[dest_ref[0, 0, a]] = base + a + 1
            return carry
        lax.fori_loop(0, chunk, put, 0, unroll=8)


def _inverse_map(dest, n_slots):
    n_assign = dest.shape[0]
    chunk = math.gcd(math.gcd(n_assign, n_slots), INVERSE_CHUNK)
    n_clear, n_put = n_slots // chunk, n_assign // chunk
    return pl.pallas_call(
        functools.partial(_inverse_kernel, n_clear),
        grid=(n_clear + n_put,),
        in_specs=[pl.BlockSpec((1, 1, chunk), lambda i: (jnp.maximum(i - n_clear, 0), 0, 0),
                               memory_space=pltpu.SMEM)],
        out_specs=pl.BlockSpec(memory_space=pltpu.SMEM),
        out_shape=jax.ShapeDtypeStruct((n_slots,), I32),
        compiler_params=pltpu.CompilerParams(dimension_semantics=("arbitrary",)),
        name="inverse_map",
    )(dest.reshape(n_put, 1, chunk))


def _dispatch_plan(eidx, rank, counts):
    T = eidx.shape[0]
    n_assign = eidx.size
    bm = MOE_BM
    cap = (n_assign + N_EXPERTS * (bm - 1) + bm - 1) // bm * bm
    n_blocks = cap // bm
    counts = counts.reshape(N_EXPERTS).astype(I32)
    padded = (counts + bm - 1) // bm * bm
    pad_end = jnp.cumsum(padded)
    pad_start = pad_end - padded
    eio = jnp.arange(N_EXPERTS, dtype=I32)
    start_of = jnp.sum(jnp.where(eidx[..., None] == eio, pad_start, 0), axis=-1)
    dest = (start_of + rank).reshape(-1)
    blk_start = jnp.arange(n_blocks, dtype=I32) * bm
    blk_expert = jnp.minimum(jnp.sum((blk_start[:, None] >= pad_end[None, :]).astype(I32), axis=-1),
                             N_EXPERTS - 1)
    inv = _inverse_map(dest, cap)
    is_pad = inv == 0
    a = inv - 1
    tok_buf = jnp.where(is_pad, 0, a // TOP_K).astype(I32)
    pad_row = n_assign + jnp.cumsum(is_pad.astype(I32)) - 1
    dst_buf = jnp.where(is_pad, pad_row, (a % TOP_K) * T + a // TOP_K).astype(I32)
    return blk_expert, tok_buf, dst_buf, cap + 2 * bm


def _layer(x, c, l, w_ada, b_ada, g_pre_mix, g_post_mix, w_in, w_out, lam_q1, lam_k1, lam_q2, lam_k2,
           g_subln, nat_rpb, g_pre_ffn, g_post_ffn, w_router, b_router, w1, b1, w2, b2):
    B, L, D = x.shape
    lam_init = 0.8 - 0.6 * math.exp(-0.3 * l)
    mod, lam = _ada(c, w_ada, b_ada, lam_q1, lam_k1, lam_q2, lam_k2, lam_init)
    sh1, sc1, gt1, sh2, sc2, gt2 = jnp.split(mod, 6, axis=-1)

    qT, kd, vT, nq, nk, nv = _inproj(x, g_pre_mix, sc1, sh1, w_in)
    o_diff = _diff_attention(qT, kd, vT, lam, g_subln, lam_init)
    o_nat = _nat_attention(nq, nk, nv, nat_rpb)

    T = B * L
    x1, h2, eidx, gates, rank, counts = _out_router(
        o_diff.reshape(T, DIFF_WIDTH), o_nat.reshape(T, NAT_WIDTH), x.reshape(T, D), w_out,
        g_post_mix, gt1, g_pre_ffn, sc2, sh2, w_router, b_router, L)
    blk_expert, tok_buf, dst_buf, n_rows_out = _dispatch_plan(eidx, rank, counts)
    y_tok = _experts(h2, blk_expert, tok_buf, dst_buf, w1, b1, w2, b2, n_rows_out)
    out = _combine(y_tok, gates, x1, gt2, g_post_ffn, L)
    return out.reshape(B, L, D)


def kernel(x, c, w_ada, b_ada, g_pre_mix, g_post_mix, w_in, w_out, lam_q1, lam_k1, lam_q2, lam_k2,
           g_subln, nat_rpb, g_pre_ffn, g_post_ffn, w_router, b_router, w1, b1, w2, b2):
    depth = w_ada.shape[0]
    for l in range(depth):
        x = _layer(x, c, l, w_ada[l], b_ada[l], g_pre_mix[l], g_post_mix[l], w_in[l], w_out[l],
                   lam_q1[l], lam_k1[l], lam_q2[l], lam_k2[l], g_subln[l], nat_rpb[l],
                   g_pre_ffn[l], g_post_ffn[l], w_router[l], b_router[l], w1[l], b1[l], w2[l], b2[l])
    return x
```

```python
import functools
import math

import jax
import jax.numpy as jnp
from jax import lax
from jax.experimental import pallas as pl
from jax.experimental.pallas import tpu as pltpu

F32 = jnp.float32
BF16 = jnp.bfloat16
I32 = jnp.int32

HEAD_DIM = 64
N_DIFF_HEADS = 4
DIFF_HEAD_W = 2 * HEAD_DIM
DIFF_WIDTH = N_DIFF_HEADS * DIFF_HEAD_W
N_NAT_HEADS = 8
NAT_WIDTH = N_NAT_HEADS * HEAD_DIM
NAT_PAIRS = N_NAT_HEADS // 2
GRID_W = 64
NAT_KH = 8
NAT_KW = 16
N_EXPERTS = 32
TOP_K = 4
SWIGLU_LIMIT = 7.0
SWIGLU_ALPHA = 1.702
RMS_EPS = 1e-6
NEG_BIG = -1e30
LOG2E = 1.4426950408889634
ROW_TILE = 8

NT_DIMS = (((1,), (1,)), ((), ()))

ADA_TN = 1536
INPROJ_TM = 512
DIFF_TQ = 256
DIFF_TK = 512
DIFF_TICKS_PER_BODY = 3
NAT_GROUP = 8
NAT_KEY_ROWS = 16
OUT_TM = 512
MOE_BM = 256
COMBINE_TM = 512
INVERSE_CHUNK = 4096
VMEM_LIMIT = 48 * 1024 * 1024
EXPERT_VMEM_LIMIT = 56 * 1024 * 1024


def _rms(x, axis=-1):
    return x * lax.rsqrt(jnp.mean(x * x, axis=axis, keepdims=True) + RMS_EPS)


def _ada_kernel(lam_init, c_ref, w_ref, b_ref, lq1_ref, lk1_ref, lq2_ref, lk2_ref, mod_ref, lam_ref):
    c = c_ref[...]
    s = c * (1.0 / (1.0 + jnp.exp(-c)))
    mod_ref[...] = jnp.dot(s, w_ref[...], preferred_element_type=F32,
                           precision=lax.Precision.HIGHEST) + b_ref[...]
    d1 = jnp.sum(lq1_ref[...] * lk1_ref[...], axis=-1, keepdims=True)
    d2 = jnp.sum(lq2_ref[...] * lk2_ref[...], axis=-1, keepdims=True)
    lam = jnp.exp(d1) - jnp.exp(d2) + lam_init
    lam_ref[...] = jnp.broadcast_to(lam, lam_ref.shape)


def _ada(c, w_ada, b_ada, lq1, lk1, lq2, lk2, lam_init):
    B, D = c.shape
    N = w_ada.shape[1]
    c8 = jnp.zeros((8, D), F32).at[:B].set(c)
    vec = pl.BlockSpec((1, HEAD_DIM), lambda j: (0, 0))
    mod, lam = pl.pallas_call(
        functools.partial(_ada_kernel, lam_init),
        grid=(N // ADA_TN,),
        in_specs=[pl.BlockSpec((8, D), lambda j: (0, 0)),
                  pl.BlockSpec((D, ADA_TN), lambda j: (0, j)),
                  pl.BlockSpec((1, ADA_TN), lambda j: (0, j)),
                  vec, vec, vec, vec],
        out_specs=[pl.BlockSpec((8, ADA_TN), lambda j: (0, j)),
                   pl.BlockSpec((8, 128), lambda j: (0, 0))],
        out_shape=[jax.ShapeDtypeStruct((8, N), F32), jax.ShapeDtypeStruct((8, 128), F32)],
        compiler_params=pltpu.CompilerParams(dimension_semantics=("arbitrary",),
                                             vmem_limit_bytes=VMEM_LIMIT),
        name="ada",
    )(c8, w_ada, b_ada.reshape(1, N), lq1.reshape(1, -1), lk1.reshape(1, -1),
      lq2.reshape(1, -1), lk2.reshape(1, -1))
    return mod[:B], lam


def _inproj_kernel(x_ref, g_ref, sc_ref, sh_ref, wqT_ref, wvT_ref, wn_ref,
                   qT_ref, k_ref, vT_ref, nq_ref, nk_ref, nv_ref):
    h = _rms(x_ref[0]) * g_ref[...]
    h = h * (1.0 + sc_ref[0]) + sh_ref[0]
    hb = h.astype(BF16)
    qT_ref[0] = lax.dot_general(wqT_ref[...], hb, NT_DIMS, preferred_element_type=F32).astype(BF16)
    vT_ref[0] = lax.dot_general(wvT_ref[...], hb, NT_DIMS, preferred_element_type=F32).astype(BF16)
    rest = jnp.dot(hb, wn_ref[...], preferred_element_type=F32).astype(BF16)
    k_ref[0] = rest[:, 0:512]
    nq_ref[0] = rest[:, 512:1024]
    nk_ref[0] = rest[:, 1024:1536]
    nv_ref[0] = rest[:, 1536:2048]


def _inproj(x, g_pre, sc1, sh1, w_in):
    B, L, D = x.shape
    tm = INPROJ_TM
    scale = HEAD_DIM ** -0.5
    wqT = (w_in[:, 0:512] * (scale * LOG2E)).T.astype(BF16)
    wvT = w_in[:, 1024:1536].T.astype(BF16)
    wn = jnp.concatenate([w_in[:, 512:1024], w_in[:, 1536:2048] * scale, w_in[:, 2048:3072]],
                         axis=1).astype(BF16)
    row_major = pl.BlockSpec((1, tm, 512), lambda b, i: (b, i, 0))
    col_major = pl.BlockSpec((1, 512, tm), lambda b, i: (b, 0, i))
    modv = pl.BlockSpec((1, 1, D), lambda b, i: (b, 0, 0))
    rm_shape = jax.ShapeDtypeStruct((B, L, 512), BF16)
    cm_shape = jax.ShapeDtypeStruct((B, 512, L), BF16)
    return pl.pallas_call(
        _inproj_kernel,
        grid=(B, L // tm),
        in_specs=[pl.BlockSpec((1, tm, D), lambda b, i: (b, i, 0)),
                  pl.BlockSpec((1, D), lambda b, i: (0, 0)),
                  modv, modv,
                  pl.BlockSpec((512, D), lambda b, i: (0, 0)),
                  pl.BlockSpec((512, D), lambda b, i: (0, 0)),
                  pl.BlockSpec((D, 2048), lambda b, i: (0, 0))],
        out_specs=[col_major, row_major, col_major, row_major, row_major, row_major],
        out_shape=[cm_shape, rm_shape, cm_shape, rm_shape, rm_shape, rm_shape],
        compiler_params=pltpu.CompilerParams(dimension_semantics=("arbitrary", "arbitrary"),
                                             vmem_limit_bytes=VMEM_LIMIT),
        name="inproj",
    )(x, g_pre.reshape(1, D), sc1.reshape(B, 1, D), sh1.reshape(B, 1, D), wqT, wvT, wn)


def _diff_kernel(lam_init, n_kchunks, coef_ref, lam_ref, qT_ref, k_ref, kaug_ref, vT_ref, g_ref, o_ref,
                 acc_ref, m_ref, l_ref, s0_ref, qd_ref, qm_ref, cq_ref, s_ref, p_ref, a_ref):
    tq = qT_ref.shape[2]
    tk = s0_ref.shape[0]
    h = pl.program_id(1)
    q0 = pl.program_id(2) * tq
    c_f = coef_ref[4 * h]
    c_parts = (coef_ref[4 * h + 1], coef_ref[4 * h + 2], coef_ref[4 * h + 3])
    c_diag = lax.div(q0, tk)

    qT = qT_ref[0]
    row = lax.broadcasted_iota(I32, qT.shape, 0)
    zero = jnp.zeros_like(qT)
    q_maps = (jnp.where(row < HEAD_DIM, qT, zero), jnp.where(row >= HEAD_DIM, qT, zero))
    aug = jnp.zeros(qT.shape, F32)
    for j in range(3):
        aug = jnp.where(row == j, 64.0 * c_parts[j], aug)
        aug = jnp.where(row == 3 + j, c_parts[j], aug)
    for mi in range(2):
        qd_ref[mi] = q_maps[mi]
        for sg, sign in enumerate((1.0, -1.0)):
            qm_ref[mi, sg, 0:DIFF_HEAD_W, :] = q_maps[mi]
            qm_ref[mi, sg, DIFF_HEAD_W:, :] = (sign * aug).astype(BF16)

    qpos = q0 + lax.broadcasted_iota(I32, (1, tq), 1)
    cq_ref[...] = c_f * qpos.astype(F32)
    kk = lax.broadcasted_iota(I32, (tk, tq), 0)
    qq = lax.broadcasted_iota(I32, (tk, tq), 1)
    s0_ref[...] = c_f * (kk - qq).astype(F32)
    acc_ref[...] = jnp.zeros_like(acc_ref)
    l_ref[...] = jnp.zeros_like(l_ref)
    m_ref[...] = jnp.full_like(m_ref, NEG_BIG)

    def chunk(i):
        j = i - 1
        c = j + (j >= c_diag).astype(I32)
        return c_diag if isinstance(i, int) and i == 0 else jnp.where(i == 0, c_diag, c)

    def start(c):
        return pl.multiple_of(c * tk, tk)

    def update(mi, slot, logits, shift):
        m_old = m_ref[mi]
        m_new = jnp.maximum(m_old, jnp.max(logits, axis=0, keepdims=True) - shift)
        alpha = jnp.exp2(m_old - m_new)
        p = jnp.exp2(logits - (m_new + shift))
        l_ref[mi] = alpha * l_ref[mi] + jnp.sum(p, axis=0, keepdims=True)
        m_ref[mi] = m_new
        a_ref[slot, mi] = alpha
        p_ref[slot, mi] = p.astype(BF16)

    def scores_diag(slot):
        kb = k_ref[0, pl.ds(start(c_diag), tk), :]
        for mi in range(2):
            s_ref[slot, mi] = jnp.dot(kb, qd_ref[mi], preferred_element_type=F32)

    def softmax_diag(slot):
        bias = jnp.abs(s0_ref[...] + c_f * (start(c_diag) - q0).astype(F32))
        for mi in range(2):
            update(mi, slot, s_ref[slot, mi] - bias, 0.0)

    def scores(c, slot):
        k0 = start(c)
        kb = jnp.concatenate([k_ref[0, pl.ds(k0, tk), :], kaug_ref[pl.ds(k0, tk), :]], axis=1)
        sg = (c > c_diag).astype(I32)
        for mi in range(2):
            s_ref[slot, mi] = jnp.dot(kb, qm_ref[mi, sg], preferred_element_type=F32)

    def softmax(c, slot):
        shift = jnp.where(c > c_diag, -1.0, 1.0) * cq_ref[...]
        for mi in range(2):
            update(mi, slot, s_ref[slot, mi], shift)

    def values(c, slot):
        vb = vT_ref[0, :, pl.ds(start(c), tk)]
        for mi in range(2):
            acc_ref[mi] = a_ref[slot, mi] * acc_ref[mi] + jnp.dot(vb, p_ref[slot, mi],
                                                                  preferred_element_type=F32)

    n = n_kchunks
    scores_diag(0)
    scores(chunk(1), 1)
    scores(chunk(2), 2)
    softmax_diag(0)
    scores(chunk(3), 0)
    softmax(chunk(1), 1)

    def ticks(j, carry):
        t0 = DIFF_TICKS_PER_BODY * j + 2
        for k in range(DIFF_TICKS_PER_BODY):
            t = t0 + k
            scores(chunk(t + 2), (k + 1) % 3)
            softmax(chunk(t), (k + 2) % 3)
            values(chunk(t - 2), k % 3)
        return carry

    lax.fori_loop(0, (n - 4) // DIFF_TICKS_PER_BODY, ticks, 0)
    for t in (n - 2, n - 1):
        softmax(chunk(t), t % 3)
        values(chunk(t - 2), (t - 2) % 3)
    values(chunk(n - 2), (n - 2) % 3)
    values(chunk(n - 1), (n - 1) % 3)

    lam = lam_ref[0:1, 0:1]
    o = acc_ref[0] / l_ref[0] - lam * (acc_ref[1] / l_ref[1])
    y = _rms(o, axis=0) * g_ref[...] * (1.0 - lam_init)
    o_ref[0] = y.T.astype(BF16)


def _diff_attention(qT, k, vT, lam, g_sub, lam_init):
    B, _, L = qT.shape
    tq, tk = min(DIFF_TQ, L), min(DIFF_TK, L)
    assert DIFF_TICKS_PER_BODY % 3 == 0 and L // tk >= 4 and (L // tk - 4) % DIFF_TICKS_PER_BODY == 0
    assert tk % tq == 0 and L <= 64 * 256
    i = jnp.arange(1, N_DIFF_HEADS + 1, dtype=F32)
    c = jnp.exp2(-8.0 * i / N_DIFF_HEADS) * LOG2E
    c1 = c.astype(BF16).astype(F32)
    c2 = (c - c1).astype(BF16).astype(F32)
    c3 = (c - c1 - c2).astype(BF16).astype(F32)
    coef = jnp.stack([c, c1, c2, c3], axis=1).reshape(-1)
    kpos = jnp.arange(L, dtype=I32)
    lane = jnp.arange(DIFF_HEAD_W, dtype=I32)
    kaug = jnp.where(lane[None, :] < 3, (kpos // 64)[:, None],
                     jnp.where(lane[None, :] < 6, (kpos % 64)[:, None], 0)).astype(BF16)
    return pl.pallas_call(
        functools.partial(_diff_kernel, lam_init, L // tk),
        grid=(B, N_DIFF_HEADS, L // tq),
        in_specs=[pl.BlockSpec(memory_space=pltpu.SMEM),
                  pl.BlockSpec((8, 128), lambda b, h, i: (0, 0)),
                  pl.BlockSpec((1, DIFF_HEAD_W, tq), lambda b, h, i: (b, h, i)),
                  pl.BlockSpec((1, L, DIFF_HEAD_W), lambda b, h, i: (b, 0, h)),
                  pl.BlockSpec((L, DIFF_HEAD_W), lambda b, h, i: (0, 0)),
                  pl.BlockSpec((1, DIFF_HEAD_W, L), lambda b, h, i: (b, h, 0)),
                  pl.BlockSpec((DIFF_HEAD_W, 1), lambda b, h, i: (0, 0))],
        out_specs=pl.BlockSpec((1, tq, DIFF_HEAD_W), lambda b, h, i: (b, i, h)),
        out_shape=jax.ShapeDtypeStruct((B, L, DIFF_WIDTH), BF16),
        scratch_shapes=[pltpu.VMEM((2, DIFF_HEAD_W, tq), F32),
                        pltpu.VMEM((2, 1, tq), F32),
                        pltpu.VMEM((2, 1, tq), F32),
                        pltpu.VMEM((tk, tq), F32),
                        pltpu.VMEM((2, DIFF_HEAD_W, tq), BF16),
                        pltpu.VMEM((2, 2, 2 * DIFF_HEAD_W, tq), BF16),
                        pltpu.VMEM((1, tq), F32),
                        pltpu.VMEM((3, 2, tk, tq), F32),
                        pltpu.VMEM((3, 2, tk, tq), BF16),
                        pltpu.VMEM((3, 2, 1, tq), F32)],
        compiler_params=pltpu.CompilerParams(
            dimension_semantics=("arbitrary", "arbitrary", "arbitrary"),
            vmem_limit_bytes=VMEM_LIMIT),
        name="diff_attn",
    )(coef, lam, qT, k, kaug, vT, g_sub.reshape(DIFF_HEAD_W, 1))


def _nat_group_span(g, n_rows):
    r0 = g * NAT_GROUP
    lo, hi = 0, n_rows - NAT_KEY_ROWS
    kb0 = r0 - NAT_KH // 2
    kb0 = min(max(kb0, lo), hi) if isinstance(g, int) else jnp.clip(kb0, lo, hi)
    return r0, kb0


def _nat_bias_table(rpb, n_rows):
    c = jnp.arange(GRID_W)
    col_start = jnp.clip(c - NAT_KW // 2, 0, GRID_W - NAT_KW)
    col_in = (c[None, :] >= col_start[:, None]) & (c[None, :] < col_start[:, None] + NAT_KW)
    dc = jnp.clip(c[None, :] - c[:, None], -(NAT_KW - 1), NAT_KW - 1) + (NAT_KW - 1)
    rpb = rpb.astype(F32)
    cols = jnp.zeros(rpb.shape[:2] + (GRID_W, GRID_W), F32)
    for j in range(2 * NAT_KW - 1):
        cols = cols + jnp.where(dc == j, rpb[:, :, j][:, :, None, None], 0.0)
    cols = jnp.where(col_in[None, None], cols, NEG_BIG)
    n_dr = 2 * NAT_KH - 1
    neg = jnp.full_like(cols, NEG_BIG)
    tiles = jnp.concatenate([jnp.concatenate([cols[:, :n_dr - 1], cols[:, 1:]], axis=-1),
                             jnp.concatenate([neg, cols], axis=-1),
                             jnp.concatenate([cols, neg], axis=-1),
                             jnp.concatenate([neg[:, :1], neg[:, :1]], axis=-1)], axis=1)
    right0, left0, none = n_dr - 1, 2 * n_dr - 1, 3 * n_dr - 1
    n_groups = n_rows // NAT_GROUP
    idx = []
    for g in (0, 1, n_groups - 1):
        r0, kb0 = _nat_group_span(min(g, n_groups - 1), n_rows)
        for j in range(NAT_GROUP):
            r = r0 + j
            rs = min(max(r - NAT_KH // 2, 0), n_rows - NAT_KH)
            for m in range(NAT_KEY_ROWS // 2):
                k_l = kb0 + 2 * m
                in_l = rs <= k_l < rs + NAT_KH
                in_r = rs <= k_l + 1 < rs + NAT_KH
                d_l = k_l - r + (NAT_KH - 1)
                idx.append(d_l if in_l and in_r else right0 + d_l + 1 if in_r else left0 + d_l if in_l else none)
    tbl = jnp.take(tiles, jnp.asarray(idx, I32), axis=1)
    return tbl.reshape(NAT_PAIRS, 2, 3, NAT_GROUP * NAT_KEY_ROWS // 2, GRID_W, 2 * GRID_W)


def _nat_kernel(n_rows, q_ref, k_ref, v_ref, bias_ref, o_ref, sb_ref):
    _, kb0 = _nat_group_span(pl.program_id(2), n_rows)
    pairs = NAT_KEY_ROWS // 2
    k0 = pl.multiple_of(kb0 * GRID_W, GRID_W)
    win = NAT_KEY_ROWS * GRID_W
    kw = k_ref[0, pl.ds(k0, win), :]
    vw = v_ref[0, pl.ds(k0, win), :]
    q = q_ref[0]
    lane = lax.broadcasted_iota(I32, q.shape, 1)
    zero = jnp.zeros_like(q)
    outs = []
    for hh in range(2):
        keep = (lane < HEAD_DIM) if hh == 0 else (lane >= HEAD_DIM)
        s = lax.dot_general(jnp.where(keep, q, zero), kw, NT_DIMS, preferred_element_type=F32)
        for j in range(NAT_GROUP):
            rows = slice(j * GRID_W, (j + 1) * GRID_W)
            for m in range(pairs):
                cols = slice(m * 2 * GRID_W, (m + 1) * 2 * GRID_W)
                sb_ref[rows, cols] = s[rows, cols] + bias_ref[0, hh, 0, j * pairs + m]
        s = sb_ref[...]
        p = jnp.exp(s - jnp.max(s, axis=-1, keepdims=True))
        l = jnp.sum(p, axis=-1, keepdims=True)
        outs.append(jnp.dot(p.astype(BF16), vw, preferred_element_type=F32) / l)
    o_ref[0] = jnp.where(lane < HEAD_DIM, outs[0], outs[1]).astype(BF16)


def _nat_attention(nq, nk, nv, rpb):
    B, L, _ = nq.shape
    n_rows = L // GRID_W
    assert n_rows >= NAT_KEY_ROWS and n_rows % NAT_GROUP == 0
    n_groups = n_rows // NAT_GROUP
    tbl = _nat_bias_table(rpb, n_rows)
    kv = pl.BlockSpec((1, L, 2 * HEAD_DIM), lambda b, p, i: (b, 0, p))
    qo = pl.BlockSpec((1, NAT_GROUP * GRID_W, 2 * HEAD_DIM), lambda b, p, i: (b, i, p))
    kind = lambda i: jnp.where(i == 0, 0, jnp.where(i == n_groups - 1, 2, 1))
    return pl.pallas_call(
        functools.partial(_nat_kernel, n_rows),
        grid=(B, NAT_PAIRS, n_groups),
        in_specs=[qo, kv, kv,
                  pl.BlockSpec((1, 2, 1, NAT_GROUP * NAT_KEY_ROWS // 2, GRID_W, 2 * GRID_W),
                               lambda b, p, i: (p, 0, kind(i), 0, 0, 0))],
        out_specs=qo,
        scratch_shapes=[pltpu.VMEM((NAT_GROUP * GRID_W, NAT_KEY_ROWS * GRID_W), F32)],
        out_shape=jax.ShapeDtypeStruct((B, L, NAT_WIDTH), BF16),
        compiler_params=pltpu.CompilerParams(
            dimension_semantics=("arbitrary", "arbitrary", "arbitrary"),
            vmem_limit_bytes=VMEM_LIMIT),
        name="nat_attn",
    )(nq, nk, nv, tbl)


def _out_kernel(od_ref, on_ref, x_ref, wt_ref, wb_ref, gpost_ref, gt1_ref, gpre_ref, sc2_ref, sh2_ref,
                wr_ref, br_ref, x1_ref, h2_ref, eidx_ref, gate_ref, rank_ref, cnt_ref, carry_ref):
    tm = x_ref.shape[0]

    @pl.when(pl.program_id(0) == 0)
    def _():
        carry_ref[...] = jnp.zeros_like(carry_ref)

    mix = (jnp.dot(od_ref[...], wt_ref[...], preferred_element_type=F32)
           + jnp.dot(on_ref[...], wb_ref[...], preferred_element_type=F32))
    x1 = x_ref[...] + gt1_ref[0] * (_rms(mix) * gpost_ref[...])
    x1_ref[...] = x1
    h2 = _rms(x1) * gpre_ref[...]
    h2 = h2 * (1.0 + sc2_ref[0]) + sh2_ref[0]
    for s in range(ROW_TILE):
        h2_ref[pl.ds(s, tm, stride=ROW_TILE), :] = h2[:, s * 128:(s + 1) * 128]

    wr = wr_ref[...]
    h_hi = h2.astype(BF16)
    h_lo = (h2 - h_hi.astype(F32)).astype(BF16)
    w_hi = wr.astype(BF16)
    w_lo = (wr - w_hi.astype(F32)).astype(BF16)
    logits = (jnp.dot(h_hi, w_hi, preferred_element_type=F32)
              + (jnp.dot(h_hi, w_lo, preferred_element_type=F32)
                 + jnp.dot(h_lo, w_hi, preferred_element_type=F32))) + br_ref[...]
    eio = lax.broadcasted_iota(I32, logits.shape, 1).astype(F32)
    onehot = jnp.zeros_like(logits)
    vals, idxs, sels = [], [], []
    cur = logits
    for _ in range(TOP_K):
        mx = jnp.max(cur, axis=-1, keepdims=True)
        idx = jnp.min(jnp.where(cur == mx, eio, float(N_EXPERTS)), axis=-1, keepdims=True)
        sel = eio == idx
        vals.append(mx)
        idxs.append(idx)
        sels.append(sel)
        cur = jnp.where(sel, -jnp.inf, cur)
        onehot = onehot + sel.astype(F32)

    ex = [jnp.exp(v - vals[0]) for v in vals]
    tot = ex[0] + ex[1] + ex[2] + ex[3]

    rr = lax.broadcasted_iota(I32, (tm, tm), 0)
    cc = lax.broadcasted_iota(I32, (tm, tm), 1)
    tri = (rr > cc).astype(BF16)
    carry = carry_ref[...]
    cum = jnp.dot(tri, onehot.astype(BF16), preferred_element_type=F32) + carry
    ranks = [jnp.sum(jnp.where(sel, cum, 0.0), axis=-1, keepdims=True) for sel in sels]
    carry = carry + jnp.sum(onehot, axis=0, keepdims=True)
    carry_ref[...] = carry
    cnt_ref[...] = carry

    kio = lax.broadcasted_iota(I32, (tm, TOP_K), 1)

    def pack(cols):
        out = jnp.broadcast_to(cols[TOP_K - 1], (tm, TOP_K))
        for k in range(TOP_K - 2, -1, -1):
            out = jnp.where(kio == k, cols[k], out)
        return out

    eidx_ref[...] = pack(idxs).astype(I32)
    gate_ref[...] = pack([e / tot for e in ex])
    rank_ref[...] = pack(ranks).astype(I32)


def _out_router(od, on, x, w_out, g_post, gt1, g_pre, sc2, sh2, w_router, b_router, tokens_per_batch):
    T, D = x.shape
    B = gt1.shape[0]
    tm = min(OUT_TM, tokens_per_batch)
    steps_per_batch = tokens_per_batch // tm
    wt = w_out[:DIFF_WIDTH].astype(BF16)
    wb = w_out[DIFF_WIDTH:].astype(BF16)
    rowblk = lambda w: pl.BlockSpec((tm, w), lambda i: (i, 0))
    const = lambda shape: pl.BlockSpec(shape, lambda i: (0,) * len(shape))
    modv = pl.BlockSpec((1, 1, D), lambda i: (i // steps_per_batch, 0, 0))
    return pl.pallas_call(
        _out_kernel,
        grid=(T // tm,),
        in_specs=[rowblk(DIFF_WIDTH), rowblk(NAT_WIDTH), rowblk(D),
                  const((DIFF_WIDTH, D)), const((NAT_WIDTH, D)), const((1, D)), modv, const((1, D)),
                  modv, modv, const((D, N_EXPERTS)), const((1, N_EXPERTS))],
        out_specs=[rowblk(D), pl.BlockSpec((tm * ROW_TILE, 128), lambda i: (i, 0)),
                   rowblk(TOP_K), rowblk(TOP_K), rowblk(TOP_K), const((1, N_EXPERTS))],
        out_shape=[jax.ShapeDtypeStruct((T, D), F32), jax.ShapeDtypeStruct((T * ROW_TILE, 128), F32),
                   jax.ShapeDtypeStruct((T, TOP_K), I32), jax.ShapeDtypeStruct((T, TOP_K), F32),
                   jax.ShapeDtypeStruct((T, TOP_K), I32), jax.ShapeDtypeStruct((1, N_EXPERTS), F32)],
        scratch_shapes=[pltpu.VMEM((1, N_EXPERTS), F32)],
        compiler_params=pltpu.CompilerParams(dimension_semantics=("arbitrary",),
                                             vmem_limit_bytes=VMEM_LIMIT),
        name="out_router",
    )(od, on, x, wt, wb, g_post.reshape(1, D), gt1.reshape(B, 1, D), g_pre.reshape(1, D),
      sc2.reshape(B, 1, D), sh2.reshape(B, 1, D), w_router, b_router.reshape(1, N_EXPERTS))


def _expert_kernel(spare_base, be_ref, tok_ref, tok_next_ref, dst_prev_ref, dst_ref, h2_hbm, w1_ref, b1g_ref, b1l_ref,
                   w2_ref, b2_ref, y_hbm, xbuf, ybuf, xb, w1t, w2t, hh, gsem, ssem):
    i = pl.program_id(0)
    n = pl.num_programs(0)
    bm = xbuf.shape[1] // ROW_TILE

    def tile_rows(t):
        t = t * ROW_TILE
        return pl.ds(t if isinstance(t, int) else pl.multiple_of(t, ROW_TILE), ROW_TILE)
    ff = w2_ref.shape[1]
    slot = i % 2

    def gather_row(idx_ref, s, r):
        return pltpu.make_async_copy(h2_hbm.at[tile_rows(idx_ref[0, 0, r]), :],
                                     xbuf.at[s, tile_rows(r), :], gsem.at[s])

    def scatter_row(idx_ref, s, r):
        return pltpu.make_async_copy(ybuf.at[s, tile_rows(r), :],
                                     y_hbm.at[tile_rows(idx_ref[0, 0, r]), :], ssem.at[s])

    def wait_gather(s):
        pltpu.make_async_copy(h2_hbm.at[pl.ds(0, bm * ROW_TILE), :], xbuf.at[s], gsem.at[s]).wait()

    def wait_scatter(s):
        pltpu.make_async_copy(ybuf.at[s], y_hbm.at[pl.ds(0, bm * ROW_TILE), :], ssem.at[s]).wait()

    @pl.when(i == 0)
    def _():
        ybuf[...] = jnp.zeros(ybuf.shape, F32)
        for r in range(bm):
            pltpu.make_async_copy(ybuf.at[0, tile_rows(r), :], y_hbm.at[tile_rows(spare_base + r), :],
                                  ssem.at[0]).start(priority=r % 2)
        for r in range(bm):
            gather_row(tok_ref, 0, r).start(priority=r % 2)

    @pl.when((i == 0) | (be_ref[i] != be_ref[jnp.maximum(i - 1, 0)]))
    def _():
        cw = 256
        for c in range(w1_ref.shape[2] // cw):
            w1t[c * cw:(c + 1) * cw, :] = w1_ref[0, :, c * cw:(c + 1) * cw].T.astype(BF16)
        for c in range(w2_ref.shape[2] // cw):
            w2t[c * cw:(c + 1) * cw, :] = w2_ref[0, :, c * cw:(c + 1) * cw].T.astype(BF16)

    wait_gather(slot)
    for s8 in range(ROW_TILE):
        xb[:, s8 * 128:(s8 + 1) * 128] = xbuf[slot, pl.ds(s8, bm, stride=ROW_TILE), :].astype(BF16)

    n_lane_tiles = bm // 128

    has_rows = i < be_ref[n]

    def start_gathers():
        for r in range(bm):
            gather_row(tok_next_ref, 1 - slot, r).start(priority=r % 2)

    def start_scatters():
        for r in range(bm):
            scatter_row(dst_prev_ref, 1 - slot, r).start(priority=r % 2)

    @pl.when(has_rows)
    def _():
        start_gathers()
        hh_t = lax.dot_general(w1t[...], xb[...], NT_DIMS, preferred_element_type=F32)
        for j in range(n_lane_tiles):
            hh[j] = hh_t[:, j * 128:(j + 1) * 128]

    @pl.when(jnp.logical_not(has_rows))
    def _():
        start_gathers()

    @pl.when(has_rows)
    def _():
        start_scatters()
        even = jnp.concatenate([hh[j, pl.ds(0, ff, stride=2), :] for j in range(n_lane_tiles)], axis=1)
        odd = jnp.concatenate([hh[j, pl.ds(1, ff, stride=2), :] for j in range(n_lane_tiles)], axis=1)
        glu = jnp.minimum(even + b1g_ref[0], SWIGLU_LIMIT)
        lin = jnp.clip(odd + b1l_ref[0], -SWIGLU_LIMIT, SWIGLU_LIMIT)
        act = glu * (1.0 / (1.0 + jnp.exp(-SWIGLU_ALPHA * glu))) * (lin + 1.0)
        y_t = jnp.dot(w2t[...], act.astype(BF16), preferred_element_type=F32)
        y = y_t.T + b2_ref[0]
        wait_scatter(slot)
        for s8 in range(ROW_TILE):
            ybuf[slot, pl.ds(s8, bm, stride=ROW_TILE), :] = y[:, s8 * 128:(s8 + 1) * 128]

    @pl.when(jnp.logical_not(has_rows))
    def _():
        start_scatters()
        wait_scatter(slot)

    @pl.when(i == n - 1)
    def _():
        wait_scatter(1 - slot)
        for r in range(bm):
            scatter_row(dst_ref, slot, r).start(priority=r % 2)
        wait_scatter(slot)
        wait_gather(1 - slot)


def _experts(h2, blk_expert, n_used, tok_buf, dst_buf, w1, b1, w2, b2, n_rows_out):
    D = w1.shape[1]
    assert D == ROW_TILE * 128 and h2.shape[1] == 128
    n_blocks = blk_expert.shape[0]
    bm = MOE_BM
    F = w2.shape[1]
    b1g = b1[:, 0::2].reshape(N_EXPERTS, F, 1)
    b1l = b1[:, 1::2].reshape(N_EXPERTS, F, 1)
    tok3 = tok_buf.reshape(n_blocks, 1, bm)
    spare = n_rows_out - 2 * bm + jnp.arange(bm, dtype=I32)
    dst3 = jnp.concatenate([spare, dst_buf]).reshape(n_blocks + 1, 1, bm)
    smem_blk = lambda fn: pl.BlockSpec((1, 1, bm), fn, memory_space=pltpu.SMEM)
    grid_spec = pltpu.PrefetchScalarGridSpec(
        num_scalar_prefetch=1,
        grid=(n_blocks,),
        in_specs=[smem_blk(lambda i, be: (i, 0, 0)),
                  smem_blk(lambda i, be: (jnp.minimum(i + 1, n_blocks - 1), 0, 0)),
                  smem_blk(lambda i, be: (i, 0, 0)),
                  smem_blk(lambda i, be: (i + 1, 0, 0)),
                  pl.BlockSpec(memory_space=pl.ANY),
                  pl.BlockSpec((1, D, 2 * F), lambda i, be: (be[i], 0, 0)),
                  pl.BlockSpec((1, F, 1), lambda i, be: (be[i], 0, 0)),
                  pl.BlockSpec((1, F, 1), lambda i, be: (be[i], 0, 0)),
                  pl.BlockSpec((1, F, D), lambda i, be: (be[i], 0, 0)),
                  pl.BlockSpec((1, 1, D), lambda i, be: (be[i], 0, 0))],
        out_specs=pl.BlockSpec(memory_space=pl.ANY),
        scratch_shapes=[pltpu.VMEM((2, bm * ROW_TILE, 128), F32), pltpu.VMEM((2, bm * ROW_TILE, 128), F32),
                        pltpu.VMEM((bm, D), BF16),
                        pltpu.VMEM((2 * F, D), BF16), pltpu.VMEM((D, F), BF16),
                        pltpu.VMEM((bm // 128, 2 * F, 128), F32),
                        pltpu.SemaphoreType.DMA((2,)), pltpu.SemaphoreType.DMA((2,))],
    )
    return pl.pallas_call(
        functools.partial(_expert_kernel, n_rows_out - bm),
        grid_spec=grid_spec,
        out_shape=jax.ShapeDtypeStruct((n_rows_out * ROW_TILE, 128), F32),
        compiler_params=pltpu.CompilerParams(dimension_semantics=("arbitrary",),
                                             vmem_limit_bytes=EXPERT_VMEM_LIMIT),
        name="experts",
    )(jnp.concatenate([blk_expert, n_used.reshape(1)]), tok3, tok3, dst3, dst3, h2, w1, b1g, b1l, w2,
      b2.reshape(N_EXPERTS, 1, D))


def _combine_kernel(y0_ref, y1_ref, y2_ref, y3_ref, gate_ref, x1_ref, gt2_ref, g_ref, o_ref):
    tm = x1_ref.shape[0]
    gates = gate_ref[...]

    def rows(y_ref):
        return jnp.concatenate([y_ref[pl.ds(s, tm, stride=ROW_TILE), :] for s in range(ROW_TILE)], axis=1)

    f = gates[:, 0:1] * rows(y0_ref)
    for k, y_ref in enumerate((y1_ref, y2_ref, y3_ref), start=1):
        f = f + gates[:, k:k + 1] * rows(y_ref)
    o_ref[...] = x1_ref[...] + gt2_ref[0] * (_rms(f) * g_ref[...])


def _combine(y_tok, gates, x1, gt2, g_post, tokens_per_batch):
    T, D = x1.shape
    B = gt2.shape[0]
    tm = min(COMBINE_TM, tokens_per_batch)
    steps_per_batch = tokens_per_batch // tm
    steps = T // tm
    y_spec = lambda k: pl.BlockSpec((tm * ROW_TILE, 128), lambda i: (k * steps + i, 0))
    return pl.pallas_call(
        _combine_kernel,
        grid=(steps,),
        in_specs=[y_spec(0), y_spec(1), y_spec(2), y_spec(3),
                  pl.BlockSpec((tm, TOP_K), lambda i: (i, 0)),
                  pl.BlockSpec((tm, D), lambda i: (i, 0)),
                  pl.BlockSpec((1, 1, D), lambda i: (i // steps_per_batch, 0, 0)),
                  pl.BlockSpec((1, D), lambda i: (0, 0))],
        out_specs=pl.BlockSpec((tm, D), lambda i: (i, 0)),
        out_shape=jax.ShapeDtypeStruct((T, D), F32),
        compiler_params=pltpu.CompilerParams(dimension_semantics=("arbitrary",),
                                             vmem_limit_bytes=VMEM_LIMIT),
        name="combine",
    )(y_tok, y_tok, y_tok, y_tok, gates, x1, gt2.reshape(B, 1, D), g_post.reshape(1, D))


def _inverse_kernel(pad_lo_ref, pad_hi_ref, dest_ref, inv_ref):
    i = pl.program_id(0)
    chunk = dest_ref.shape[2]

    @pl.when(i == 0)
    def _():
        def clear(s, carry):
            inv_ref[s] = 0
            return carry
        for e in range(pad_lo_ref.shape[0]):
            lax.fori_loop(pad_lo_ref[e], pad_hi_ref[e], clear, 0)

    @pl.when(i > 0)
    def _():
        base = (i - 1) * chunk

        def put(a, carry):
            inv_ref[dest_ref[0, 0, a]] = base + a + 1
            return carry
        lax.fori_loop(0, chunk, put, 0, unroll=8)


def _inverse_map(dest, pad_lo, pad_hi, n_slots):
    n_assign = dest.shape[0]
    chunk = math.gcd(n_assign, INVERSE_CHUNK)
    n_put = n_assign // chunk
    grid_spec = pltpu.PrefetchScalarGridSpec(
        num_scalar_prefetch=2,
        grid=(1 + n_put,),
        in_specs=[pl.BlockSpec((1, 1, chunk), lambda i, lo, hi: (jnp.maximum(i - 1, 0), 0, 0),
                               memory_space=pltpu.SMEM)],
        out_specs=pl.BlockSpec(memory_space=pltpu.SMEM),
    )
    return pl.pallas_call(
        _inverse_kernel,
        grid_spec=grid_spec,
        out_shape=jax.ShapeDtypeStruct((n_slots,), I32),
        compiler_params=pltpu.CompilerParams(dimension_semantics=("arbitrary",)),
        name="inverse_map",
    )(pad_lo, pad_hi, dest.reshape(n_put, 1, chunk))


def _dispatch_plan(eidx, rank, counts):
    T = eidx.shape[0]
    n_assign = eidx.size
    bm = MOE_BM
    cap = (n_assign + N_EXPERTS * (bm - 1) + bm - 1) // bm * bm
    n_blocks = cap // bm
    counts = counts.reshape(N_EXPERTS).astype(I32)
    padded = (counts + bm - 1) // bm * bm
    pad_end = jnp.cumsum(padded)
    pad_start = pad_end - padded
    eio = jnp.arange(N_EXPERTS, dtype=I32)
    start_of = jnp.sum(jnp.where(eidx[..., None] == eio, pad_start, 0), axis=-1)
    dest = (start_of + rank).reshape(-1)
    blk_start = jnp.arange(n_blocks, dtype=I32) * bm
    blk_expert = jnp.minimum(jnp.sum((blk_start[:, None] >= pad_end[None, :]).astype(I32), axis=-1),
                             N_EXPERTS - 1)
    pad_lo = jnp.concatenate([pad_start + counts, pad_end[-1:]]).astype(I32)
    pad_hi = jnp.concatenate([pad_end, jnp.full((1,), cap, I32)]).astype(I32)
    inv = _inverse_map(dest, pad_lo, pad_hi, cap)
    is_pad = inv == 0
    a = inv - 1
    tok_buf = jnp.where(is_pad, 0, a // TOP_K).astype(I32)
    pad_row = n_assign + jnp.cumsum(is_pad.astype(I32)) - 1
    dst_buf = jnp.where(is_pad, pad_row, (a % TOP_K) * T + a // TOP_K).astype(I32)
    n_used = (pad_end[-1] // bm).astype(I32)
    return blk_expert, n_used, tok_buf, dst_buf, cap + 2 * bm


def _layer(x, c, l, w_ada, b_ada, g_pre_mix, g_post_mix, w_in, w_out, lam_q1, lam_k1, lam_q2, lam_k2,
           g_subln, nat_rpb, g_pre_ffn, g_post_ffn, w_router, b_router, w1, b1, w2, b2):
    B, L, D = x.shape
    lam_init = 0.8 - 0.6 * math.exp(-0.3 * l)
    mod, lam = _ada(c, w_ada, b_ada, lam_q1, lam_k1, lam_q2, lam_k2, lam_init)
    sh1, sc1, gt1, sh2, sc2, gt2 = jnp.split(mod, 6, axis=-1)

    qT, kd, vT, nq, nk, nv = _inproj(x, g_pre_mix, sc1, sh1, w_in)
    o_diff = _diff_attention(qT, kd, vT, lam, g_subln, lam_init)
    o_nat = _nat_attention(nq, nk, nv, nat_rpb)

    T = B * L
    x1, h2, eidx, gates, rank, counts = _out_router(
        o_diff.reshape(T, DIFF_WIDTH), o_nat.reshape(T, NAT_WIDTH), x.reshape(T, D), w_out,
        g_post_mix, gt1, g_pre_ffn, sc2, sh2, w_router, b_router, L)
    blk_expert, n_used, tok_buf, dst_buf, n_rows_out = _dispatch_plan(eidx, rank, counts)
    y_tok = _experts(h2, blk_expert, n_used, tok_buf, dst_buf, w1, b1, w2, b2, n_rows_out)
    out = _combine(y_tok, gates, x1, gt2, g_post_ffn, L)
    return out.reshape(B, L, D)


def kernel(x, c, w_ada, b_ada, g_pre_mix, g_post_mix, w_in, w_out, lam_q1, lam_k1, lam_q2, lam_k2,
           g_subln, nat_rpb, g_pre_ffn, g_post_ffn, w_router, b_router, w1, b1, w2, b2):
    depth = w_ada.shape[0]
    for l in range(depth):
        x = _layer(x, c, l, w_ada[l], b_ada[l], g_pre_mix[l], g_post_mix[l], w_in[l], w_out[l],
                   lam_q1[l], lam_k1[l], lam_q2[l], lam_k2[l], g_subln[l], nat_rpb[l],
                   g_pre_ffn[l], g_post_ffn[l], w_router[l], b_router[l], w1[l], b1[l], w2[l], b2[l])
    return x
```

```python
import functools
import math

import jax
import jax.numpy as jnp
from jax import lax
from jax.experimental import pallas as pl
from jax.experimental.pallas import tpu as pltpu

F32 = jnp.float32
BF16 = jnp.bfloat16
I32 = jnp.int32

HEAD_DIM = 64
N_DIFF_HEADS = 4
DIFF_HEAD_W = 2 * HEAD_DIM
DIFF_WIDTH = N_DIFF_HEADS * DIFF_HEAD_W
N_NAT_HEADS = 8
NAT_WIDTH = N_NAT_HEADS * HEAD_DIM
NAT_PAIRS = N_NAT_HEADS // 2
GRID_W = 64
NAT_KH = 8
NAT_KW = 16
N_EXPERTS = 32
TOP_K = 4
SWIGLU_LIMIT = 7.0
SWIGLU_ALPHA = 1.702
RMS_EPS = 1e-6
NEG_BIG = -1e30
LOG2E = 1.4426950408889634
ROW_TILE = 8

NT_DIMS = (((1,), (1,)), ((), ()))

ADA_TN = 1536
INPROJ_TM = 512
DIFF_TQ = 256
DIFF_TK = 512
DIFF_TICKS_PER_BODY = 3
NAT_GROUP = 8
NAT_KEY_ROWS = 16
OUT_TM = 512
MOE_BM = 256
COMBINE_TM = 512
INVERSE_CHUNK = 4096
INVERSE_GROUP = 16
VMEM_LIMIT = 48 * 1024 * 1024
EXPERT_VMEM_LIMIT = 56 * 1024 * 1024


def _rms(x, axis=-1):
    return x * lax.rsqrt(jnp.mean(x * x, axis=axis, keepdims=True) + RMS_EPS)


def _ada_kernel(lam_init, c_ref, w_ref, b_ref, lq1_ref, lk1_ref, lq2_ref, lk2_ref, mod_ref, lam_ref):
    c = c_ref[...]
    s = c * (1.0 / (1.0 + jnp.exp(-c)))
    mod_ref[...] = jnp.dot(s, w_ref[...], preferred_element_type=F32,
                           precision=lax.Precision.HIGHEST) + b_ref[...]
    d1 = jnp.sum(lq1_ref[...] * lk1_ref[...], axis=-1, keepdims=True)
    d2 = jnp.sum(lq2_ref[...] * lk2_ref[...], axis=-1, keepdims=True)
    lam = jnp.exp(d1) - jnp.exp(d2) + lam_init
    lam_ref[...] = jnp.broadcast_to(lam, lam_ref.shape)


def _ada(c, w_ada, b_ada, lq1, lk1, lq2, lk2, lam_init):
    B, D = c.shape
    N = w_ada.shape[1]
    c8 = jnp.zeros((8, D), F32).at[:B].set(c)
    vec = pl.BlockSpec((1, HEAD_DIM), lambda j: (0, 0))
    mod, lam = pl.pallas_call(
        functools.partial(_ada_kernel, lam_init),
        grid=(N // ADA_TN,),
        in_specs=[pl.BlockSpec((8, D), lambda j: (0, 0)),
                  pl.BlockSpec((D, ADA_TN), lambda j: (0, j)),
                  pl.BlockSpec((1, ADA_TN), lambda j: (0, j)),
                  vec, vec, vec, vec],
        out_specs=[pl.BlockSpec((8, ADA_TN), lambda j: (0, j)),
                   pl.BlockSpec((8, 128), lambda j: (0, 0))],
        out_shape=[jax.ShapeDtypeStruct((8, N), F32), jax.ShapeDtypeStruct((8, 128), F32)],
        compiler_params=pltpu.CompilerParams(dimension_semantics=("arbitrary",),
                                             vmem_limit_bytes=VMEM_LIMIT),
        name="ada",
    )(c8, w_ada, b_ada.reshape(1, N), lq1.reshape(1, -1), lk1.reshape(1, -1),
      lq2.reshape(1, -1), lk2.reshape(1, -1))
    return mod[:B], lam


def _inproj_kernel(x_ref, g_ref, sc_ref, sh_ref, wqT_ref, wvT_ref, wn_ref,
                   qT_ref, k_ref, vT_ref, nq_ref, nk_ref, nv_ref):
    h = _rms(x_ref[0]) * g_ref[...]
    h = h * (1.0 + sc_ref[0]) + sh_ref[0]
    hb = h.astype(BF16)
    qT_ref[0] = lax.dot_general(wqT_ref[...], hb, NT_DIMS, preferred_element_type=F32).astype(BF16)
    vT_ref[0] = lax.dot_general(wvT_ref[...], hb, NT_DIMS, preferred_element_type=F32).astype(BF16)
    rest = jnp.dot(hb, wn_ref[...], preferred_element_type=F32).astype(BF16)
    k_ref[0] = rest[:, 0:512]
    nq_ref[0] = rest[:, 512:1024]
    nk_ref[0] = rest[:, 1024:1536]
    nv_ref[0] = rest[:, 1536:2048]


def _inproj(x, g_pre, sc1, sh1, w_in):
    B, L, D = x.shape
    tm = INPROJ_TM
    scale = HEAD_DIM ** -0.5
    wqT = (w_in[:, 0:512] * (scale * LOG2E)).T.astype(BF16)
    wvT = w_in[:, 1024:1536].T.astype(BF16)
    wn = jnp.concatenate([w_in[:, 512:1024], w_in[:, 1536:2048] * scale, w_in[:, 2048:3072]],
                         axis=1).astype(BF16)
    row_major = pl.BlockSpec((1, tm, 512), lambda b, i: (b, i, 0))
    col_major = pl.BlockSpec((1, 512, tm), lambda b, i: (b, 0, i))
    modv = pl.BlockSpec((1, 1, D), lambda b, i: (b, 0, 0))
    rm_shape = jax.ShapeDtypeStruct((B, L, 512), BF16)
    cm_shape = jax.ShapeDtypeStruct((B, 512, L), BF16)
    return pl.pallas_call(
        _inproj_kernel,
        grid=(B, L // tm),
        in_specs=[pl.BlockSpec((1, tm, D), lambda b, i: (b, i, 0)),
                  pl.BlockSpec((1, D), lambda b, i: (0, 0)),
                  modv, modv,
                  pl.BlockSpec((512, D), lambda b, i: (0, 0)),
                  pl.BlockSpec((512, D), lambda b, i: (0, 0)),
                  pl.BlockSpec((D, 2048), lambda b, i: (0, 0))],
        out_specs=[col_major, row_major, col_major, row_major, row_major, row_major],
        out_shape=[cm_shape, rm_shape, cm_shape, rm_shape, rm_shape, rm_shape],
        compiler_params=pltpu.CompilerParams(dimension_semantics=("arbitrary", "arbitrary"),
                                             vmem_limit_bytes=VMEM_LIMIT),
        name="inproj",
    )(x, g_pre.reshape(1, D), sc1.reshape(B, 1, D), sh1.reshape(B, 1, D), wqT, wvT, wn)


def _diff_kernel(lam_init, n_kchunks, coef_ref, lam_ref, qT_ref, k_ref, kaug_ref, vT_ref, g_ref, o_ref,
                 acc_ref, m_ref, l_ref, s0_ref, qd_ref, qm_ref, cq_ref, s_ref, p_ref, a_ref):
    tq = qT_ref.shape[2]
    tk = s0_ref.shape[0]
    h = pl.program_id(1)
    q0 = pl.program_id(2) * tq
    c_f = coef_ref[4 * h]
    c_parts = (coef_ref[4 * h + 1], coef_ref[4 * h + 2], coef_ref[4 * h + 3])
    c_diag = lax.div(q0, tk)

    qT = qT_ref[0]
    row = lax.broadcasted_iota(I32, qT.shape, 0)
    zero = jnp.zeros_like(qT)
    q_maps = (jnp.where(row < HEAD_DIM, qT, zero), jnp.where(row >= HEAD_DIM, qT, zero))
    aug = jnp.zeros(qT.shape, F32)
    for j in range(3):
        aug = jnp.where(row == j, 64.0 * c_parts[j], aug)
        aug = jnp.where(row == 3 + j, c_parts[j], aug)
    for mi in range(2):
        qd_ref[mi] = q_maps[mi]
        for sg, sign in enumerate((1.0, -1.0)):
            qm_ref[mi, sg, 0:DIFF_HEAD_W, :] = q_maps[mi]
            qm_ref[mi, sg, DIFF_HEAD_W:, :] = (sign * aug).astype(BF16)

    qpos = q0 + lax.broadcasted_iota(I32, (1, tq), 1)
    cq_ref[...] = c_f * qpos.astype(F32)
    kk = lax.broadcasted_iota(I32, (tk, tq), 0)
    qq = lax.broadcasted_iota(I32, (tk, tq), 1)
    s0_ref[...] = c_f * (kk - qq).astype(F32)
    acc_ref[...] = jnp.zeros_like(acc_ref)
    l_ref[...] = jnp.zeros_like(l_ref)
    m_ref[...] = jnp.full_like(m_ref, NEG_BIG)

    def chunk(i):
        j = i - 1
        c = j + (j >= c_diag).astype(I32)
        return c_diag if isinstance(i, int) and i == 0 else jnp.where(i == 0, c_diag, c)

    def start(c):
        return pl.multiple_of(c * tk, tk)

    def update(mi, slot, logits, shift):
        m_old = m_ref[mi]
        m_new = jnp.maximum(m_old, jnp.max(logits, axis=0, keepdims=True) - shift)
        alpha = jnp.exp2(m_old - m_new)
        p = jnp.exp2(logits - (m_new + shift))
        l_ref[mi] = alpha * l_ref[mi] + jnp.sum(p, axis=0, keepdims=True)
        m_ref[mi] = m_new
        a_ref[slot, mi] = alpha
        p_ref[slot, mi] = p.astype(BF16)

    def scores_diag(slot):
        kb = k_ref[0, pl.ds(start(c_diag), tk), :]
        for mi in range(2):
            s_ref[slot, mi] = jnp.dot(kb, qd_ref[mi], preferred_element_type=F32)

    def softmax_diag(slot):
        bias = jnp.abs(s0_ref[...] + c_f * (start(c_diag) - q0).astype(F32))
        for mi in range(2):
            update(mi, slot, s_ref[slot, mi] - bias, 0.0)

    def scores(c, slot):
        k0 = start(c)
        kb = jnp.concatenate([k_ref[0, pl.ds(k0, tk), :], kaug_ref[pl.ds(k0, tk), :]], axis=1)
        sg = (c > c_diag).astype(I32)
        for mi in range(2):
            s_ref[slot, mi] = jnp.dot(kb, qm_ref[mi, sg], preferred_element_type=F32)

    def softmax(c, slot):
        shift = jnp.where(c > c_diag, -1.0, 1.0) * cq_ref[...]
        for mi in range(2):
            update(mi, slot, s_ref[slot, mi], shift)

    def values(c, slot):
        vb = vT_ref[0, :, pl.ds(start(c), tk)]
        for mi in range(2):
            acc_ref[mi] = a_ref[slot, mi] * acc_ref[mi] + jnp.dot(vb, p_ref[slot, mi],
                                                                  preferred_element_type=F32)

    n = n_kchunks
    scores_diag(0)
    scores(chunk(1), 1)
    scores(chunk(2), 2)
    softmax_diag(0)
    scores(chunk(3), 0)
    softmax(chunk(1), 1)

    def ticks(j, carry):
        t0 = DIFF_TICKS_PER_BODY * j + 2
        for k in range(DIFF_TICKS_PER_BODY):
            t = t0 + k
            scores(chunk(t + 2), (k + 1) % 3)
            softmax(chunk(t), (k + 2) % 3)
            values(chunk(t - 2), k % 3)
        return carry

    lax.fori_loop(0, (n - 4) // DIFF_TICKS_PER_BODY, ticks, 0)
    for t in (n - 2, n - 1):
        softmax(chunk(t), t % 3)
        values(chunk(t - 2), (t - 2) % 3)
    values(chunk(n - 2), (n - 2) % 3)
    values(chunk(n - 1), (n - 1) % 3)

    lam = lam_ref[0:1, 0:1]
    o = acc_ref[0] / l_ref[0] - lam * (acc_ref[1] / l_ref[1])
    y = _rms(o, axis=0) * g_ref[...] * (1.0 - lam_init)
    o_ref[0] = y.T.astype(BF16)


def _diff_attention(qT, k, vT, lam, g_sub, lam_init):
    B, _, L = qT.shape
    tq, tk = min(DIFF_TQ, L), min(DIFF_TK, L)
    assert DIFF_TICKS_PER_BODY % 3 == 0 and L // tk >= 4 and (L // tk - 4) % DIFF_TICKS_PER_BODY == 0
    assert tk % tq == 0 and L <= 64 * 256
    i = jnp.arange(1, N_DIFF_HEADS + 1, dtype=F32)
    c = jnp.exp2(-8.0 * i / N_DIFF_HEADS) * LOG2E
    c1 = c.astype(BF16).astype(F32)
    c2 = (c - c1).astype(BF16).astype(F32)
    c3 = (c - c1 - c2).astype(BF16).astype(F32)
    coef = jnp.stack([c, c1, c2, c3], axis=1).reshape(-1)
    kpos = jnp.arange(L, dtype=I32)
    lane = jnp.arange(DIFF_HEAD_W, dtype=I32)
    kaug = jnp.where(lane[None, :] < 3, (kpos // 64)[:, None],
                     jnp.where(lane[None, :] < 6, (kpos % 64)[:, None], 0)).astype(BF16)
    return pl.pallas_call(
        functools.partial(_diff_kernel, lam_init, L // tk),
        grid=(B, N_DIFF_HEADS, L // tq),
        in_specs=[pl.BlockSpec(memory_space=pltpu.SMEM),
                  pl.BlockSpec((8, 128), lambda b, h, i: (0, 0)),
                  pl.BlockSpec((1, DIFF_HEAD_W, tq), lambda b, h, i: (b, h, i)),
                  pl.BlockSpec((1, L, DIFF_HEAD_W), lambda b, h, i: (b, 0, h)),
                  pl.BlockSpec((L, DIFF_HEAD_W), lambda b, h, i: (0, 0)),
                  pl.BlockSpec((1, DIFF_HEAD_W, L), lambda b, h, i: (b, h, 0)),
                  pl.BlockSpec((DIFF_HEAD_W, 1), lambda b, h, i: (0, 0))],
        out_specs=pl.BlockSpec((1, tq, DIFF_HEAD_W), lambda b, h, i: (b, i, h)),
        out_shape=jax.ShapeDtypeStruct((B, L, DIFF_WIDTH), BF16),
        scratch_shapes=[pltpu.VMEM((2, DIFF_HEAD_W, tq), F32),
                        pltpu.VMEM((2, 1, tq), F32),
                        pltpu.VMEM((2, 1, tq), F32),
                        pltpu.VMEM((tk, tq), F32),
                        pltpu.VMEM((2, DIFF_HEAD_W, tq), BF16),
                        pltpu.VMEM((2, 2, 2 * DIFF_HEAD_W, tq), BF16),
                        pltpu.VMEM((1, tq), F32),
                        pltpu.VMEM((3, 2, tk, tq), F32),
                        pltpu.VMEM((3, 2, tk, tq), BF16),
                        pltpu.VMEM((3, 2, 1, tq), F32)],
        compiler_params=pltpu.CompilerParams(
            dimension_semantics=("arbitrary", "arbitrary", "arbitrary"),
            vmem_limit_bytes=VMEM_LIMIT),
        name="diff_attn",
    )(coef, lam, qT, k, kaug, vT, g_sub.reshape(DIFF_HEAD_W, 1))


def _nat_group_span(g, n_rows):
    r0 = g * NAT_GROUP
    lo, hi = 0, n_rows - NAT_KEY_ROWS
    kb0 = r0 - NAT_KH // 2
    kb0 = min(max(kb0, lo), hi) if isinstance(g, int) else jnp.clip(kb0, lo, hi)
    return r0, kb0


def _nat_bias_table(rpb, n_rows):
    c = jnp.arange(GRID_W)
    col_start = jnp.clip(c - NAT_KW // 2, 0, GRID_W - NAT_KW)
    col_in = (c[None, :] >= col_start[:, None]) & (c[None, :] < col_start[:, None] + NAT_KW)
    dc = jnp.clip(c[None, :] - c[:, None], -(NAT_KW - 1), NAT_KW - 1) + (NAT_KW - 1)
    rpb = rpb.astype(F32)
    cols = jnp.zeros(rpb.shape[:2] + (GRID_W, GRID_W), F32)
    for j in range(2 * NAT_KW - 1):
        cols = cols + jnp.where(dc == j, rpb[:, :, j][:, :, None, None], 0.0)
    cols = jnp.where(col_in[None, None], cols, NEG_BIG)
    n_dr = 2 * NAT_KH - 1
    neg = jnp.full_like(cols, NEG_BIG)
    tiles = jnp.concatenate([jnp.concatenate([cols[:, :n_dr - 1], cols[:, 1:]], axis=-1),
                             jnp.concatenate([neg, cols], axis=-1),
                             jnp.concatenate([cols, neg], axis=-1),
                             jnp.concatenate([neg[:, :1], neg[:, :1]], axis=-1)], axis=1)
    right0, left0, none = n_dr - 1, 2 * n_dr - 1, 3 * n_dr - 1
    n_groups = n_rows // NAT_GROUP
    idx = []
    for g in (0, 1, n_groups - 1):
        r0, kb0 = _nat_group_span(min(g, n_groups - 1), n_rows)
        for j in range(NAT_GROUP):
            r = r0 + j
            rs = min(max(r - NAT_KH // 2, 0), n_rows - NAT_KH)
            for m in range(NAT_KEY_ROWS // 2):
                k_l = kb0 + 2 * m
                in_l = rs <= k_l < rs + NAT_KH
                in_r = rs <= k_l + 1 < rs + NAT_KH
                d_l = k_l - r + (NAT_KH - 1)
                idx.append(d_l if in_l and in_r else right0 + d_l + 1 if in_r else left0 + d_l if in_l else none)
    tbl = jnp.take(tiles, jnp.asarray(idx, I32), axis=1)
    return tbl.reshape(NAT_PAIRS, 2, 3, NAT_GROUP * NAT_KEY_ROWS // 2, GRID_W, 2 * GRID_W)


def _nat_kernel(n_rows, q_ref, k_ref, v_ref, bias_ref, o_ref, sb_ref):
    _, kb0 = _nat_group_span(pl.program_id(2), n_rows)
    pairs = NAT_KEY_ROWS // 2
    k0 = pl.multiple_of(kb0 * GRID_W, GRID_W)
    win = NAT_KEY_ROWS * GRID_W
    kw = k_ref[0, pl.ds(k0, win), :]
    vw = v_ref[0, pl.ds(k0, win), :]
    q = q_ref[0]
    lane = lax.broadcasted_iota(I32, q.shape, 1)
    zero = jnp.zeros_like(q)
    outs = []
    for hh in range(2):
        keep = (lane < HEAD_DIM) if hh == 0 else (lane >= HEAD_DIM)
        s = lax.dot_general(jnp.where(keep, q, zero), kw, NT_DIMS, preferred_element_type=F32)
        for j in range(NAT_GROUP):
            rows = slice(j * GRID_W, (j + 1) * GRID_W)
            for m in range(pairs):
                cols = slice(m * 2 * GRID_W, (m + 1) * 2 * GRID_W)
                sb_ref[rows, cols] = s[rows, cols] + bias_ref[0, hh, 0, j * pairs + m]
        s = sb_ref[...]
        p = jnp.exp(s - jnp.max(s, axis=-1, keepdims=True))
        l = jnp.sum(p, axis=-1, keepdims=True)
        outs.append(jnp.dot(p.astype(BF16), vw, preferred_element_type=F32) / l)
    o_ref[0] = jnp.where(lane < HEAD_DIM, outs[0], outs[1]).astype(BF16)


def _nat_attention(nq, nk, nv, rpb):
    B, L, _ = nq.shape
    n_rows = L // GRID_W
    assert n_rows >= NAT_KEY_ROWS and n_rows % NAT_GROUP == 0
    n_groups = n_rows // NAT_GROUP
    tbl = _nat_bias_table(rpb, n_rows)
    kv = pl.BlockSpec((1, L, 2 * HEAD_DIM), lambda b, p, i: (b, 0, p))
    qo = pl.BlockSpec((1, NAT_GROUP * GRID_W, 2 * HEAD_DIM), lambda b, p, i: (b, i, p))
    kind = lambda i: jnp.where(i == 0, 0, jnp.where(i == n_groups - 1, 2, 1))
    return pl.pallas_call(
        functools.partial(_nat_kernel, n_rows),
        grid=(B, NAT_PAIRS, n_groups),
        in_specs=[qo, kv, kv,
                  pl.BlockSpec((1, 2, 1, NAT_GROUP * NAT_KEY_ROWS // 2, GRID_W, 2 * GRID_W),
                               lambda b, p, i: (p, 0, kind(i), 0, 0, 0))],
        out_specs=qo,
        scratch_shapes=[pltpu.VMEM((NAT_GROUP * GRID_W, NAT_KEY_ROWS * GRID_W), F32)],
        out_shape=jax.ShapeDtypeStruct((B, L, NAT_WIDTH), BF16),
        compiler_params=pltpu.CompilerParams(
            dimension_semantics=("arbitrary", "arbitrary", "arbitrary"),
            vmem_limit_bytes=VMEM_LIMIT),
        name="nat_attn",
    )(nq, nk, nv, tbl)


def _out_kernel(od_ref, on_ref, x_ref, wt_ref, wb_ref, gpost_ref, gt1_ref, gpre_ref, sc2_ref, sh2_ref,
                wr_ref, br_ref, x1_ref, h2_ref, eidx_ref, gate_ref, rank_ref, cnt_ref, carry_ref):
    tm = x_ref.shape[0]

    @pl.when(pl.program_id(0) == 0)
    def _():
        carry_ref[...] = jnp.zeros_like(carry_ref)

    mix = (jnp.dot(od_ref[...], wt_ref[...], preferred_element_type=F32)
           + jnp.dot(on_ref[...], wb_ref[...], preferred_element_type=F32))
    x1 = x_ref[...] + gt1_ref[0] * (_rms(mix) * gpost_ref[...])
    x1_ref[...] = x1
    h2 = _rms(x1) * gpre_ref[...]
    h2 = h2 * (1.0 + sc2_ref[0]) + sh2_ref[0]
    for s in range(ROW_TILE):
        h2_ref[pl.ds(s, tm, stride=ROW_TILE), :] = h2[:, s * 128:(s + 1) * 128]

    wr = wr_ref[...]
    h_hi = h2.astype(BF16)
    h_lo = (h2 - h_hi.astype(F32)).astype(BF16)
    w_hi = wr.astype(BF16)
    w_lo = (wr - w_hi.astype(F32)).astype(BF16)
    logits = (jnp.dot(h_hi, w_hi, preferred_element_type=F32)
              + (jnp.dot(h_hi, w_lo, preferred_element_type=F32)
                 + jnp.dot(h_lo, w_hi, preferred_element_type=F32))) + br_ref[...]
    eio = lax.broadcasted_iota(I32, logits.shape, 1).astype(F32)
    onehot = jnp.zeros_like(logits)
    vals, idxs, sels = [], [], []
    cur = logits
    for _ in range(TOP_K):
        mx = jnp.max(cur, axis=-1, keepdims=True)
        idx = jnp.min(jnp.where(cur == mx, eio, float(N_EXPERTS)), axis=-1, keepdims=True)
        sel = eio == idx
        vals.append(mx)
        idxs.append(idx)
        sels.append(sel)
        cur = jnp.where(sel, -jnp.inf, cur)
        onehot = onehot + sel.astype(F32)

    ex = [jnp.exp(v - vals[0]) for v in vals]
    tot = ex[0] + ex[1] + ex[2] + ex[3]

    rr = lax.broadcasted_iota(I32, (tm, tm), 0)
    cc = lax.broadcasted_iota(I32, (tm, tm), 1)
    tri = (rr > cc).astype(BF16)
    carry = carry_ref[...]
    cum = jnp.dot(tri, onehot.astype(BF16), preferred_element_type=F32) + carry
    ranks = [jnp.sum(jnp.where(sel, cum, 0.0), axis=-1, keepdims=True) for sel in sels]
    carry = carry + jnp.sum(onehot, axis=0, keepdims=True)
    carry_ref[...] = carry
    cnt_ref[...] = carry

    kio = lax.broadcasted_iota(I32, (tm, TOP_K), 1)

    def pack(cols):
        out = jnp.broadcast_to(cols[TOP_K - 1], (tm, TOP_K))
        for k in range(TOP_K - 2, -1, -1):
            out = jnp.where(kio == k, cols[k], out)
        return out

    eidx_ref[...] = pack(idxs).astype(I32)
    gate_ref[...] = pack([e / tot for e in ex])
    rank_ref[...] = pack(ranks).astype(I32)


def _out_router(od, on, x, w_out, g_post, gt1, g_pre, sc2, sh2, w_router, b_router, tokens_per_batch):
    T, D = x.shape
    B = gt1.shape[0]
    tm = min(OUT_TM, tokens_per_batch)
    steps_per_batch = tokens_per_batch // tm
    wt = w_out[:DIFF_WIDTH].astype(BF16)
    wb = w_out[DIFF_WIDTH:].astype(BF16)
    rowblk = lambda w: pl.BlockSpec((tm, w), lambda i: (i, 0))
    const = lambda shape: pl.BlockSpec(shape, lambda i: (0,) * len(shape))
    modv = pl.BlockSpec((1, 1, D), lambda i: (i // steps_per_batch, 0, 0))
    return pl.pallas_call(
        _out_kernel,
        grid=(T // tm,),
        in_specs=[rowblk(DIFF_WIDTH), rowblk(NAT_WIDTH), rowblk(D),
                  const((DIFF_WIDTH, D)), const((NAT_WIDTH, D)), const((1, D)), modv, const((1, D)),
                  modv, modv, const((D, N_EXPERTS)), const((1, N_EXPERTS))],
        out_specs=[rowblk(D), pl.BlockSpec((tm * ROW_TILE, 128), lambda i: (i, 0)),
                   rowblk(TOP_K), rowblk(TOP_K), rowblk(TOP_K), const((1, N_EXPERTS))],
        out_shape=[jax.ShapeDtypeStruct((T, D), F32), jax.ShapeDtypeStruct((T * ROW_TILE, 128), F32),
                   jax.ShapeDtypeStruct((T, TOP_K), I32), jax.ShapeDtypeStruct((T, TOP_K), F32),
                   jax.ShapeDtypeStruct((T, TOP_K), I32), jax.ShapeDtypeStruct((1, N_EXPERTS), F32)],
        scratch_shapes=[pltpu.VMEM((1, N_EXPERTS), F32)],
        compiler_params=pltpu.CompilerParams(dimension_semantics=("arbitrary",),
                                             vmem_limit_bytes=VMEM_LIMIT),
        name="out_router",
    )(od, on, x, wt, wb, g_post.reshape(1, D), gt1.reshape(B, 1, D), g_pre.reshape(1, D),
      sc2.reshape(B, 1, D), sh2.reshape(B, 1, D), w_router, b_router.reshape(1, N_EXPERTS))


def _expert_kernel(spare_base, be_ref, tok_ref, tok_next_ref, dst_prev_ref, dst_ref, h2_hbm, w1_ref, b1g_ref, b1l_ref,
                   w2_ref, b2_ref, y_hbm, xbuf, ybuf, xb, w1t, w2t, hh, gsem, ssem):
    i = pl.program_id(0)
    n = pl.num_programs(0)
    bm = xbuf.shape[1] // ROW_TILE

    def tile_rows(t):
        t = t * ROW_TILE
        return pl.ds(t if isinstance(t, int) else pl.multiple_of(t, ROW_TILE), ROW_TILE)
    ff = w2_ref.shape[1]
    slot = i % 2

    def gather_row(idx_ref, s, r):
        return pltpu.make_async_copy(h2_hbm.at[tile_rows(idx_ref[0, 0, r]), :],
                                     xbuf.at[s, tile_rows(r), :], gsem.at[s])

    def scatter_row(idx_ref, s, r):
        return pltpu.make_async_copy(ybuf.at[s, tile_rows(r), :],
                                     y_hbm.at[tile_rows(idx_ref[0, 0, r]), :], ssem.at[s])

    def wait_gather(s):
        pltpu.make_async_copy(h2_hbm.at[pl.ds(0, bm * ROW_TILE), :], xbuf.at[s], gsem.at[s]).wait()

    def wait_scatter(s):
        pltpu.make_async_copy(ybuf.at[s], y_hbm.at[pl.ds(0, bm * ROW_TILE), :], ssem.at[s]).wait()

    @pl.when(i == 0)
    def _():
        ybuf[...] = jnp.zeros(ybuf.shape, F32)
        for r in range(bm):
            pltpu.make_async_copy(ybuf.at[0, tile_rows(r), :], y_hbm.at[tile_rows(spare_base + r), :],
                                  ssem.at[0]).start(priority=r % 2)
        for r in range(bm):
            gather_row(tok_ref, 0, r).start(priority=r % 2)

    @pl.when((i == 0) | (be_ref[i] != be_ref[jnp.maximum(i - 1, 0)]))
    def _():
        cw = 256
        for c in range(w1_ref.shape[2] // cw):
            w1t[c * cw:(c + 1) * cw, :] = w1_ref[0, :, c * cw:(c + 1) * cw].T.astype(BF16)
        for c in range(w2_ref.shape[2] // cw):
            w2t[c * cw:(c + 1) * cw, :] = w2_ref[0, :, c * cw:(c + 1) * cw].T.astype(BF16)

    wait_gather(slot)
    for s8 in range(ROW_TILE):
        xb[:, s8 * 128:(s8 + 1) * 128] = xbuf[slot, pl.ds(s8, bm, stride=ROW_TILE), :].astype(BF16)

    n_lane_tiles = bm // 128

    has_rows = i < be_ref[n]

    def start_gathers():
        for r in range(bm):
            gather_row(tok_next_ref, 1 - slot, r).start(priority=r % 2)

    def start_scatters():
        for r in range(bm):
            scatter_row(dst_prev_ref, 1 - slot, r).start(priority=r % 2)

    @pl.when(has_rows)
    def _():
        start_gathers()
        hh_t = lax.dot_general(w1t[...], xb[...], NT_DIMS, preferred_element_type=F32)
        for j in range(n_lane_tiles):
            hh[j] = hh_t[:, j * 128:(j + 1) * 128]

    @pl.when(jnp.logical_not(has_rows))
    def _():
        start_gathers()

    @pl.when(has_rows)
    def _():
        start_scatters()
        even = jnp.concatenate([hh[j, pl.ds(0, ff, stride=2), :] for j in range(n_lane_tiles)], axis=1)
        odd = jnp.concatenate([hh[j, pl.ds(1, ff, stride=2), :] for j in range(n_lane_tiles)], axis=1)
        glu = jnp.minimum(even + b1g_ref[0], SWIGLU_LIMIT)
        lin = jnp.clip(odd + b1l_ref[0], -SWIGLU_LIMIT, SWIGLU_LIMIT)
        act = glu * (1.0 / (1.0 + jnp.exp(-SWIGLU_ALPHA * glu))) * (lin + 1.0)
        y_t = jnp.dot(w2t[...], act.astype(BF16), preferred_element_type=F32)
        y = y_t.T + b2_ref[0]
        wait_scatter(slot)
        for s8 in range(ROW_TILE):
            ybuf[slot, pl.ds(s8, bm, stride=ROW_TILE), :] = y[:, s8 * 128:(s8 + 1) * 128]

    @pl.when(jnp.logical_not(has_rows))
    def _():
        start_scatters()
        wait_scatter(slot)

    @pl.when(i == n - 1)
    def _():
        wait_scatter(1 - slot)
        for r in range(bm):
            scatter_row(dst_ref, slot, r).start(priority=r % 2)
        wait_scatter(slot)
        wait_gather(1 - slot)


def _experts(h2, blk_expert, n_used, tok_buf, dst_buf, w1, b1, w2, b2, n_rows_out):
    D = w1.shape[1]
    assert D == ROW_TILE * 128 and h2.shape[1] == 128
    n_blocks = blk_expert.shape[0]
    bm = MOE_BM
    F = w2.shape[1]
    b1g = b1[:, 0::2].reshape(N_EXPERTS, F, 1)
    b1l = b1[:, 1::2].reshape(N_EXPERTS, F, 1)
    tok3 = tok_buf.reshape(n_blocks, 1, bm)
    spare = n_rows_out - 2 * bm + jnp.arange(bm, dtype=I32)
    dst3 = jnp.concatenate([spare, dst_buf]).reshape(n_blocks + 1, 1, bm)
    smem_blk = lambda fn: pl.BlockSpec((1, 1, bm), fn, memory_space=pltpu.SMEM)
    grid_spec = pltpu.PrefetchScalarGridSpec(
        num_scalar_prefetch=1,
        grid=(n_blocks,),
        in_specs=[smem_blk(lambda i, be: (i, 0, 0)),
                  smem_blk(lambda i, be: (jnp.minimum(i + 1, n_blocks - 1), 0, 0)),
                  smem_blk(lambda i, be: (i, 0, 0)),
                  smem_blk(lambda i, be: (i + 1, 0, 0)),
                  pl.BlockSpec(memory_space=pl.ANY),
                  pl.BlockSpec((1, D, 2 * F), lambda i, be: (be[i], 0, 0)),
                  pl.BlockSpec((1, F, 1), lambda i, be: (be[i], 0, 0)),
                  pl.BlockSpec((1, F, 1), lambda i, be: (be[i], 0, 0)),
                  pl.BlockSpec((1, F, D), lambda i, be: (be[i], 0, 0)),
                  pl.BlockSpec((1, 1, D), lambda i, be: (be[i], 0, 0))],
        out_specs=pl.BlockSpec(memory_space=pl.ANY),
        scratch_shapes=[pltpu.VMEM((2, bm * ROW_TILE, 128), F32), pltpu.VMEM((2, bm * ROW_TILE, 128), F32),
                        pltpu.VMEM((bm, D), BF16),
                        pltpu.VMEM((2 * F, D), BF16), pltpu.VMEM((D, F), BF16),
                        pltpu.VMEM((bm // 128, 2 * F, 128), F32),
                        pltpu.SemaphoreType.DMA((2,)), pltpu.SemaphoreType.DMA((2,))],
    )
    return pl.pallas_call(
        functools.partial(_expert_kernel, n_rows_out - bm),
        grid_spec=grid_spec,
        out_shape=jax.ShapeDtypeStruct((n_rows_out * ROW_TILE, 128), F32),
        compiler_params=pltpu.CompilerParams(dimension_semantics=("arbitrary",),
                                             vmem_limit_bytes=EXPERT_VMEM_LIMIT),
        name="experts",
    )(jnp.concatenate([blk_expert, n_used.reshape(1)]), tok3, tok3, dst3, dst3, h2, w1, b1g, b1l, w2,
      b2.reshape(N_EXPERTS, 1, D))


def _combine_kernel(y0_ref, y1_ref, y2_ref, y3_ref, gate_ref, x1_ref, gt2_ref, g_ref, o_ref):
    tm = x1_ref.shape[0]
    gates = gate_ref[...]

    def rows(y_ref):
        return jnp.concatenate([y_ref[pl.ds(s, tm, stride=ROW_TILE), :] for s in range(ROW_TILE)], axis=1)

    f = gates[:, 0:1] * rows(y0_ref)
    for k, y_ref in enumerate((y1_ref, y2_ref, y3_ref), start=1):
        f = f + gates[:, k:k + 1] * rows(y_ref)
    o_ref[...] = x1_ref[...] + gt2_ref[0] * (_rms(f) * g_ref[...])


def _combine(y_tok, gates, x1, gt2, g_post, tokens_per_batch):
    T, D = x1.shape
    B = gt2.shape[0]
    tm = min(COMBINE_TM, tokens_per_batch)
    steps_per_batch = tokens_per_batch // tm
    steps = T // tm
    y_spec = lambda k: pl.BlockSpec((tm * ROW_TILE, 128), lambda i: (k * steps + i, 0))
    return pl.pallas_call(
        _combine_kernel,
        grid=(steps,),
        in_specs=[y_spec(0), y_spec(1), y_spec(2), y_spec(3),
                  pl.BlockSpec((tm, TOP_K), lambda i: (i, 0)),
                  pl.BlockSpec((tm, D), lambda i: (i, 0)),
                  pl.BlockSpec((1, 1, D), lambda i: (i // steps_per_batch, 0, 0)),
                  pl.BlockSpec((1, D), lambda i: (0, 0))],
        out_specs=pl.BlockSpec((tm, D), lambda i: (i, 0)),
        out_shape=jax.ShapeDtypeStruct((T, D), F32),
        compiler_params=pltpu.CompilerParams(dimension_semantics=("arbitrary",),
                                             vmem_limit_bytes=VMEM_LIMIT),
        name="combine",
    )(y_tok, y_tok, y_tok, y_tok, gates, x1, gt2.reshape(B, 1, D), g_post.reshape(1, D))


def _inverse_kernel(pad_lo_ref, pad_hi_ref, dest_ref, inv_ref):
    i = pl.program_id(0)
    chunk = dest_ref.shape[2]

    @pl.when(i == 0)
    def _():
        def clear(s, carry):
            inv_ref[s] = 0
            return carry
        for e in range(pad_lo_ref.shape[0]):
            lax.fori_loop(pad_lo_ref[e], pad_hi_ref[e], clear, 0)

    @pl.when(i > 0)
    def _():
        base = (i - 1) * chunk

        def put(g, carry):
            a0 = g * INVERSE_GROUP
            slots = [dest_ref[0, 0, a0 + u] for u in range(INVERSE_GROUP)]
            for u in range(INVERSE_GROUP):
                inv_ref[slots[u]] = base + a0 + u + 1
            return carry
        lax.fori_loop(0, chunk // INVERSE_GROUP, put, 0)


def _inverse_map(dest, pad_lo, pad_hi, n_slots):
    n_assign = dest.shape[0]
    chunk = math.gcd(n_assign, INVERSE_CHUNK)
    n_put = n_assign // chunk
    grid_spec = pltpu.PrefetchScalarGridSpec(
        num_scalar_prefetch=2,
        grid=(1 + n_put,),
        in_specs=[pl.BlockSpec((1, 1, chunk), lambda i, lo, hi: (jnp.maximum(i - 1, 0), 0, 0),
                               memory_space=pltpu.SMEM)],
        out_specs=pl.BlockSpec(memory_space=pltpu.SMEM),
    )
    return pl.pallas_call(
        _inverse_kernel,
        grid_spec=grid_spec,
        out_shape=jax.ShapeDtypeStruct((n_slots,), I32),
        compiler_params=pltpu.CompilerParams(dimension_semantics=("arbitrary",)),
        name="inverse_map",
    )(pad_lo, pad_hi, dest.reshape(n_put, 1, chunk))


def _dispatch_plan(eidx, rank, counts):
    T = eidx.shape[0]
    n_assign = eidx.size
    bm = MOE_BM
    cap = (n_assign + N_EXPERTS * (bm - 1) + bm - 1) // bm * bm
    n_blocks = cap // bm
    counts = counts.reshape(N_EXPERTS).astype(I32)
    padded = (counts + bm - 1) // bm * bm
    pad_end = jnp.cumsum(padded)
    pad_start = pad_end - padded
    eio = jnp.arange(N_EXPERTS, dtype=I32)
    start_of = jnp.sum(jnp.where(eidx[..., None] == eio, pad_start, 0), axis=-1)
    dest = (start_of + rank).reshape(-1)
    blk_start = jnp.arange(n_blocks, dtype=I32) * bm
    blk_expert = jnp.minimum(jnp.sum((blk_start[:, None] >= pad_end[None, :]).astype(I32), axis=-1),
                             N_EXPERTS - 1)
    pad_lo = jnp.concatenate([pad_start + counts, pad_end[-1:]]).astype(I32)
    pad_hi = jnp.concatenate([pad_end, jnp.full((1,), cap, I32)]).astype(I32)
    inv = _inverse_map(dest, pad_lo, pad_hi, cap)
    is_pad = inv == 0
    a = inv - 1
    tok_buf = jnp.where(is_pad, 0, a // TOP_K).astype(I32)
    pad_row = n_assign + jnp.cumsum(is_pad.astype(I32)) - 1
    dst_buf = jnp.where(is_pad, pad_row, (a % TOP_K) * T + a // TOP_K).astype(I32)
    n_used = (pad_end[-1] // bm).astype(I32)
    return blk_expert, n_used, tok_buf, dst_buf, cap + 2 * bm


def _layer(x, c, l, w_ada, b_ada, g_pre_mix, g_post_mix, w_in, w_out, lam_q1, lam_k1, lam_q2, lam_k2,
           g_subln, nat_rpb, g_pre_ffn, g_post_ffn, w_router, b_router, w1, b1, w2, b2):
    B, L, D = x.shape
    lam_init = 0.8 - 0.6 * math.exp(-0.3 * l)
    mod, lam = _ada(c, w_ada, b_ada, lam_q1, lam_k1, lam_q2, lam_k2, lam_init)
    sh1, sc1, gt1, sh2, sc2, gt2 = jnp.split(mod, 6, axis=-1)

    qT, kd, vT, nq, nk, nv = _inproj(x, g_pre_mix, sc1, sh1, w_in)
    o_diff = _diff_attention(qT, kd, vT, lam, g_subln, lam_init)
    o_nat = _nat_attention(nq, nk, nv, nat_rpb)

    T = B * L
    x1, h2, eidx, gates, rank, counts = _out_router(
        o_diff.reshape(T, DIFF_WIDTH), o_nat.reshape(T, NAT_WIDTH), x.reshape(T, D), w_out,
        g_post_mix, gt1, g_pre_ffn, sc2, sh2, w_router, b_router, L)
    blk_expert, n_used, tok_buf, dst_buf, n_rows_out = _dispatch_plan(eidx, rank, counts)
    y_tok = _experts(h2, blk_expert, n_used, tok_buf, dst_buf, w1, b1, w2, b2, n_rows_out)
    out = _combine(y_tok, gates, x1, gt2, g_post_ffn, L)
    return out.reshape(B, L, D)


def kernel(x, c, w_ada, b_ada, g_pre_mix, g_post_mix, w_in, w_out, lam_q1, lam_k1, lam_q2, lam_k2,
           g_subln, nat_rpb, g_pre_ffn, g_post_ffn, w_router, b_router, w1, b1, w2, b2):
    depth = w_ada.shape[0]
    for l in range(depth):
        x = _layer(x, c, l, w_ada[l], b_ada[l], g_pre_mix[l], g_post_mix[l], w_in[l], w_out[l],
                   lam_q1[l], lam_k1[l], lam_q2[l], lam_k2[l], g_subln[l], nat_rpb[l],
                   g_pre_ffn[l], g_post_ffn[l], w_router[l], b_router[l], w1[l], b1[l], w2[l], b2[l])
    return x
```

```python
import functools
import math

import jax
import jax.numpy as jnp
from jax import lax
from jax.experimental import pallas as pl
from jax.experimental.pallas import tpu as pltpu

F32 = jnp.float32
BF16 = jnp.bfloat16
I32 = jnp.int32

HEAD_DIM = 64
N_DIFF_HEADS = 4
DIFF_HEAD_W = 2 * HEAD_DIM
DIFF_WIDTH = N_DIFF_HEADS * DIFF_HEAD_W
N_NAT_HEADS = 8
NAT_WIDTH = N_NAT_HEADS * HEAD_DIM
NAT_PAIRS = N_NAT_HEADS // 2
GRID_W = 64
NAT_KH = 8
NAT_KW = 16
N_EXPERTS = 32
TOP_K = 4
SWIGLU_LIMIT = 7.0
SWIGLU_ALPHA = 1.702
RMS_EPS = 1e-6
NEG_BIG = -1e30
LOG2E = 1.4426950408889634
ROW_TILE = 8

NT_DIMS = (((1,), (1,)), ((), ()))

ADA_TN = 1536
INPROJ_TM = 512
DIFF_TQ = 256
DIFF_TK = 512
DIFF_TICKS_PER_BODY = 3
NAT_GROUP = 8
NAT_KEY_ROWS = 16
OUT_TM = 512
MOE_BM = 256
COMBINE_TM = 512
INVERSE_CHUNK = 4096
INVERSE_GROUP = 16
VMEM_LIMIT = 48 * 1024 * 1024
EXPERT_VMEM_LIMIT = 56 * 1024 * 1024


def _rms(x, axis=-1):
    return x * lax.rsqrt(jnp.mean(x * x, axis=axis, keepdims=True) + RMS_EPS)


def _ada_kernel(lam_init, c_ref, w_ref, b_ref, lq1_ref, lk1_ref, lq2_ref, lk2_ref, mod_ref, lam_ref):
    c = c_ref[...]
    s = c * (1.0 / (1.0 + jnp.exp(-c)))
    mod_ref[...] = jnp.dot(s, w_ref[...], preferred_element_type=F32,
                           precision=lax.Precision.HIGHEST) + b_ref[...]
    d1 = jnp.sum(lq1_ref[...] * lk1_ref[...], axis=-1, keepdims=True)
    d2 = jnp.sum(lq2_ref[...] * lk2_ref[...], axis=-1, keepdims=True)
    lam = jnp.exp(d1) - jnp.exp(d2) + lam_init
    lam_ref[...] = jnp.broadcast_to(lam, lam_ref.shape)


def _ada(c, w_ada, b_ada, lq1, lk1, lq2, lk2, lam_init):
    B, D = c.shape
    N = w_ada.shape[1]
    c8 = jnp.zeros((8, D), F32).at[:B].set(c)
    vec = pl.BlockSpec((1, HEAD_DIM), lambda j: (0, 0))
    mod, lam = pl.pallas_call(
        functools.partial(_ada_kernel, lam_init),
        grid=(N // ADA_TN,),
        in_specs=[pl.BlockSpec((8, D), lambda j: (0, 0)),
                  pl.BlockSpec((D, ADA_TN), lambda j: (0, j)),
                  pl.BlockSpec((1, ADA_TN), lambda j: (0, j)),
                  vec, vec, vec, vec],
        out_specs=[pl.BlockSpec((8, ADA_TN), lambda j: (0, j)),
                   pl.BlockSpec((8, 128), lambda j: (0, 0))],
        out_shape=[jax.ShapeDtypeStruct((8, N), F32), jax.ShapeDtypeStruct((8, 128), F32)],
        compiler_params=pltpu.CompilerParams(dimension_semantics=("arbitrary",),
                                             vmem_limit_bytes=VMEM_LIMIT),
        name="ada",
    )(c8, w_ada, b_ada.reshape(1, N), lq1.reshape(1, -1), lk1.reshape(1, -1),
      lq2.reshape(1, -1), lk2.reshape(1, -1))
    return mod[:B], lam


def _inproj_kernel(x_ref, g_ref, sc_ref, sh_ref, wqT_ref, wvT_ref, wn_ref,
                   qT_ref, k_ref, vT_ref, nq_ref, nk_ref, nv_ref):
    h = _rms(x_ref[0]) * g_ref[...]
    h = h * (1.0 + sc_ref[0]) + sh_ref[0]
    hb = h.astype(BF16)
    qT_ref[0] = lax.dot_general(wqT_ref[...], hb, NT_DIMS, preferred_element_type=F32).astype(BF16)
    vT_ref[0] = lax.dot_general(wvT_ref[...], hb, NT_DIMS, preferred_element_type=F32).astype(BF16)
    rest = jnp.dot(hb, wn_ref[...], preferred_element_type=F32).astype(BF16)
    k_ref[0] = rest[:, 0:512]
    nq_ref[0] = rest[:, 512:1024]
    nk_ref[0] = rest[:, 1024:1536]
    nv_ref[0] = rest[:, 1536:2048]


def _inproj(x, g_pre, sc1, sh1, w_in):
    B, L, D = x.shape
    tm = INPROJ_TM
    scale = HEAD_DIM ** -0.5
    wqT = (w_in[:, 0:512] * (scale * LOG2E)).T.astype(BF16)
    wvT = w_in[:, 1024:1536].T.astype(BF16)
    wn = jnp.concatenate([w_in[:, 512:1024], w_in[:, 1536:2048] * scale, w_in[:, 2048:3072]],
                         axis=1).astype(BF16)
    row_major = pl.BlockSpec((1, tm, 512), lambda b, i: (b, i, 0))
    col_major = pl.BlockSpec((1, 512, tm), lambda b, i: (b, 0, i))
    modv = pl.BlockSpec((1, 1, D), lambda b, i: (b, 0, 0))
    rm_shape = jax.ShapeDtypeStruct((B, L, 512), BF16)
    cm_shape = jax.ShapeDtypeStruct((B, 512, L), BF16)
    return pl.pallas_call(
        _inproj_kernel,
        grid=(B, L // tm),
        in_specs=[pl.BlockSpec((1, tm, D), lambda b, i: (b, i, 0)),
                  pl.BlockSpec((1, D), lambda b, i: (0, 0)),
                  modv, modv,
                  pl.BlockSpec((512, D), lambda b, i: (0, 0)),
                  pl.BlockSpec((512, D), lambda b, i: (0, 0)),
                  pl.BlockSpec((D, 2048), lambda b, i: (0, 0))],
        out_specs=[col_major, row_major, col_major, row_major, row_major, row_major],
        out_shape=[cm_shape, rm_shape, cm_shape, rm_shape, rm_shape, rm_shape],
        compiler_params=pltpu.CompilerParams(dimension_semantics=("arbitrary", "arbitrary"),
                                             vmem_limit_bytes=VMEM_LIMIT),
        name="inproj",
    )(x, g_pre.reshape(1, D), sc1.reshape(B, 1, D), sh1.reshape(B, 1, D), wqT, wvT, wn)


def _diff_kernel(lam_init, n_kchunks, coef_ref, lam_ref, qT_ref, k_ref, kaug_ref, vT_ref, g_ref, o_ref,
                 acc_ref, m_ref, l_ref, s0_ref, qd_ref, qm_ref, cq_ref, s_ref, p_ref, a_ref):
    tq = qT_ref.shape[2]
    tk = s0_ref.shape[0]
    h = pl.program_id(1)
    q0 = pl.program_id(2) * tq
    c_f = coef_ref[4 * h]
    c_parts = (coef_ref[4 * h + 1], coef_ref[4 * h + 2], coef_ref[4 * h + 3])
    c_diag = lax.div(q0, tk)

    qT = qT_ref[0]
    row = lax.broadcasted_iota(I32, qT.shape, 0)
    zero = jnp.zeros_like(qT)
    q_maps = (jnp.where(row < HEAD_DIM, qT, zero), jnp.where(row >= HEAD_DIM, qT, zero))
    aug = jnp.zeros(qT.shape, F32)
    for j in range(3):
        aug = jnp.where(row == j, 64.0 * c_parts[j], aug)
        aug = jnp.where(row == 3 + j, c_parts[j], aug)
    for mi in range(2):
        qd_ref[mi] = q_maps[mi]
        for sg, sign in enumerate((1.0, -1.0)):
            qm_ref[mi, sg, 0:DIFF_HEAD_W, :] = q_maps[mi]
            qm_ref[mi, sg, DIFF_HEAD_W:, :] = (sign * aug).astype(BF16)

    qpos = q0 + lax.broadcasted_iota(I32, (1, tq), 1)
    cq_ref[...] = c_f * qpos.astype(F32)
    kk = lax.broadcasted_iota(I32, (tk, tq), 0)
    qq = lax.broadcasted_iota(I32, (tk, tq), 1)
    s0_ref[...] = c_f * (kk - qq).astype(F32)
    acc_ref[...] = jnp.zeros_like(acc_ref)
    l_ref[...] = jnp.zeros_like(l_ref)
    m_ref[...] = jnp.full_like(m_ref, NEG_BIG)

    def chunk(i):
        j = i - 1
        c = j + (j >= c_diag).astype(I32)
        return c_diag if isinstance(i, int) and i == 0 else jnp.where(i == 0, c_diag, c)

    def start(c):
        return pl.multiple_of(c * tk, tk)

    def update(mi, slot, logits, shift):
        m_old = m_ref[mi]
        m_new = jnp.maximum(m_old, jnp.max(logits, axis=0, keepdims=True) - shift)
        alpha = jnp.exp2(m_old - m_new)
        p = jnp.exp2(logits - (m_new + shift))
        l_ref[mi] = alpha * l_ref[mi] + jnp.sum(p, axis=0, keepdims=True)
        m_ref[mi] = m_new
        a_ref[slot, mi] = alpha
        p_ref[slot, mi] = p.astype(BF16)

    def scores_diag(slot):
        kb = k_ref[0, pl.ds(start(c_diag), tk), :]
        for mi in range(2):
            s_ref[slot, mi] = jnp.dot(kb, qd_ref[mi], preferred_element_type=F32)

    def softmax_diag(slot):
        bias = jnp.abs(s0_ref[...] + c_f * (start(c_diag) - q0).astype(F32))
        for mi in range(2):
            update(mi, slot, s_ref[slot, mi] - bias, 0.0)

    def scores(c, slot):
        k0 = start(c)
        kb = jnp.concatenate([k_ref[0, pl.ds(k0, tk), :], kaug_ref[pl.ds(k0, tk), :]], axis=1)
        sg = (c > c_diag).astype(I32)
        for mi in range(2):
            s_ref[slot, mi] = jnp.dot(kb, qm_ref[mi, sg], preferred_element_type=F32)

    def softmax(c, slot):
        shift = jnp.where(c > c_diag, -1.0, 1.0) * cq_ref[...]
        for mi in range(2):
            update(mi, slot, s_ref[slot, mi], shift)

    def values(c, slot):
        vb = vT_ref[0, :, pl.ds(start(c), tk)]
        for mi in range(2):
            acc_ref[mi] = a_ref[slot, mi] * acc_ref[mi] + jnp.dot(vb, p_ref[slot, mi],
                                                                  preferred_element_type=F32)

    n = n_kchunks
    scores_diag(0)
    scores(chunk(1), 1)
    scores(chunk(2), 2)
    softmax_diag(0)
    scores(chunk(3), 0)
    softmax(chunk(1), 1)

    def ticks(j, carry):
        t0 = DIFF_TICKS_PER_BODY * j + 2
        for k in range(DIFF_TICKS_PER_BODY):
            t = t0 + k
            scores(chunk(t + 2), (k + 1) % 3)
            softmax(chunk(t), (k + 2) % 3)
            values(chunk(t - 2), k % 3)
        return carry

    lax.fori_loop(0, (n - 4) // DIFF_TICKS_PER_BODY, ticks, 0)
    for t in (n - 2, n - 1):
        softmax(chunk(t), t % 3)
        values(chunk(t - 2), (t - 2) % 3)
    values(chunk(n - 2), (n - 2) % 3)
    values(chunk(n - 1), (n - 1) % 3)

    lam = lam_ref[0:1, 0:1]
    o = acc_ref[0] / l_ref[0] - lam * (acc_ref[1] / l_ref[1])
    y = _rms(o, axis=0) * g_ref[...] * (1.0 - lam_init)
    o_ref[0] = y.T.astype(BF16)


def _diff_attention(qT, k, vT, lam, g_sub, lam_init):
    B, _, L = qT.shape
    tq, tk = min(DIFF_TQ, L), min(DIFF_TK, L)
    assert DIFF_TICKS_PER_BODY % 3 == 0 and L // tk >= 4 and (L // tk - 4) % DIFF_TICKS_PER_BODY == 0
    assert tk % tq == 0 and L <= 64 * 256
    i = jnp.arange(1, N_DIFF_HEADS + 1, dtype=F32)
    c = jnp.exp2(-8.0 * i / N_DIFF_HEADS) * LOG2E
    c1 = c.astype(BF16).astype(F32)
    c2 = (c - c1).astype(BF16).astype(F32)
    c3 = (c - c1 - c2).astype(BF16).astype(F32)
    coef = jnp.stack([c, c1, c2, c3], axis=1).reshape(-1)
    kpos = jnp.arange(L, dtype=I32)
    lane = jnp.arange(DIFF_HEAD_W, dtype=I32)
    kaug = jnp.where(lane[None, :] < 3, (kpos // 64)[:, None],
                     jnp.where(lane[None, :] < 6, (kpos % 64)[:, None], 0)).astype(BF16)
    return pl.pallas_call(
        functools.partial(_diff_kernel, lam_init, L // tk),
        grid=(B, N_DIFF_HEADS, L // tq),
        in_specs=[pl.BlockSpec(memory_space=pltpu.SMEM),
                  pl.BlockSpec((8, 128), lambda b, h, i: (0, 0)),
                  pl.BlockSpec((1, DIFF_HEAD_W, tq), lambda b, h, i: (b, h, i)),
                  pl.BlockSpec((1, L, DIFF_HEAD_W), lambda b, h, i: (b, 0, h)),
                  pl.BlockSpec((L, DIFF_HEAD_W), lambda b, h, i: (0, 0)),
                  pl.BlockSpec((1, DIFF_HEAD_W, L), lambda b, h, i: (b, h, 0)),
                  pl.BlockSpec((DIFF_HEAD_W, 1), lambda b, h, i: (0, 0))],
        out_specs=pl.BlockSpec((1, tq, DIFF_HEAD_W), lambda b, h, i: (b, i, h)),
        out_shape=jax.ShapeDtypeStruct((B, L, DIFF_WIDTH), BF16),
        scratch_shapes=[pltpu.VMEM((2, DIFF_HEAD_W, tq), F32),
                        pltpu.VMEM((2, 1, tq), F32),
                        pltpu.VMEM((2, 1, tq), F32),
                        pltpu.VMEM((tk, tq), F32),
                        pltpu.VMEM((2, DIFF_HEAD_W, tq), BF16),
                        pltpu.VMEM((2, 2, 2 * DIFF_HEAD_W, tq), BF16),
                        pltpu.VMEM((1, tq), F32),
                        pltpu.VMEM((3, 2, tk, tq), F32),
                        pltpu.VMEM((3, 2, tk, tq), BF16),
                        pltpu.VMEM((3, 2, 1, tq), F32)],
        compiler_params=pltpu.CompilerParams(
            dimension_semantics=("arbitrary", "arbitrary", "arbitrary"),
            vmem_limit_bytes=VMEM_LIMIT),
        name="diff_attn",
    )(coef, lam, qT, k, kaug, vT, g_sub.reshape(DIFF_HEAD_W, 1))


def _nat_group_span(g, n_rows):
    r0 = g * NAT_GROUP
    lo, hi = 0, n_rows - NAT_KEY_ROWS
    kb0 = r0 - NAT_KH // 2
    kb0 = min(max(kb0, lo), hi) if isinstance(g, int) else jnp.clip(kb0, lo, hi)
    return r0, kb0


def _nat_bias_table(rpb, n_rows):
    c = jnp.arange(GRID_W)
    col_start = jnp.clip(c - NAT_KW // 2, 0, GRID_W - NAT_KW)
    col_in = (c[None, :] >= col_start[:, None]) & (c[None, :] < col_start[:, None] + NAT_KW)
    dc = jnp.clip(c[None, :] - c[:, None], -(NAT_KW - 1), NAT_KW - 1) + (NAT_KW - 1)
    rpb = rpb.astype(F32)
    cols = jnp.zeros(rpb.shape[:2] + (GRID_W, GRID_W), F32)
    for j in range(2 * NAT_KW - 1):
        cols = cols + jnp.where(dc == j, rpb[:, :, j][:, :, None, None], 0.0)
    cols = jnp.where(col_in[None, None], cols, NEG_BIG)
    n_dr = 2 * NAT_KH - 1
    neg = jnp.full_like(cols, NEG_BIG)
    tiles = jnp.concatenate([jnp.concatenate([cols[:, :n_dr - 1], cols[:, 1:]], axis=-1),
                             jnp.concatenate([neg, cols], axis=-1),
                             jnp.concatenate([cols, neg], axis=-1),
                             jnp.concatenate([neg[:, :1], neg[:, :1]], axis=-1)], axis=1)
    right0, left0, none = n_dr - 1, 2 * n_dr - 1, 3 * n_dr - 1
    n_groups = n_rows // NAT_GROUP
    idx = []
    for g in (0, 1, n_groups - 1):
        r0, kb0 = _nat_group_span(min(g, n_groups - 1), n_rows)
        for j in range(NAT_GROUP):
            r = r0 + j
            rs = min(max(r - NAT_KH // 2, 0), n_rows - NAT_KH)
            for m in range(NAT_KEY_ROWS // 2):
                k_l = kb0 + 2 * m
                in_l = rs <= k_l < rs + NAT_KH
                in_r = rs <= k_l + 1 < rs + NAT_KH
                d_l = k_l - r + (NAT_KH - 1)
                idx.append(d_l if in_l and in_r else right0 + d_l + 1 if in_r else left0 + d_l if in_l else none)
    tbl = jnp.take(tiles, jnp.asarray(idx, I32), axis=1)
    return tbl.reshape(NAT_PAIRS, 2, 3, NAT_GROUP * NAT_KEY_ROWS // 2, GRID_W, 2 * GRID_W)


def _nat_kernel(n_rows, q_ref, k_ref, v_ref, bias_ref, o_ref, s_scr, p_scr, l_scr):
    n_groups = n_rows // NAT_GROUP
    g = pl.program_id(2)
    pairs = NAT_KEY_ROWS // 2
    win = NAT_KEY_ROWS * GRID_W

    @pl.when((pl.program_id(0) == 0) & (pl.program_id(1) == 0) & (g == 0))
    def _():
        s_scr[...] = jnp.zeros(s_scr.shape, F32)
        p_scr[...] = jnp.zeros(p_scr.shape, BF16)
        l_scr[...] = jnp.ones(l_scr.shape, F32)

    def window(ref, group):
        _, kb0 = _nat_group_span(group, n_rows)
        return ref[0, pl.ds(pl.multiple_of(kb0 * GRID_W, GRID_W), win), :]

    def step(sa):
        sb = 1 - sa
        q = q_ref[0]
        lane = lax.broadcasted_iota(I32, q.shape, 1)
        zero = jnp.zeros_like(q)
        kw = window(k_ref, jnp.minimum(g, n_groups - 1))
        vw = window(v_ref, jnp.clip(g - 2, 0, n_groups - 1))
        outs = []
        for hh in range(2):
            keep = (lane < HEAD_DIM) if hh == 0 else (lane >= HEAD_DIM)
            s_scr[sa, hh] = lax.dot_general(jnp.where(keep, q, zero), kw, NT_DIMS, preferred_element_type=F32)
            for j in range(NAT_GROUP):
                rows = slice(j * GRID_W, (j + 1) * GRID_W)
                for m in range(pairs):
                    cols = slice(m * 2 * GRID_W, (m + 1) * 2 * GRID_W)
                    s_scr[sb, hh, rows, cols] = s_scr[sb, hh, rows, cols] + bias_ref[0, hh, 0, j * pairs + m]
            s = s_scr[sb, hh]
            p = jnp.exp(s - jnp.max(s, axis=-1, keepdims=True))
            l_scr[sb, hh] = jnp.sum(p, axis=-1, keepdims=True)
            p_scr[sb, hh] = p.astype(BF16)
            outs.append(jnp.dot(p_scr[sa, hh], vw, preferred_element_type=F32) / l_scr[sa, hh])
        o_ref[0] = jnp.where(lane < HEAD_DIM, outs[0], outs[1]).astype(BF16)

    @pl.when(g % 2 == 0)
    def _():
        step(0)

    @pl.when(g % 2 == 1)
    def _():
        step(1)


def _nat_attention(nq, nk, nv, rpb):
    B, L, _ = nq.shape
    n_rows = L // GRID_W
    assert n_rows >= NAT_KEY_ROWS and n_rows % NAT_GROUP == 0
    n_groups = n_rows // NAT_GROUP
    tbl = _nat_bias_table(rpb, n_rows)
    kv = pl.BlockSpec((1, L, 2 * HEAD_DIM), lambda b, p, i: (b, 0, p))
    group_blk = lambda lag: pl.BlockSpec((1, NAT_GROUP * GRID_W, 2 * HEAD_DIM),
                                         lambda b, p, i: (b, jnp.clip(i - lag, 0, n_groups - 1), p))
    kind = lambda grp: jnp.where(grp == 0, 0, jnp.where(grp == n_groups - 1, 2, 1))
    rows, keys = NAT_GROUP * GRID_W, NAT_KEY_ROWS * GRID_W
    return pl.pallas_call(
        functools.partial(_nat_kernel, n_rows),
        grid=(B, NAT_PAIRS, n_groups + 2),
        in_specs=[group_blk(0), kv, kv,
                  pl.BlockSpec((1, 2, 1, NAT_GROUP * NAT_KEY_ROWS // 2, GRID_W, 2 * GRID_W),
                               lambda b, p, i: (p, 0, kind(jnp.clip(i - 1, 0, n_groups - 1)), 0, 0, 0))],
        out_specs=group_blk(2),
        scratch_shapes=[pltpu.VMEM((2, 2, rows, keys), F32), pltpu.VMEM((2, 2, rows, keys), BF16),
                        pltpu.VMEM((2, 2, rows, 1), F32)],
        out_shape=jax.ShapeDtypeStruct((B, L, NAT_WIDTH), BF16),
        compiler_params=pltpu.CompilerParams(
            dimension_semantics=("arbitrary", "arbitrary", "arbitrary"),
            vmem_limit_bytes=VMEM_LIMIT),
        name="nat_attn",
    )(nq, nk, nv, tbl)


def _out_kernel(od_ref, on_ref, x_ref, wt_ref, wb_ref, gpost_ref, gt1_ref, gpre_ref, sc2_ref, sh2_ref,
                wr_ref, br_ref, x1_ref, h2_ref, eidx_ref, gate_ref, rank_ref, cnt_ref, carry_ref):
    tm = x_ref.shape[0]

    @pl.when(pl.program_id(0) == 0)
    def _():
        carry_ref[...] = jnp.zeros_like(carry_ref)

    mix = (jnp.dot(od_ref[...], wt_ref[...], preferred_element_type=F32)
           + jnp.dot(on_ref[...], wb_ref[...], preferred_element_type=F32))
    x1 = x_ref[...] + gt1_ref[0] * (_rms(mix) * gpost_ref[...])
    x1_ref[...] = x1
    h2 = _rms(x1) * gpre_ref[...]
    h2 = h2 * (1.0 + sc2_ref[0]) + sh2_ref[0]
    for s in range(ROW_TILE):
        h2_ref[pl.ds(s, tm, stride=ROW_TILE), :] = h2[:, s * 128:(s + 1) * 128]

    wr = wr_ref[...]
    h_hi = h2.astype(BF16)
    h_lo = (h2 - h_hi.astype(F32)).astype(BF16)
    w_hi = wr.astype(BF16)
    w_lo = (wr - w_hi.astype(F32)).astype(BF16)
    logits = (jnp.dot(h_hi, w_hi, preferred_element_type=F32)
              + (jnp.dot(h_hi, w_lo, preferred_element_type=F32)
                 + jnp.dot(h_lo, w_hi, preferred_element_type=F32))) + br_ref[...]
    eio = lax.broadcasted_iota(I32, logits.shape, 1).astype(F32)
    onehot = jnp.zeros_like(logits)
    vals, idxs, sels = [], [], []
    cur = logits
    for _ in range(TOP_K):
        mx = jnp.max(cur, axis=-1, keepdims=True)
        idx = jnp.min(jnp.where(cur == mx, eio, float(N_EXPERTS)), axis=-1, keepdims=True)
        sel = eio == idx
        vals.append(mx)
        idxs.append(idx)
        sels.append(sel)
        cur = jnp.where(sel, -jnp.inf, cur)
        onehot = onehot + sel.astype(F32)

    ex = [jnp.exp(v - vals[0]) for v in vals]
    tot = ex[0] + ex[1] + ex[2] + ex[3]

    rr = lax.broadcasted_iota(I32, (tm, tm), 0)
    cc = lax.broadcasted_iota(I32, (tm, tm), 1)
    tri = (rr > cc).astype(BF16)
    carry = carry_ref[...]
    cum = jnp.dot(tri, onehot.astype(BF16), preferred_element_type=F32) + carry
    ranks = [jnp.sum(jnp.where(sel, cum, 0.0), axis=-1, keepdims=True) for sel in sels]
    carry = carry + jnp.sum(onehot, axis=0, keepdims=True)
    carry_ref[...] = carry
    cnt_ref[...] = carry

    kio = lax.broadcasted_iota(I32, (tm, TOP_K), 1)

    def pack(cols):
        out = jnp.broadcast_to(cols[TOP_K - 1], (tm, TOP_K))
        for k in range(TOP_K - 2, -1, -1):
            out = jnp.where(kio == k, cols[k], out)
        return out

    eidx_ref[...] = pack(idxs).astype(I32)
    gate_ref[...] = pack([e / tot for e in ex])
    rank_ref[...] = pack(ranks).astype(I32)


def _out_router(od, on, x, w_out, g_post, gt1, g_pre, sc2, sh2, w_router, b_router, tokens_per_batch):
    T, D = x.shape
    B = gt1.shape[0]
    tm = min(OUT_TM, tokens_per_batch)
    steps_per_batch = tokens_per_batch // tm
    wt = w_out[:DIFF_WIDTH].astype(BF16)
    wb = w_out[DIFF_WIDTH:].astype(BF16)
    rowblk = lambda w: pl.BlockSpec((tm, w), lambda i: (i, 0))
    const = lambda shape: pl.BlockSpec(shape, lambda i: (0,) * len(shape))
    modv = pl.BlockSpec((1, 1, D), lambda i: (i // steps_per_batch, 0, 0))
    return pl.pallas_call(
        _out_kernel,
        grid=(T // tm,),
        in_specs=[rowblk(DIFF_WIDTH), rowblk(NAT_WIDTH), rowblk(D),
                  const((DIFF_WIDTH, D)), const((NAT_WIDTH, D)), const((1, D)), modv, const((1, D)),
                  modv, modv, const((D, N_EXPERTS)), const((1, N_EXPERTS))],
        out_specs=[rowblk(D), pl.BlockSpec((tm * ROW_TILE, 128), lambda i: (i, 0)),
                   rowblk(TOP_K), rowblk(TOP_K), rowblk(TOP_K), const((1, N_EXPERTS))],
        out_shape=[jax.ShapeDtypeStruct((T, D), F32), jax.ShapeDtypeStruct((T * ROW_TILE, 128), F32),
                   jax.ShapeDtypeStruct((T, TOP_K), I32), jax.ShapeDtypeStruct((T, TOP_K), F32),
                   jax.ShapeDtypeStruct((T, TOP_K), I32), jax.ShapeDtypeStruct((1, N_EXPERTS), F32)],
        scratch_shapes=[pltpu.VMEM((1, N_EXPERTS), F32)],
        compiler_params=pltpu.CompilerParams(dimension_semantics=("arbitrary",),
                                             vmem_limit_bytes=VMEM_LIMIT),
        name="out_router",
    )(od, on, x, wt, wb, g_post.reshape(1, D), gt1.reshape(B, 1, D), g_pre.reshape(1, D),
      sc2.reshape(B, 1, D), sh2.reshape(B, 1, D), w_router, b_router.reshape(1, N_EXPERTS))


def _expert_kernel(spare_base, be_ref, tok_ref, tok_next_ref, dst_prev_ref, dst_ref, h2_hbm, w1_ref, b1g_ref, b1l_ref,
                   w2_ref, b2_ref, y_hbm, xbuf, ybuf, xb, w1t, w2t, hh, gsem, ssem):
    i = pl.program_id(0)
    n = pl.num_programs(0)
    bm = xbuf.shape[1] // ROW_TILE

    def tile_rows(t):
        t = t * ROW_TILE
        return pl.ds(t if isinstance(t, int) else pl.multiple_of(t, ROW_TILE), ROW_TILE)
    ff = w2_ref.shape[1]
    slot = i % 2

    def gather_row(idx_ref, s, r):
        return pltpu.make_async_copy(h2_hbm.at[tile_rows(idx_ref[0, 0, r]), :],
                                     xbuf.at[s, tile_rows(r), :], gsem.at[s])

    def scatter_row(idx_ref, s, r):
        return pltpu.make_async_copy(ybuf.at[s, tile_rows(r), :],
                                     y_hbm.at[tile_rows(idx_ref[0, 0, r]), :], ssem.at[s])

    def wait_gather(s):
        pltpu.make_async_copy(h2_hbm.at[pl.ds(0, bm * ROW_TILE), :], xbuf.at[s], gsem.at[s]).wait()

    def wait_scatter(s):
        pltpu.make_async_copy(ybuf.at[s], y_hbm.at[pl.ds(0, bm * ROW_TILE), :], ssem.at[s]).wait()

    @pl.when(i == 0)
    def _():
        ybuf[...] = jnp.zeros(ybuf.shape, F32)
        for r in range(bm):
            pltpu.make_async_copy(ybuf.at[0, tile_rows(r), :], y_hbm.at[tile_rows(spare_base + r), :],
                                  ssem.at[0]).start(priority=r % 2)
        for r in range(bm):
            gather_row(tok_ref, 0, r).start(priority=r % 2)

    @pl.when((i == 0) | (be_ref[i] != be_ref[jnp.maximum(i - 1, 0)]))
    def _():
        cw = 256
        for c in range(w1_ref.shape[2] // cw):
            w1t[c * cw:(c + 1) * cw, :] = w1_ref[0, :, c * cw:(c + 1) * cw].T.astype(BF16)
        for c in range(w2_ref.shape[2] // cw):
            w2t[c * cw:(c + 1) * cw, :] = w2_ref[0, :, c * cw:(c + 1) * cw].T.astype(BF16)

    wait_gather(slot)
    for s8 in range(ROW_TILE):
        xb[:, s8 * 128:(s8 + 1) * 128] = xbuf[slot, pl.ds(s8, bm, stride=ROW_TILE), :].astype(BF16)

    n_lane_tiles = bm // 128

    has_rows = i < be_ref[n]

    def start_gathers():
        for r in range(bm):
            gather_row(tok_next_ref, 1 - slot, r).start(priority=r % 2)

    def start_scatters():
        for r in range(bm):
            scatter_row(dst_prev_ref, 1 - slot, r).start(priority=r % 2)

    @pl.when(has_rows)
    def _():
        start_gathers()
        hh_t = lax.dot_general(w1t[...], xb[...], NT_DIMS, preferred_element_type=F32)
        for j in range(n_lane_tiles):
            hh[j] = hh_t[:, j * 128:(j + 1) * 128]

    @pl.when(jnp.logical_not(has_rows))
    def _():
        start_gathers()

    @pl.when(has_rows)
    def _():
        start_scatters()
        even = jnp.concatenate([hh[j, pl.ds(0, ff, stride=2), :] for j in range(n_lane_tiles)], axis=1)
        odd = jnp.concatenate([hh[j, pl.ds(1, ff, stride=2), :] for j in range(n_lane_tiles)], axis=1)
        glu = jnp.minimum(even + b1g_ref[0], SWIGLU_LIMIT)
        lin = jnp.clip(odd + b1l_ref[0], -SWIGLU_LIMIT, SWIGLU_LIMIT)
        act = glu * (1.0 / (1.0 + jnp.exp(-SWIGLU_ALPHA * glu))) * (lin + 1.0)
        y_t = jnp.dot(w2t[...], act.astype(BF16), preferred_element_type=F32)
        y = y_t.T + b2_ref[0]
        wait_scatter(slot)
        for s8 in range(ROW_TILE):
            ybuf[slot, pl.ds(s8, bm, stride=ROW_TILE), :] = y[:, s8 * 128:(s8 + 1) * 128]

    @pl.when(jnp.logical_not(has_rows))
    def _():
        start_scatters()
        wait_scatter(slot)

    @pl.when(i == n - 1)
    def _():
        wait_scatter(1 - slot)
        for r in range(bm):
            scatter_row(dst_ref, slot, r).start(priority=r % 2)
        wait_scatter(slot)
        wait_gather(1 - slot)


def _experts(h2, blk_expert, n_used, tok_buf, dst_buf, w1, b1, w2, b2, n_rows_out):
    D = w1.shape[1]
    assert D == ROW_TILE * 128 and h2.shape[1] == 128
    n_blocks = blk_expert.shape[0]
    bm = MOE_BM
    F = w2.shape[1]
    b1g = b1[:, 0::2].reshape(N_EXPERTS, F, 1)
    b1l = b1[:, 1::2].reshape(N_EXPERTS, F, 1)
    tok3 = tok_buf.reshape(n_blocks, 1, bm)
    spare = n_rows_out - 2 * bm + jnp.arange(bm, dtype=I32)
    dst3 = jnp.concatenate([spare, dst_buf]).reshape(n_blocks + 1, 1, bm)
    smem_blk = lambda fn: pl.BlockSpec((1, 1, bm), fn, memory_space=pltpu.SMEM)
    grid_spec = pltpu.PrefetchScalarGridSpec(
        num_scalar_prefetch=1,
        grid=(n_blocks,),
        in_specs=[smem_blk(lambda i, be: (i, 0, 0)),
                  smem_blk(lambda i, be: (jnp.minimum(i + 1, n_blocks - 1), 0, 0)),
                  smem_blk(lambda i, be: (i, 0, 0)),
                  smem_blk(lambda i, be: (i + 1, 0, 0)),
                  pl.BlockSpec(memory_space=pl.ANY),
                  pl.BlockSpec((1, D, 2 * F), lambda i, be: (be[i], 0, 0)),
                  pl.BlockSpec((1, F, 1), lambda i, be: (be[i], 0, 0)),
                  pl.BlockSpec((1, F, 1), lambda i, be: (be[i], 0, 0)),
                  pl.BlockSpec((1, F, D), lambda i, be: (be[i], 0, 0)),
                  pl.BlockSpec((1, 1, D), lambda i, be: (be[i], 0, 0))],
        out_specs=pl.BlockSpec(memory_space=pl.ANY),
        scratch_shapes=[pltpu.VMEM((2, bm * ROW_TILE, 128), F32), pltpu.VMEM((2, bm * ROW_TILE, 128), F32),
                        pltpu.VMEM((bm, D), BF16),
                        pltpu.VMEM((2 * F, D), BF16), pltpu.VMEM((D, F), BF16),
                        pltpu.VMEM((bm // 128, 2 * F, 128), F32),
                        pltpu.SemaphoreType.DMA((2,)), pltpu.SemaphoreType.DMA((2,))],
    )
    return pl.pallas_call(
        functools.partial(_expert_kernel, n_rows_out - bm),
        grid_spec=grid_spec,
        out_shape=jax.ShapeDtypeStruct((n_rows_out * ROW_TILE, 128), F32),
        compiler_params=pltpu.CompilerParams(dimension_semantics=("arbitrary",),
                                             vmem_limit_bytes=EXPERT_VMEM_LIMIT),
        name="experts",
    )(jnp.concatenate([blk_expert, n_used.reshape(1)]), tok3, tok3, dst3, dst3, h2, w1, b1g, b1l, w2,
      b2.reshape(N_EXPERTS, 1, D))


def _combine_kernel(y0_ref, y1_ref, y2_ref, y3_ref, gate_ref, x1_ref, gt2_ref, g_ref, o_ref):
    tm = x1_ref.shape[0]
    gates = gate_ref[...]

    def rows(y_ref):
        return jnp.concatenate([y_ref[pl.ds(s, tm, stride=ROW_TILE), :] for s in range(ROW_TILE)], axis=1)

    f = gates[:, 0:1] * rows(y0_ref)
    for k, y_ref in enumerate((y1_ref, y2_ref, y3_ref), start=1):
        f = f + gates[:, k:k + 1] * rows(y_ref)
    o_ref[...] = x1_ref[...] + gt2_ref[0] * (_rms(f) * g_ref[...])


def _combine(y_tok, gates, x1, gt2, g_post, tokens_per_batch):
    T, D = x1.shape
    B = gt2.shape[0]
    tm = min(COMBINE_TM, tokens_per_batch)
    steps_per_batch = tokens_per_batch // tm
    steps = T // tm
    y_spec = lambda k: pl.BlockSpec((tm * ROW_TILE, 128), lambda i: (k * steps + i, 0))
    return pl.pallas_call(
        _combine_kernel,
        grid=(steps,),
        in_specs=[y_spec(0), y_spec(1), y_spec(2), y_spec(3),
                  pl.BlockSpec((tm, TOP_K), lambda i: (i, 0)),
                  pl.BlockSpec((tm, D), lambda i: (i, 0)),
                  pl.BlockSpec((1, 1, D), lambda i: (i // steps_per_batch, 0, 0)),
                  pl.BlockSpec((1, D), lambda i: (0, 0))],
        out_specs=pl.BlockSpec((tm, D), lambda i: (i, 0)),
        out_shape=jax.ShapeDtypeStruct((T, D), F32),
        compiler_params=pltpu.CompilerParams(dimension_semantics=("arbitrary",),
                                             vmem_limit_bytes=VMEM_LIMIT),
        name="combine",
    )(y_tok, y_tok, y_tok, y_tok, gates, x1, gt2.reshape(B, 1, D), g_post.reshape(1, D))


def _inverse_kernel(pad_lo_ref, pad_hi_ref, dest_ref, inv_ref):
    i = pl.program_id(0)
    chunk = dest_ref.shape[2]

    @pl.when(i == 0)
    def _():
        def clear(s, carry):
            inv_ref[s] = 0
            return carry
        for e in range(pad_lo_ref.shape[0]):
            lax.fori_loop(pad_lo_ref[e], pad_hi_ref[e], clear, 0)

    @pl.when(i > 0)
    def _():
        base = (i - 1) * chunk

        def put(g, carry):
            a0 = g * INVERSE_GROUP
            slots = [dest_ref[0, 0, a0 + u] for u in range(INVERSE_GROUP)]
            for u in range(INVERSE_GROUP):
                inv_ref[slots[u]] = base + a0 + u + 1
            return carry
        lax.fori_loop(0, chunk // INVERSE_GROUP, put, 0)


def _inverse_map(dest, pad_lo, pad_hi, n_slots):
    n_assign = dest.shape[0]
    chunk = math.gcd(n_assign, INVERSE_CHUNK)
    n_put = n_assign // chunk
    grid_spec = pltpu.PrefetchScalarGridSpec(
        num_scalar_prefetch=2,
        grid=(1 + n_put,),
        in_specs=[pl.BlockSpec((1, 1, chunk), lambda i, lo, hi: (jnp.maximum(i - 1, 0), 0, 0),
                               memory_space=pltpu.SMEM)],
        out_specs=pl.BlockSpec(memory_space=pltpu.SMEM),
    )
    return pl.pallas_call(
        _inverse_kernel,
        grid_spec=grid_spec,
        out_shape=jax.ShapeDtypeStruct((n_slots,), I32),
        compiler_params=pltpu.CompilerParams(dimension_semantics=("arbitrary",)),
        name="inverse_map",
    )(pad_lo, pad_hi, dest.reshape(n_put, 1, chunk))


def _dispatch_plan(eidx, rank, counts):
    T = eidx.shape[0]
    n_assign = eidx.size
    bm = MOE_BM
    cap = (n_assign + N_EXPERTS * (bm - 1) + bm - 1) // bm * bm
    n_blocks = cap // bm
    counts = counts.reshape(N_EXPERTS).astype(I32)
    padded = (counts + bm - 1) // bm * bm
    pad_end = jnp.cumsum(padded)
    pad_start = pad_end - padded
    eio = jnp.arange(N_EXPERTS, dtype=I32)
    start_of = jnp.sum(jnp.where(eidx[..., None] == eio, pad_start, 0), axis=-1)
    dest = (start_of + rank).reshape(-1)
    blk_start = jnp.arange(n_blocks, dtype=I32) * bm
    blk_expert = jnp.minimum(jnp.sum((blk_start[:, None] >= pad_end[None, :]).astype(I32), axis=-1),
                             N_EXPERTS - 1)
    pad_lo = jnp.concatenate([pad_start + counts, pad_end[-1:]]).astype(I32)
    pad_hi = jnp.concatenate([pad_end, jnp.full((1,), cap, I32)]).astype(I32)
    inv = _inverse_map(dest, pad_lo, pad_hi, cap)
    is_pad = inv == 0
    a = inv - 1
    tok_buf = jnp.where(is_pad, 0, a // TOP_K).astype(I32)
    pad_row = n_assign + jnp.cumsum(is_pad.astype(I32)) - 1
    dst_buf = jnp.where(is_pad, pad_row, (a % TOP_K) * T + a // TOP_K).astype(I32)
    n_used = (pad_end[-1] // bm).astype(I32)
    return blk_expert, n_used, tok_buf, dst_buf, cap + 2 * bm


def _layer(x, c, l, w_ada, b_ada, g_pre_mix, g_post_mix, w_in, w_out, lam_q1, lam_k1, lam_q2, lam_k2,
           g_subln, nat_rpb, g_pre_ffn, g_post_ffn, w_router, b_router, w1, b1, w2, b2):
    B, L, D = x.shape
    lam_init = 0.8 - 0.6 * math.exp(-0.3 * l)
    mod, lam = _ada(c, w_ada, b_ada, lam_q1, lam_k1, lam_q2, lam_k2, lam_init)
    sh1, sc1, gt1, sh2, sc2, gt2 = jnp.split(mod, 6, axis=-1)

    qT, kd, vT, nq, nk, nv = _inproj(x, g_pre_mix, sc1, sh1, w_in)
    o_diff = _diff_attention(qT, kd, vT, lam, g_subln, lam_init)
    o_nat = _nat_attention(nq, nk, nv, nat_rpb)

    T = B * L
    x1, h2, eidx, gates, rank, counts = _out_router(
        o_diff.reshape(T, DIFF_WIDTH), o_nat.reshape(T, NAT_WIDTH), x.reshape(T, D), w_out,
        g_post_mix, gt1, g_pre_ffn, sc2, sh2, w_router, b_router, L)
    blk_expert, n_used, tok_buf, dst_buf, n_rows_out = _dispatch_plan(eidx, rank, counts)
    y_tok = _experts(h2, blk_expert, n_used, tok_buf, dst_buf, w1, b1, w2, b2, n_rows_out)
    out = _combine(y_tok, gates, x1, gt2, g_post_ffn, L)
    return out.reshape(B, L, D)


def kernel(x, c, w_ada, b_ada, g_pre_mix, g_post_mix, w_in, w_out, lam_q1, lam_k1, lam_q2, lam_k2,
           g_subln, nat_rpb, g_pre_ffn, g_post_ffn, w_router, b_router, w1, b1, w2, b2):
    depth = w_ada.shape[0]
    for l in range(depth):
        x = _layer(x, c, l, w_ada[l], b_ada[l], g_pre_mix[l], g_post_mix[l], w_in[l], w_out[l],
                   lam_q1[l], lam_k1[l], lam_q2[l], lam_k2[l], g_subln[l], nat_rpb[l],
                   g_pre_ffn[l], g_post_ffn[l], w_router[l], b_router[l], w1[l], b1[l], w2[l], b2[l])
    return x
```
